```python
import math
import jax
import jax.numpy as jnp
from jax import lax
import numpy as np

D_MODEL = 2048
BATCH = 2
SEQ = 4096
DEPTH = 2
DEC_BATCH = 32
DEC_SEQ = 8
PAST_LEN = 8192
PAGE_SIZE = 128

N_EVEN = (DEPTH + 1) // 2
N_ODD = DEPTH // 2
NORM_EPS = 1e-6
F32 = jnp.float32

DA_HEADS = 8
DA_QK_DIM = 64
DA_V_DIM = 2 * DA_QK_DIM
DA_SCALE = DA_QK_DIM ** -0.5
ROT_DIM = DA_QK_DIM // 4
ROPE_THETA = 500000.0
Q_BLOCK = 128

GDN_HEADS = 8
GDN_DK = 128
GDN_DV = 128
GDN_QKV = GDN_HEADS * (2 * GDN_DK + GDN_DV)
GDN_CONV = 4
GDN_CHUNK = 64

HG_EXPAND = 128
HG_HEADS = D_MODEL // HG_EXPAND
HG_DI = D_MODEL // HG_HEADS
HG_F = HG_HEADS * HG_EXPAND
HG_I = HG_HEADS * HG_DI
HG_CHUNK = 64

D_FF = -(-(8 * D_MODEL) // (3 * 256)) * 256

EVEN_SPLITS = (DA_HEADS * 2 * DA_QK_DIM, DA_HEADS * 2 * DA_QK_DIM, DA_HEADS * DA_V_DIM,
               GDN_QKV, GDN_HEADS * GDN_DV, GDN_HEADS, GDN_HEADS)
EVEN_IN = sum(EVEN_SPLITS)
EVEN_OUT = DA_HEADS * DA_V_DIM + GDN_HEADS * GDN_DV
ODD_SPLITS = (HG_F, HG_F, HG_I, HG_I)
ODD_IN = sum(ODD_SPLITS)

kernel_name = 'hybrid_diffattn_gdn_hgrn2_step'


def _rmsnorm(x, w):
    xf = x.astype(F32)
    y = xf * lax.rsqrt(jnp.mean(xf * xf, axis=-1, keepdims=True) + NORM_EPS)
    return (y * w.astype(F32)).astype(x.dtype)


def _l2norm(x):
    xf = x.astype(F32)
    return xf * lax.rsqrt(jnp.sum(xf * xf, axis=-1, keepdims=True) + NORM_EPS)


def _split(t, sizes):
    return jnp.split(t, np.cumsum(sizes)[:-1].tolist(), axis=-1)


def _rope(x, pos):
    half = ROT_DIM // 2
    inv = ROPE_THETA ** (-jnp.arange(half, dtype=F32) * 2.0 / ROT_DIM)
    ang = pos.astype(F32)[:, None] * inv[None, :]
    cos = jnp.cos(ang)[None, :, None, None, :]
    sin = jnp.sin(ang)[None, :, None, None, :]
    xf = x.astype(F32)
    x1, x2, xp = xf[..., :half], xf[..., half:ROT_DIM], xf[..., ROT_DIM:]
    return jnp.concatenate([x1 * cos - x2 * sin, x2 * cos + x1 * sin, xp], axis=-1).astype(x.dtype)


def _diff_probs(scores, lam):
    p = jax.nn.softmax(scores, axis=-1)
    return p[:, :, 0] - lam * p[:, :, 1]


def _diff_attn_prompt(q, k, v, lam):
    B, S = q.shape[:2]
    nb = S // Q_BLOCK
    qb = jnp.moveaxis(q.reshape(B, nb, Q_BLOCK, DA_HEADS, 2, DA_QK_DIM), 1, 0)
    kpos = jnp.arange(S)

    def block(args):
        qi, start = args
        s = jnp.einsum('bqhmd,bkhmd->bhmqk', qi, k, preferred_element_type=F32) * DA_SCALE
        qpos = start + jnp.arange(Q_BLOCK)
        s = jnp.where(kpos[None, :] <= qpos[:, None], s, -jnp.inf)
        p = _diff_probs(s, lam)
        return jnp.einsum('bhqk,bkhv->bqhv', p.astype(v.dtype), v)

    o = lax.map(block, (qb, jnp.arange(nb) * Q_BLOCK))
    return jnp.moveaxis(o, 0, 1).reshape(B, S, DA_HEADS, DA_V_DIM)


def _diff_attn_sample(q, k_new, v_new, cache_k, cache_v, layer, page_table, lam):
    DB, Q = q.shape[:2]
    k_past = cache_k[layer, page_table].reshape(DB, -1, DA_HEADS, 2, DA_QK_DIM)
    v_past = cache_v[layer, page_table].reshape(DB, -1, DA_HEADS, DA_V_DIM)
    P = k_past.shape[1]
    s_past = jnp.einsum('bqhmd,bkhmd->bhmqk', q, k_past, preferred_element_type=F32) * DA_SCALE
    s_new = jnp.einsum('bqhmd,bkhmd->bhmqk', q, k_new, preferred_element_type=F32) * DA_SCALE
    s_new = jnp.where(jnp.tril(jnp.ones((Q, Q), bool)), s_new, -jnp.inf)
    p = _diff_probs(jnp.concatenate([s_past, s_new], axis=-1), lam).astype(v_new.dtype)
    return (jnp.einsum('bhqk,bkhv->bqhv', p[..., :P], v_past)
            + jnp.einsum('bhqk,bkhv->bqhv', p[..., P:], v_new))


def _pad_len(t, Lp):
    return jnp.pad(t, [(0, 0), (0, Lp - t.shape[1])] + [(0, 0)] * (t.ndim - 2))


def _chunkify(t, C):
    B, L, H, D = t.shape
    return t.reshape(B, L // C, C, H, D).transpose(1, 0, 3, 2, 4)


def _gated_delta_chunked(q, k, v, g, beta, s0):
    B, L, H, _ = q.shape
    C = min(GDN_CHUNK, L)
    Lp = -(-L // C) * C
    q, k, v, g, beta = (_chunkify(_pad_len(t.astype(F32), Lp), C)
                        for t in (q, k, v, g[..., None], beta[..., None]))
    g, beta = g[..., 0], beta[..., 0]
    b = jnp.cumsum(g, axis=-1)
    causal = jnp.tril(jnp.ones((C, C), bool))
    strict = jnp.tril(jnp.ones((C, C), bool), -1)
    decay = jnp.exp(jnp.where(causal, b[..., :, None] - b[..., None, :], -jnp.inf))
    kb = k * beta[..., None]
    m = jnp.where(strict, jnp.einsum('nbhtd,nbhsd->nbhts', kb, k) * decay, 0.0)
    t_mat = m + jnp.eye(C, dtype=F32)
    u = lax.linalg.triangular_solve(t_mat, v * beta[..., None], left_side=True, lower=True)
    w = lax.linalg.triangular_solve(t_mat, kb * jnp.exp(b)[..., None], left_side=True, lower=True)
    attn = jnp.einsum('nbhtd,nbhsd->nbhts', q, k) * decay
    qg = q * jnp.exp(b)[..., None]
    kd = k * jnp.exp(b[..., -1:] - b)[..., None]
    gl = jnp.exp(b[..., -1])

    def step(S, xs):
        qg_c, a_c, w_c, u_c, kd_c, gl_c = xs
        v_new = u_c - jnp.einsum('bhtk,bhkv->bhtv', w_c, S)
        o = jnp.einsum('bhtk,bhkv->bhtv', qg_c, S) + jnp.einsum('bhts,bhsv->bhtv', a_c, v_new)
        S = S * gl_c[..., None, None] + jnp.einsum('bhtk,bhtv->bhkv', kd_c, v_new)
        return S, o

    S, o = lax.scan(step, s0.astype(F32), (qg, attn, w, u, kd, gl))
    o = o.transpose(1, 0, 3, 2, 4).reshape(B, Lp, H, -1)[:, :L]
    return o, S


def _gla_chunked(q, k, v, logf, s0):
    B, L, H, _ = q.shape
    C = min(HG_CHUNK, L)
    Lp = -(-L // C) * C
    xs = tuple(_chunkify(_pad_len(t.astype(F32), Lp), C) for t in (q, k, v, logf))
    causal = jnp.tril(jnp.ones((C, C), bool))[:, :, None]

    def step(S, xs_c):
        qc, kc, vc, lc = xs_c
        bc = jnp.cumsum(lc, axis=2)
        dec = jnp.exp(jnp.where(causal, bc[:, :, :, None, :] - bc[:, :, None, :, :], -jnp.inf))
        a = jnp.einsum('bhtk,bhsk,bhtsk->bhts', qc, kc, dec)
        o = jnp.einsum('bhtk,bhkv->bhtv', qc * jnp.exp(bc), S) + jnp.einsum('bhts,bhsv->bhtv', a, vc)
        blast = bc[:, :, -1:]
        S = S * jnp.exp(blast[:, :, 0, :, None]) + jnp.einsum('bhsk,bhsv->bhkv', kc * jnp.exp(blast - bc), vc)
        return S, o

    S, o = lax.scan(step, s0.astype(F32), xs)
    o = o.transpose(1, 0, 3, 2, 4).reshape(B, Lp, H, -1)[:, :L]
    return o, S


def _causal_conv(x, prev, w):
    L = x.shape[1]
    xx = jnp.concatenate([prev.astype(x.dtype), x], axis=1)
    y = xx[:, 0:L] * w[0]
    for j in range(1, GDN_CONV):
        y = y + xx[:, j:j + L] * w[j]
    return y, xx[:, L:]


def _gdn(qkv, z, a, bta, conv_w, conv_prev, a_log, dt_bias, norm_w, s0):
    B, L, _ = qkv.shape
    qkv, conv_new = _causal_conv(qkv, conv_prev, conv_w)
    q, k, v = _split(jax.nn.silu(qkv), (GDN_HEADS * GDN_DK, GDN_HEADS * GDN_DK, GDN_HEADS * GDN_DV))
    q = _l2norm(q.reshape(B, L, GDN_HEADS, GDN_DK)) * GDN_DK ** -0.5
    k = _l2norm(k.reshape(B, L, GDN_HEADS, GDN_DK))
    v = v.reshape(B, L, GDN_HEADS, GDN_DV)
    g = -jnp.exp(a_log.astype(F32)) * jax.nn.softplus(a.astype(F32) + dt_bias.astype(F32))
    beta = jax.nn.sigmoid(bta.astype(F32))
    o, s_new = _gated_delta_chunked(q, k, v, g, beta, s0)
    o = _rmsnorm(o, norm_w) * jax.nn.silu(z.astype(F32).reshape(B, L, GDN_HEADS, GDN_DV))
    return o.reshape(B, L, -1).astype(qkv.dtype), conv_new, s_new


def _even_mixer(h, pos, w_in, w_out, da_lambda, lam_init, da_subln, conv_w, a_log, dt_bias,
                gdn_norm, conv_prev, s0, attend):
    B, L, _ = h.shape
    dq, dk, dv, gqkv, gz, ga, gb = _split(h @ w_in, EVEN_SPLITS)
    dq = _rope(dq.reshape(B, L, DA_HEADS, 2, DA_QK_DIM), pos)
    dk = _rope(dk.reshape(B, L, DA_HEADS, 2, DA_QK_DIM), pos)
    dv = dv.reshape(B, L, DA_HEADS, DA_V_DIM)
    lf = da_lambda.astype(F32)
    lam = jnp.exp(jnp.sum(lf[0] * lf[1])) - jnp.exp(jnp.sum(lf[2] * lf[3])) + lam_init
    oa = _rmsnorm(attend(dq, dk, dv, lam), da_subln) * (1.0 - lam_init)
    ob, conv_new, s_new = _gdn(gqkv, gz, ga, gb, conv_w, conv_prev, a_log, dt_bias, gdn_norm, s0)
    y = jnp.concatenate([oa.reshape(B, L, -1).astype(h.dtype), ob], axis=-1) @ w_out
    return y, dk, dv, conv_new, s_new


def _odd_mixer(h, w_in, w_out, lb, hg_norm, s0):
    B, L, _ = h.shape
    q, f, i, g = _split(h @ w_in, ODD_SPLITS)
    qf = jax.nn.silu(q.astype(F32)).reshape(B, L, HG_HEADS, HG_EXPAND) * HG_EXPAND ** -0.5
    fg = lb + (1.0 - lb) * jax.nn.sigmoid(f.astype(F32))
    logf = jnp.log(fg).reshape(B, L, HG_HEADS, HG_EXPAND)
    kk = (1.0 - fg).reshape(B, L, HG_HEADS, HG_EXPAND)
    iv = i.astype(F32).reshape(B, L, HG_HEADS, HG_DI)
    o, s_new = _gla_chunked(qf, kk, iv, logf, s0)
    o = _rmsnorm(o, hg_norm.reshape(HG_HEADS, HG_DI)) * jax.nn.silu(g.astype(F32)).reshape(B, L, HG_HEADS, HG_DI)
    return o.reshape(B, L, D_MODEL).astype(h.dtype) @ w_out, s_new


def _swiglu(h, w_gate_up, w_down):
    gate, up = jnp.split(h @ w_gate_up, 2, axis=-1)
    return (jax.nn.silu(gate) * up) @ w_down


def setup_inputs(seed: int = 0) -> dict:
    key = jax.random.key(seed)
    ks = iter(jax.random.split(key, 40))

    def nrm(shape, scale):
        return jax.random.normal(next(ks), shape, F32) * scale

    n_pages = PAST_LEN // PAGE_SIZE
    n_used = DEC_BATCH * n_pages
    n_phys = n_used + max(1, n_used // 4)
    page_table = jax.random.permutation(next(ks), n_phys)[:n_used].reshape(DEC_BATCH, n_pages).astype(jnp.int32)
    dt = jnp.exp(jax.random.uniform(next(ks), (N_EVEN, GDN_HEADS), F32, math.log(1e-3), math.log(1e-1)))
    dt_bias = dt + jnp.log(-jnp.expm1(-dt))
    a_log = jnp.log(jax.random.uniform(next(ks), (N_EVEN, GDN_HEADS), F32, 1.0, 16.0))
    return {
        'x_prompt': nrm((BATCH, SEQ, D_MODEL), 1.0),
        'x_sample': nrm((DEC_BATCH, DEC_SEQ, D_MODEL), 1.0),
        'cache_k': nrm((N_EVEN, n_phys, PAGE_SIZE, DA_HEADS, 2, DA_QK_DIM), 1.0),
        'cache_v': nrm((N_EVEN, n_phys, PAGE_SIZE, DA_HEADS, DA_V_DIM), 1.0),
        'state_gdn_conv': nrm((N_EVEN, DEC_BATCH, GDN_CONV - 1, GDN_QKV), 1.0),
        'state_gdn': nrm((N_EVEN, DEC_BATCH, GDN_HEADS, GDN_DK, GDN_DV), 0.3),
        'state_hgrn': nrm((N_ODD, DEC_BATCH, HG_HEADS, HG_EXPAND, HG_DI), 1.0),
        'page_table': page_table,
        'norm_mix': 1.0 + nrm((DEPTH, D_MODEL), 0.02),
        'norm_ffn': 1.0 + nrm((DEPTH, D_MODEL), 0.02),
        'norm_final': 1.0 + nrm((D_MODEL,), 0.02),
        'w_in_even': nrm((N_EVEN, D_MODEL, EVEN_IN), D_MODEL ** -0.5),
        'w_out_even': nrm((N_EVEN, EVEN_OUT, D_MODEL), EVEN_OUT ** -0.5),
        'da_lambda': nrm((N_EVEN, 4, DA_QK_DIM), 0.1),
        'da_subln': 1.0 + nrm((N_EVEN, DA_V_DIM), 0.02),
        'gdn_conv_w': nrm((N_EVEN, GDN_CONV, GDN_QKV), GDN_CONV ** -0.5),
        'gdn_a_log': a_log,
        'gdn_dt_bias': dt_bias,
        'gdn_norm': 1.0 + nrm((N_EVEN, GDN_DV), 0.02),
        'w_in_odd': nrm((N_ODD, D_MODEL, ODD_IN), D_MODEL ** -0.5),
        'w_out_odd': nrm((N_ODD, HG_I, D_MODEL), HG_I ** -0.5),
        'hg_lower': nrm((DEPTH, HG_F), 0.5),
        'hg_norm': 1.0 + nrm((N_ODD, HG_I), 0.02),
        'w_gate_up': nrm((DEPTH, D_MODEL, 2 * D_FF), D_MODEL ** -0.5),
        'w_down': nrm((DEPTH, D_FF, D_MODEL), D_FF ** -0.5),
    }


def reference(x_prompt, x_sample, cache_k, cache_v, state_gdn_conv, state_gdn, state_hgrn, page_table,
              norm_mix, norm_ffn, norm_final, w_in_even, w_out_even, da_lambda, da_subln, gdn_conv_w,
              gdn_a_log, gdn_dt_bias, gdn_norm, w_in_odd, w_out_odd, hg_lower, hg_norm, w_gate_up, w_down):
    Bp, Lp_len = x_prompt.shape[:2]
    past_len = page_table.shape[1] * PAGE_SIZE
    pos_p = jnp.arange(Lp_len)
    pos_s = past_len + jnp.arange(x_sample.shape[1])
    lb_all = jnp.cumsum(jax.nn.softmax(hg_lower.astype(F32), axis=0), axis=0)
    lb_all = lb_all - lb_all[0]

    hp, hs = x_prompt, x_sample
    k_p, v_p, conv_p, gdn_p, hg_p = [], [], [], [], []
    k_s, v_s, conv_s, gdn_s, hg_s = [], [], [], [], []
    for l in range(DEPTH):
        if l % 2 == 0:
            e = l // 2
            lam_init = 0.8 - 0.6 * math.exp(-0.3 * l)
            y, kr, vr, cv, st = _even_mixer(
                _rmsnorm(hp, norm_mix[l]), pos_p, w_in_even[e], w_out_even[e], da_lambda[e], lam_init,
                da_subln[e], gdn_conv_w[e], gdn_a_log[e], gdn_dt_bias[e], gdn_norm[e],
                jnp.zeros((Bp, GDN_CONV - 1, GDN_QKV), hp.dtype),
                jnp.zeros((Bp, GDN_HEADS, GDN_DK, GDN_DV), F32), _diff_attn_prompt)
            hp = hp + y
            k_p.append(kr); v_p.append(vr); conv_p.append(cv); gdn_p.append(st.astype(hp.dtype))
            attend = lambda q, k, v, lam, e=e: _diff_attn_sample(q, k, v, cache_k, cache_v, e, page_table, lam)
            y, kr, vr, cv, st = _even_mixer(
                _rmsnorm(hs, norm_mix[l]), pos_s, w_in_even[e], w_out_even[e], da_lambda[e], lam_init,
                da_subln[e], gdn_conv_w[e], gdn_a_log[e], gdn_dt_bias[e], gdn_norm[e],
                state_gdn_conv[e], state_gdn[e], attend)
            hs = hs + y
            k_s.append(kr); v_s.append(vr); conv_s.append(cv); gdn_s.append(st.astype(hs.dtype))
        else:
            o = l // 2
            y, st = _odd_mixer(_rmsnorm(hp, norm_mix[l]), w_in_odd[o], w_out_odd[o], lb_all[l], hg_norm[o],
                               jnp.zeros((Bp, HG_HEADS, HG_EXPAND, HG_DI), F32))
            hp = hp + y
            hg_p.append(st.astype(hp.dtype))
            y, st = _odd_mixer(_rmsnorm(hs, norm_mix[l]), w_in_odd[o], w_out_odd[o], lb_all[l], hg_norm[o],
                               state_hgrn[o])
            hs = hs + y
            hg_s.append(st.astype(hs.dtype))
        hp = hp + _swiglu(_rmsnorm(hp, norm_ffn[l]), w_gate_up[l], w_down[l])
        hs = hs + _swiglu(_rmsnorm(hs, norm_ffn[l]), w_gate_up[l], w_down[l])

    y_prompt = _rmsnorm(hp, norm_final)
    y_sample = _rmsnorm(hs, norm_final)
    return (y_prompt, y_sample,
            jnp.stack(k_p), jnp.stack(v_p), jnp.stack(conv_p), jnp.stack(gdn_p), jnp.stack(hg_p),
            jnp.stack(k_s), jnp.stack(v_s), jnp.stack(conv_s), jnp.stack(gdn_s), jnp.stack(hg_s))
```

```python
import functools
import math

import numpy as np
import jax
import jax.numpy as jnp
from jax import lax
from jax.experimental import pallas as pl
from jax.experimental.pallas import tpu as pltpu

F32 = jnp.float32
BF16 = jnp.bfloat16

D_MODEL = 2048
DEPTH = 2
PAGE_SIZE = 128
NORM_EPS = 1e-6

DA_HEADS = 8
DA_QK_DIM = 64
DA_V_DIM = 2 * DA_QK_DIM
DA_SCALE = DA_QK_DIM ** -0.5
ROT_DIM = DA_QK_DIM // 4
ROPE_THETA = 500000.0

GDN_HEADS = 8
GDN_DK = 128
GDN_DV = 128
GDN_QKV = GDN_HEADS * (2 * GDN_DK + GDN_DV)
GDN_CONV = 4
GDN_CHUNK = 64

HG_EXPAND = 128
HG_HEADS = D_MODEL // HG_EXPAND
HG_DI = D_MODEL // HG_HEADS
HG_CHUNK = 64

D_FF = -(-(8 * D_MODEL) // (3 * 256)) * 256

DA_W = DA_HEADS * 2 * DA_QK_DIM
EVEN_MAIN = 3 * DA_W + GDN_QKV + GDN_HEADS * GDN_DV
ODD_IN = 4 * D_MODEL

LANES = 128
SUBLANES = 8
VMEM_LIMIT = 56 * 1024 * 1024
NEG_BIG = -1e30


def _cparams(*sem):
    return pltpu.CompilerParams(dimension_semantics=sem, vmem_limit_bytes=VMEM_LIMIT)


def _bdot(a, b):
    return jnp.dot(a.astype(BF16), b.astype(BF16), preferred_element_type=F32)


def _bdot_nt(a, b):
    return lax.dot_general(a.astype(BF16), b.astype(BF16), (((1,), (1,)), ((), ())),
                           preferred_element_type=F32)


def _split2(x):
    hi = x.astype(BF16)
    lo = (x - hi.astype(F32)).astype(BF16)
    return hi, lo


def _dot3(a, b):
    ah, al = _split2(a)
    bh, bl = _split2(b)
    d = functools.partial(jnp.dot, preferred_element_type=F32)
    return d(ah, bh) + (d(ah, bl) + d(al, bh))


def _cumsum_rows(tri, x):
    hi = x.astype(BF16)
    r = x - hi.astype(F32)
    mid = r.astype(BF16)
    lo = (r - mid.astype(F32)).astype(BF16)
    d = functools.partial(jnp.dot, preferred_element_type=F32)
    return d(tri, hi) + (d(tri, mid) + d(tri, lo))


def _sigmoid(x):
    return 1.0 / (1.0 + jnp.exp(-x))


def _silu(x):
    return x * _sigmoid(x)


def _rms_rows(x, w):
    return x * lax.rsqrt(jnp.mean(x * x, axis=-1, keepdims=True) + NORM_EPS) * w


def _inproj_even_kernel(x_ref, nw_ref, w_ref, wab_ref, cos_ref, sina_ref, sinb_ref,
                        o_ref, ab_ref, xn_ref, *, n_rope_tiles, tn):
    j = pl.program_id(1)

    @pl.when(j == 0)
    def _():
        xn = _rms_rows(x_ref[...], nw_ref[...]).astype(BF16)
        xn_ref[...] = xn
        ab_ref[...] = jnp.dot(xn, wab_ref[...].astype(BF16), preferred_element_type=F32)

    acc = jnp.dot(xn_ref[...], w_ref[...].astype(BF16), preferred_element_type=F32)

    @pl.when(j < n_rope_tiles)
    def _():
        cosf, sina, sinb = cos_ref[...], sina_ref[...], sinb_ref[...]
        for c in range(tn // LANES):
            a = acc[:, c * LANES:(c + 1) * LANES]
            o_ref[:, c * LANES:(c + 1) * LANES] = (
                a * cosf + pltpu.roll(a, LANES - ROT_DIM // 2, 1) * sina
                + pltpu.roll(a, ROT_DIM // 2, 1) * sinb)

    @pl.when(j >= n_rope_tiles)
    def _():
        o_ref[...] = acc


def _rope_tables(pos):
    half = ROT_DIM // 2
    inv = np.power(np.float32(ROPE_THETA), -np.arange(half, dtype=np.float32) * np.float32(2.0) / np.float32(ROT_DIM))
    ang = (pos.astype(np.float32)[:, None] * inv[None, :]).astype(np.float32)
    cos = np.cos(ang.astype(np.float64)).astype(np.float32)
    sin = np.sin(ang.astype(np.float64)).astype(np.float32)
    n = pos.shape[0]
    cosf = np.ones((n, LANES), np.float32)
    sina = np.zeros((n, LANES), np.float32)
    sinb = np.zeros((n, LANES), np.float32)
    for base in range(0, LANES, DA_QK_DIM):
        cosf[:, base:base + half] = cos
        cosf[:, base + half:base + 2 * half] = cos
        sina[:, base:base + half] = -sin
        sinb[:, base + half:base + 2 * half] = sin
    return jnp.asarray(cosf), jnp.asarray(sina), jnp.asarray(sinb)


def _inproj_even(x, nw, w_in_all, e, wab, pos, tm, tn=512):
    M = x.shape[0]
    cosf, sina, sinb = _rope_tables(pos)
    n_tiles = EVEN_MAIN // tn
    kern = functools.partial(_inproj_even_kernel, n_rope_tiles=(2 * DA_W) // tn, tn=tn)
    row = lambda i, j: (i, 0)
    return pl.pallas_call(
        kern,
        grid=(M // tm, n_tiles),
        in_specs=[
            pl.BlockSpec((tm, D_MODEL), row),
            pl.BlockSpec((1, D_MODEL), lambda i, j: (0, 0)),
            pl.BlockSpec((None, D_MODEL, tn), lambda i, j: (e, 0, j)),
            pl.BlockSpec((D_MODEL, LANES), lambda i, j: (0, 0)),
            pl.BlockSpec((tm, LANES), row),
            pl.BlockSpec((tm, LANES), row),
            pl.BlockSpec((tm, LANES), row),
        ],
        out_specs=[pl.BlockSpec((tm, tn), lambda i, j: (i, j)),
                   pl.BlockSpec((tm, LANES), row)],
        out_shape=[jax.ShapeDtypeStruct((M, EVEN_MAIN), F32),
                   jax.ShapeDtypeStruct((M, LANES), F32)],
        scratch_shapes=[pltpu.VMEM((tm, D_MODEL), BF16)],
        compiler_params=_cparams("parallel", "arbitrary"),
        name="inproj_even",
    )(x, nw, w_in_all, wab, cosf, sina, sinb)


def _norm_matmul_kernel(x_ref, nw_ref, w_ref, o_ref, xn_ref):
    @pl.when(pl.program_id(1) == 0)
    def _():
        xn_ref[...] = _rms_rows(x_ref[...], nw_ref[...]).astype(BF16)

    o_ref[...] = jnp.dot(xn_ref[...], w_ref[...].astype(BF16), preferred_element_type=F32)


def _norm_matmul(x, nw, w_all, l, tm, tn=512):
    M = x.shape[0]
    N = w_all.shape[-1]
    return pl.pallas_call(
        _norm_matmul_kernel,
        grid=(M // tm, N // tn),
        in_specs=[
            pl.BlockSpec((tm, D_MODEL), lambda i, j: (i, 0)),
            pl.BlockSpec((1, D_MODEL), lambda i, j: (0, 0)),
            pl.BlockSpec((None, D_MODEL, tn), lambda i, j: (l, 0, j)),
        ],
        out_specs=pl.BlockSpec((tm, tn), lambda i, j: (i, j)),
        out_shape=jax.ShapeDtypeStruct((M, N), F32),
        scratch_shapes=[pltpu.VMEM((tm, D_MODEL), BF16)],
        compiler_params=_cparams("parallel", "arbitrary"),
        name="norm_matmul",
    )(x, nw, w_all)


def _ffn_up_kernel(x_ref, nw_ref, wg_ref, wu_ref, o_ref, xn_ref):
    @pl.when(pl.program_id(1) == 0)
    def _():
        xn_ref[...] = _rms_rows(x_ref[...], nw_ref[...]).astype(BF16)

    xn = xn_ref[...]
    g = jnp.dot(xn, wg_ref[...].astype(BF16), preferred_element_type=F32)
    u = jnp.dot(xn, wu_ref[...].astype(BF16), preferred_element_type=F32)
    o_ref[...] = (_silu(g) * u).astype(BF16)


def _ffn_up(x, nw, w_gate_up, l, tm, tn=512):
    M = x.shape[0]
    nj = D_FF // tn
    return pl.pallas_call(
        _ffn_up_kernel,
        grid=(M // tm, nj),
        in_specs=[
            pl.BlockSpec((tm, D_MODEL), lambda i, j: (i, 0)),
            pl.BlockSpec((1, D_MODEL), lambda i, j: (0, 0)),
            pl.BlockSpec((None, D_MODEL, tn), lambda i, j: (l, 0, j)),
            pl.BlockSpec((None, D_MODEL, tn), lambda i, j: (l, 0, j + nj)),
        ],
        out_specs=pl.BlockSpec((tm, tn), lambda i, j: (i, j)),
        out_shape=jax.ShapeDtypeStruct((M, D_FF), BF16),
        scratch_shapes=[pltpu.VMEM((tm, D_MODEL), BF16)],
        compiler_params=_cparams("parallel", "arbitrary"),
        name="ffn_up",
    )(x, nw, w_gate_up, w_gate_up)


def _mm_res_kernel(a_ref, w_ref, r_ref, o_ref, acc_ref, *, nk):
    k = pl.program_id(2)
    p = jnp.dot(a_ref[...].astype(BF16), w_ref[...].astype(BF16), preferred_element_type=F32)
    if nk == 1:
        o_ref[...] = r_ref[...] + p
    else:
        @pl.when(k == 0)
        def _():
            acc_ref[...] = p

        @pl.when(jnp.logical_and(k > 0, k < nk - 1))
        def _():
            acc_ref[...] += p

        @pl.when(k == nk - 1)
        def _():
            o_ref[...] = r_ref[...] + (acc_ref[...] + p)


def _mm_res(a, w_all, l, res, tm, tn=512, tk=None):
    M, K = a.shape
    N = w_all.shape[-1]
    tk = K if tk is None else tk
    nk = K // tk
    return pl.pallas_call(
        functools.partial(_mm_res_kernel, nk=nk),
        grid=(M // tm, N // tn, nk),
        in_specs=[
            pl.BlockSpec((tm, tk), lambda i, j, k: (i, k)),
            pl.BlockSpec((None, tk, tn), lambda i, j, k: (l, k, j)),
            pl.BlockSpec((tm, tn), lambda i, j, k: (i, j)),
        ],
        out_specs=pl.BlockSpec((tm, tn), lambda i, j, k: (i, j)),
        out_shape=jax.ShapeDtypeStruct((M, N), F32),
        scratch_shapes=[pltpu.VMEM((tm, tn), F32)],
        compiler_params=_cparams("parallel", "parallel", "arbitrary"),
        name="matmul_residual",
    )(a, w_all, res)


def _final_norm_kernel(x_ref, w_ref, o_ref):
    o_ref[...] = _rms_rows(x_ref[...], w_ref[...])


def _final_norm(x, w, tm):
    M = x.shape[0]
    return pl.pallas_call(
        _final_norm_kernel,
        grid=(M // tm,),
        in_specs=[pl.BlockSpec((tm, D_MODEL), lambda i: (i, 0)),
                  pl.BlockSpec((1, D_MODEL), lambda i: (0, 0))],
        out_specs=pl.BlockSpec((tm, D_MODEL), lambda i: (i, 0)),
        out_shape=jax.ShapeDtypeStruct((M, D_MODEL), F32),
        compiler_params=_cparams("parallel"),
        name="final_norm",
    )(x, w)


def _lambda_of(lam_ref, lam_init):
    lf = lam_ref[...]
    s1 = jnp.sum(lf[0:1] * lf[1:2], axis=-1, keepdims=True)
    s2 = jnp.sum(lf[2:3] * lf[3:4], axis=-1, keepdims=True)
    return jnp.exp(s1) - jnp.exp(s2) + lam_init


def _attn_prompt_kernel(lam_ref, q_ref, k_ref, v_ref, sub_ref, o_ref, m_ref, l_ref, acc_ref,
                        *, tq, tk, lam_init):
    qi = pl.program_id(2)
    ki = pl.program_id(3)

    @pl.when(ki == 0)
    def _():
        m_ref[...] = jnp.full(m_ref.shape, NEG_BIG, F32)
        l_ref[...] = jnp.zeros(l_ref.shape, F32)
        acc_ref[...] = jnp.zeros(acc_ref.shape, F32)

    @pl.when(ki <= qi)
    def _():
        q = q_ref[...]
        k = k_ref[...]
        v = v_ref[...].astype(BF16)
        rows = qi * tq + lax.broadcasted_iota(jnp.int32, (tq, 1), 0)
        cols = ki * tk + lax.broadcasted_iota(jnp.int32, (1, tk), 1)
        causal = cols <= rows
        for m in range(2):
            sl = slice(m * DA_QK_DIM, (m + 1) * DA_QK_DIM)
            s = _bdot_nt(q[:, sl] * DA_SCALE, k[:, sl])
            s = jnp.where(causal, s, NEG_BIG)
            m_old = m_ref[m]
            m_new = jnp.maximum(m_old, jnp.max(s, axis=-1, keepdims=True))
            alpha = jnp.exp(m_old - m_new)
            p = jnp.exp(s - m_new)
            l_ref[m] = alpha * l_ref[m] + jnp.sum(p, axis=-1, keepdims=True)
            acc_ref[m] = alpha * acc_ref[m] + jnp.dot(p.astype(BF16), v, preferred_element_type=F32)
            m_ref[m] = m_new

    @pl.when(ki == qi)
    def _():
        lam = _lambda_of(lam_ref, lam_init)
        o = acc_ref[0] / l_ref[0] - lam * (acc_ref[1] / l_ref[1])
        o_ref[...] = _rms_rows(o, sub_ref[...]) * (1.0 - lam_init)


def _attn_prompt(proj, da_lambda_e, subln_e, B, S, lam_init, tq=512):
    tk = tq
    nq = S // tq
    kern = functools.partial(_attn_prompt_kernel, tq=tq, tk=tk, lam_init=lam_init)
    return pl.pallas_call(
        kern,
        grid=(B, DA_HEADS, nq, nq),
        in_specs=[
            pl.BlockSpec((4, DA_QK_DIM), lambda b, h, qi, ki: (0, 0)),
            pl.BlockSpec((tq, LANES), lambda b, h, qi, ki: (b * nq + qi, h)),
            pl.BlockSpec((tk, LANES), lambda b, h, qi, ki: (b * nq + jnp.minimum(ki, qi), DA_HEADS + h)),
            pl.BlockSpec((tk, LANES), lambda b, h, qi, ki: (b * nq + jnp.minimum(ki, qi), 2 * DA_HEADS + h)),
            pl.BlockSpec((1, DA_V_DIM), lambda b, h, qi, ki: (0, 0)),
        ],
        out_specs=pl.BlockSpec((tq, LANES), lambda b, h, qi, ki: (b * nq + qi, h)),
        out_shape=jax.ShapeDtypeStruct((B * S, DA_W), F32),
        scratch_shapes=[pltpu.VMEM((2, tq, 1), F32), pltpu.VMEM((2, tq, 1), F32),
                        pltpu.VMEM((2, tq, DA_V_DIM), F32)],
        compiler_params=_cparams("parallel", "parallel", "parallel", "arbitrary"),
        name="diff_attn_prompt",
    )(da_lambda_e, proj, proj, proj, subln_e)


def _attn_sample_kernel(pt_ref, lam_ref, qbd_ref, kn_ref, vn_ref, sub_ref, *rest,
                        G, n_steps, dec_seq, lam_init):
    k_refs = rest[:G]
    v_refs = rest[G:2 * G]
    o_ref = rest[2 * G]
    st_ref, mx_ref, psum_ref, oacc_ref, kpad_ref, vpad_ref = rest[2 * G + 1:]
    t = pl.program_id(1)
    n_pages = n_steps * G
    new_row = n_pages * PAGE_SIZE

    @pl.when(t == 0)
    def _():
        mx_ref[...] = jnp.full(mx_ref.shape, NEG_BIG, F32)
        kpad_ref[...] = jnp.zeros(kpad_ref.shape, F32)
        vpad_ref[...] = jnp.zeros(vpad_ref.shape, F32)
        kpad_ref[0:dec_seq, :] = kn_ref[...]
        vpad_ref[0:dec_seq, :] = vn_ref[...]

    def scores(k_page, row0, mask=None):
        s = jnp.dot(k_page.astype(BF16), qbd_ref[...], preferred_element_type=F32)
        if mask is not None:
            s = jnp.where(mask, s, NEG_BIG)
        st_ref[pl.ds(row0, PAGE_SIZE), :] = s
        mx_ref[...] = jnp.maximum(mx_ref[...], jnp.max(s, axis=0, keepdims=True))

    @pl.when(t < n_steps)
    def _():
        for g in range(G):
            scores(k_refs[g][...], pl.multiple_of((t * G + g) * PAGE_SIZE, PAGE_SIZE))

    @pl.when(t == n_steps - 1)
    def _():
        key = lax.broadcasted_iota(jnp.int32, (PAGE_SIZE, LANES), 0)
        qry = lax.broadcasted_iota(jnp.int32, (PAGE_SIZE, LANES), 1) % dec_seq
        scores(kpad_ref[...], new_row, mask=key <= qry)

    @pl.when(t == n_steps)
    def _():
        psum_ref[...] = jnp.zeros(psum_ref.shape, F32)
        oacc_ref[...] = jnp.zeros(oacc_ref.shape, F32)

    def accumulate(v_page, row0):
        p = jnp.exp(st_ref[pl.ds(row0, PAGE_SIZE), :] - mx_ref[...])
        psum_ref[...] += p
        oacc_ref[...] += jnp.dot(p.T.astype(BF16), v_page.astype(BF16), preferred_element_type=F32)

    @pl.when(t >= n_steps)
    def _():
        for g in range(G):
            accumulate(v_refs[g][...], pl.multiple_of(((t - n_steps) * G + g) * PAGE_SIZE, PAGE_SIZE))

    @pl.when(t == 2 * n_steps - 1)
    def _():
        accumulate(vpad_ref[...], new_row)
        lam = _lambda_of(lam_ref, lam_init)
        denom = jnp.sum(psum_ref[...].T, axis=-1, keepdims=True)
        rows_per_head = 2 * dec_seq
        for h in range(DA_HEADS):
            r0 = h * rows_per_head
            blk = oacc_ref[r0:r0 + rows_per_head, h * DA_V_DIM:(h + 1) * DA_V_DIM]
            n = blk / denom[r0:r0 + rows_per_head]
            o = n[0:dec_seq] - lam * n[dec_seq:rows_per_head]
            o_ref[:, h * DA_V_DIM:(h + 1) * DA_V_DIM] = _rms_rows(o, sub_ref[...]) * (1.0 - lam_init)


def _attn_sample(proj_s, cache_k, cache_v, e, page_table, da_lambda_e, subln_e, DB, dec_seq, lam_init, G=4):
    n_pages = page_table.shape[1]
    n_steps = n_pages // G
    n_cols = DA_HEADS * 2 * dec_seq
    assert n_cols == LANES and n_pages % G == 0
    q = proj_s[:, :DA_W].reshape(DB, dec_seq, DA_HEADS * 2, DA_QK_DIM).transpose(0, 2, 3, 1)
    eye = jnp.eye(DA_HEADS * 2, dtype=F32)
    qbd = (q[:, :, :, None, :] * eye[None, :, None, :, None]).reshape(DB, DA_W, n_cols)
    qbd = (qbd * DA_SCALE).astype(BF16)
    ck = cache_k.reshape(cache_k.shape[0], cache_k.shape[1], PAGE_SIZE, DA_W)
    cv = cache_v.reshape(cache_v.shape[0], cache_v.shape[1], PAGE_SIZE, DA_W)
    pt = page_table.reshape(-1)

    def k_map(g):
        return lambda b, t, pt: (e, pt[b * n_pages + jnp.minimum(t, n_steps - 1) * G + g], 0, 0)

    def v_map(g):
        return lambda b, t, pt: (e, pt[b * n_pages + jnp.maximum(t - n_steps, 0) * G + g], 0, 0)

    page_block = (None, None, PAGE_SIZE, DA_W)
    kern = functools.partial(_attn_sample_kernel, G=G, n_steps=n_steps, dec_seq=dec_seq, lam_init=lam_init)
    grid_spec = pltpu.PrefetchScalarGridSpec(
        num_scalar_prefetch=1,
        grid=(DB, 2 * n_steps),
        in_specs=[
            pl.BlockSpec((4, DA_QK_DIM), lambda b, t, pt: (0, 0)),
            pl.BlockSpec((None, DA_W, n_cols), lambda b, t, pt: (b, 0, 0)),
            pl.BlockSpec((dec_seq, DA_W), lambda b, t, pt: (b, 1)),
            pl.BlockSpec((dec_seq, DA_W), lambda b, t, pt: (b, 2)),
            pl.BlockSpec((1, DA_V_DIM), lambda b, t, pt: (0, 0)),
        ] + [pl.BlockSpec(page_block, k_map(g)) for g in range(G)]
          + [pl.BlockSpec(page_block, v_map(g)) for g in range(G)],
        out_specs=pl.BlockSpec((dec_seq, DA_W), lambda b, t, pt: (b, 0)),
        scratch_shapes=[
            pltpu.VMEM(((n_pages + 1) * PAGE_SIZE, n_cols), F32),
            pltpu.VMEM((1, n_cols), F32),
            pltpu.VMEM((PAGE_SIZE, n_cols), F32),
            pltpu.VMEM((n_cols, DA_W), F32),
            pltpu.VMEM((PAGE_SIZE, DA_W), F32),
            pltpu.VMEM((PAGE_SIZE, DA_W), F32),
        ],
    )
    return pl.pallas_call(
        kern,
        grid_spec=grid_spec,
        out_shape=jax.ShapeDtypeStruct((DB * dec_seq, DA_W), F32),
        compiler_params=_cparams("parallel", "arbitrary"),
        name="diff_attn_sample",
    )(pt, da_lambda_e, qbd, proj_s, proj_s, subln_e, *([ck] * G), *([cv] * G))


def _tri_inverse(m_strict, C):
    ti = lax.broadcasted_iota(jnp.int32, (C, C), 0)
    sj = lax.broadcasted_iota(jnp.int32, (C, C), 1)
    eye = (ti == sj).astype(F32)
    a = jnp.where(ti // SUBLANES == sj // SUBLANES, -m_strict, 0.0)
    x = eye + a
    p = _dot3(a, a)
    x = x + _dot3(x, p)
    p = _dot3(p, p)
    x = x + _dot3(x, p)
    b = SUBLANES
    while b < C:
        join = jnp.logical_and(ti // (2 * b) == sj // (2 * b), ti // b != sj // b)
        c = jnp.where(join, m_strict, 0.0)
        x = x - _dot3(x, _dot3(c, x))
        b *= 2
    return x


def _gdn_kernel(*refs, C, has_state):
    if has_state:
        (q_ref, k_ref, v_ref, z_ref, ab_ref, cw_ref, alog_ref, dtb_ref, nw_ref,
         prev_ref, s0_ref, o_ref, sout_ref, buf_ref, s_ref) = refs
    else:
        (q_ref, k_ref, v_ref, z_ref, ab_ref, cw_ref, alog_ref, dtb_ref, nw_ref,
         o_ref, sout_ref, buf_ref, s_ref) = refs
    ci = pl.program_id(1)
    W = GDN_HEADS * GDN_DK

    @pl.when(ci == 0)
    def _():
        if has_state:
            buf_ref[0:SUBLANES, :] = prev_ref[...]
            s_ref[...] = s0_ref[...]
        else:
            buf_ref[0:SUBLANES, :] = jnp.zeros((SUBLANES, GDN_QKV), F32)
            s_ref[...] = jnp.zeros(s_ref.shape, F32)

    buf_ref[SUBLANES:SUBLANES + C, 0:W] = q_ref[...]
    buf_ref[SUBLANES:SUBLANES + C, W:2 * W] = k_ref[...]
    buf_ref[SUBLANES:SUBLANES + C, 2 * W:3 * W] = v_ref[...]
    y = buf_ref[SUBLANES:SUBLANES + C, :] * cw_ref[GDN_CONV - 1:GDN_CONV, :]
    for back in range(1, GDN_CONV):
        y = y + buf_ref[pl.ds(SUBLANES - back, C), :] * cw_ref[GDN_CONV - 1 - back:GDN_CONV - back, :]
    tail = buf_ref[C:C + SUBLANES, :]
    buf_ref[0:SUBLANES, :] = tail
    y = _silu(y)

    ab = ab_ref[...]
    sp = ab + dtb_ref[...]
    softplus = jnp.maximum(sp, 0.0) + jnp.log(1.0 + jnp.exp(-jnp.abs(sp)))
    g_all = -jnp.exp(alog_ref[...]) * softplus
    beta_all = _sigmoid(ab)

    ti = lax.broadcasted_iota(jnp.int32, (C, C), 0)
    sj = lax.broadcasted_iota(jnp.int32, (C, C), 1)
    causal = sj <= ti
    strict = sj < ti
    eye = ti == sj
    tri = causal.astype(BF16)
    bc_all = _cumsum_rows(tri, g_all)

    for h in range(GDN_HEADS):
        hs = slice(h * GDN_DK, (h + 1) * GDN_DK)
        qh = y[:, hs]
        kh = y[:, W + h * GDN_DK:W + (h + 1) * GDN_DK]
        vh = y[:, 2 * W + h * GDN_DV:2 * W + (h + 1) * GDN_DV]
        qh = qh * lax.rsqrt(jnp.sum(qh * qh, axis=-1, keepdims=True) + NORM_EPS) * (GDN_DK ** -0.5)
        kh = kh * lax.rsqrt(jnp.sum(kh * kh, axis=-1, keepdims=True) + NORM_EPS)
        bcol = bc_all[:, h:h + 1]
        beta = beta_all[:, GDN_HEADS + h:GDN_HEADS + h + 1]
        brow = jnp.sum(jnp.where(eye, bcol, 0.0), axis=0, keepdims=True)
        dec = jnp.exp(jnp.where(causal, bcol - brow, NEG_BIG))
        kb = kh * beta
        eb = jnp.exp(bcol)
        m_strict = jnp.where(strict, _bdot_nt(kb, kh) * dec, 0.0)
        tinv = _tri_inverse(m_strict, C)
        uw = _bdot(tinv, jnp.concatenate([vh * beta, kb * eb], axis=1))
        u, w = uw[:, :GDN_DV], uw[:, GDN_DV:]
        attn = _bdot_nt(qh, kh) * dec
        blast = bcol[C - 1:C, :]
        kd = kh * jnp.exp(blast - bcol)
        s = s_ref[h]
        ws = _bdot(jnp.concatenate([w, qh * eb], axis=0), s)
        v_new = u - ws[:C]
        o = ws[C:] + _bdot(attn, v_new)
        s_ref[h] = s * jnp.exp(blast) + _bdot(kd.T, v_new)
        o_ref[:, hs] = _rms_rows(o, nw_ref[...]) * _silu(z_ref[:, hs])

    @pl.when(ci == pl.num_programs(1) - 1)
    def _():
        sout_ref[...] = s_ref[...]


def _gdn(proj, ab, conv_w_e, alog_pad, dtb_pad, norm_e, B, L, C, prev8=None, s0=None):
    has_state = prev8 is not None
    nc = L // C
    W = GDN_HEADS * GDN_DK
    base = (3 * DA_W) // W
    row = lambda b, c: (b * nc + c, 0)
    in_specs = [
        pl.BlockSpec((C, W), lambda b, c: (b * nc + c, base)),
        pl.BlockSpec((C, W), lambda b, c: (b * nc + c, base + 1)),
        pl.BlockSpec((C, W), lambda b, c: (b * nc + c, base + 2)),
        pl.BlockSpec((C, W), lambda b, c: (b * nc + c, base + 3)),
        pl.BlockSpec((C, LANES), row),
        pl.BlockSpec((GDN_CONV, GDN_QKV), lambda b, c: (0, 0)),
        pl.BlockSpec((1, LANES), lambda b, c: (0, 0)),
        pl.BlockSpec((1, LANES), lambda b, c: (0, 0)),
        pl.BlockSpec((1, GDN_DV), lambda b, c: (0, 0)),
    ]
    args = [proj, proj, proj, proj, ab, conv_w_e, alog_pad, dtb_pad, norm_e]
    state_block = (None, GDN_HEADS, GDN_DK, GDN_DV)
    if has_state:
        in_specs += [pl.BlockSpec((None, SUBLANES, GDN_QKV), lambda b, c: (b, 0, 0)),
                     pl.BlockSpec(state_block, lambda b, c: (b, 0, 0, 0))]
        args += [prev8, s0]
    return pl.pallas_call(
        functools.partial(_gdn_kernel, C=C, has_state=has_state),
        grid=(B, nc),
        in_specs=in_specs,
        out_specs=[pl.BlockSpec((C, W), row),
                   pl.BlockSpec(state_block, lambda b, c: (b, 0, 0, 0))],
        out_shape=[jax.ShapeDtypeStruct((B * L, W), F32),
                   jax.ShapeDtypeStruct((B, GDN_HEADS, GDN_DK, GDN_DV), F32)],
        scratch_shapes=[pltpu.VMEM((SUBLANES + C, GDN_QKV), F32),
                        pltpu.VMEM((GDN_HEADS, GDN_DK, GDN_DV), F32)],
        compiler_params=_cparams("parallel", "arbitrary"),
        name="gdn_state" if has_state else "gdn_fresh",
    )(*args)


def _hgrn_kernel(*refs, C, HB, layer, has_state):
    if has_state:
        (q_ref, f_ref, i_ref, g_ref, lower_ref, nw_ref, s0_ref, o_ref, sout_ref, st_ref) = refs
    else:
        (q_ref, f_ref, i_ref, g_ref, lower_ref, nw_ref, o_ref, sout_ref, st_ref) = refs
    ci = pl.program_id(2)

    @pl.when(ci == 0)
    def _():
        for h in range(HB):
            st_ref[h] = s0_ref[h].T if has_state else jnp.zeros((HG_DI, HG_EXPAND), F32)

    low = lower_ref[...]
    ex = jnp.exp(low - jnp.max(low, axis=0, keepdims=True))
    lb = jnp.sum(ex[1:layer + 1], axis=0, keepdims=True) / jnp.sum(ex, axis=0, keepdims=True)

    ri = lax.broadcasted_iota(jnp.int32, (C, 1), 0)
    ti = lax.broadcasted_iota(jnp.int32, (C, C), 0)
    sj = lax.broadcasted_iota(jnp.int32, (C, C), 1)
    tri = (sj <= ti).astype(BF16)
    levels = [b for b in (2 * SUBLANES, 4 * SUBLANES, 8 * SUBLANES) if b <= C]

    for h in range(HB):
        hs = slice(h * HG_EXPAND, (h + 1) * HG_EXPAND)
        qt = _silu(q_ref[:, hs]) * (HG_EXPAND ** -0.5)
        lbh = lb[:, hs]
        fg = lbh + (1.0 - lbh) * _sigmoid(f_ref[:, hs])
        logf = jnp.log(fg)
        kk = 1.0 - fg
        v = i_ref[:, hs]
        bc = _cumsum_rows(tri, logf)
        st = st_ref[h]
        o = _bdot_nt(qt * jnp.exp(bc), st)
        if levels:
            a = jnp.zeros((C, C), F32)
            for b in levels:
                ref = jnp.concatenate(
                    [jnp.broadcast_to(bc[m:m + 1, :], (b, HG_EXPAND)) for m in range(b // 2, C, b)], axis=0)
                upper = (ri % b) >= (b // 2)
                eq = jnp.exp(jnp.where(upper, bc - ref, 0.0))
                ek = jnp.exp(jnp.where(upper, 0.0, ref - bc))
                qs = jnp.where(upper, qt * eq, 0.0)
                ks = jnp.where(upper, 0.0, kk * ek)
                a = a + jnp.where(ti // b == sj // b, _bdot_nt(qs, ks), 0.0)
            o = o + _bdot(a, v)
        r8 = ri % SUBLANES
        o = o + jnp.sum(qt * kk, axis=-1, keepdims=True) * v
        for d in range(1, SUBLANES):
            valid = r8 >= d
            ks = pltpu.roll(kk, d, 0)
            bs = pltpu.roll(bc, d, 0)
            vs = pltpu.roll(v, d, 0)
            e = jnp.exp(jnp.where(valid, bc - bs, 0.0))
            wgt = jnp.where(valid, jnp.sum(qt * ks * e, axis=-1, keepdims=True), 0.0)
            o = o + wgt * vs
        blast = bc[C - 1:C, :]
        st_ref[h] = st * jnp.exp(blast) + _bdot(v.T, kk * jnp.exp(blast - bc))
        o_ref[:, hs] = _rms_rows(o, nw_ref[:, hs]) * _silu(g_ref[:, hs])

    @pl.when(ci == pl.num_programs(2) - 1)
    def _():
        for h in range(HB):
            sout_ref[h] = st_ref[h].T


def _hgrn(proj, hg_lower, hg_norm_o, layer, B, L, C, s0=None, HB=4):
    has_state = s0 is not None
    nc = L // C
    nhg = HG_HEADS // HB
    Wb = HB * HG_EXPAND

    def col(section):
        return lambda b, hg, c: (b * nc + c, section * nhg + hg)

    in_specs = [pl.BlockSpec((C, Wb), col(s)) for s in range(4)] + [
        pl.BlockSpec((DEPTH, Wb), lambda b, hg, c: (0, hg)),
        pl.BlockSpec((1, Wb), lambda b, hg, c: (0, hg)),
    ]
    args = [proj, proj, proj, proj, hg_lower, hg_norm_o]
    state_block = (None, HB, HG_EXPAND, HG_DI)
    if has_state:
        in_specs.append(pl.BlockSpec(state_block, lambda b, hg, c: (b, hg, 0, 0)))
        args.append(s0)
    return pl.pallas_call(
        functools.partial(_hgrn_kernel, C=C, HB=HB, layer=layer, has_state=has_state),
        grid=(B, nhg, nc),
        in_specs=in_specs,
        out_specs=[pl.BlockSpec((C, Wb), col(0)),
                   pl.BlockSpec(state_block, lambda b, hg, c: (b, hg, 0, 0))],
        out_shape=[jax.ShapeDtypeStruct((B * L, D_MODEL), F32),
                   jax.ShapeDtypeStruct((B, HG_HEADS, HG_EXPAND, HG_DI), F32)],
        scratch_shapes=[pltpu.VMEM((HB, HG_DI, HG_EXPAND), F32)],
        compiler_params=_cparams("parallel", "parallel", "arbitrary"),
        name="hgrn_state" if has_state else "hgrn_fresh",
    )(*args)


def _pad_lanes(v, offset=0):
    return jnp.zeros((1, LANES), F32).at[0, offset:offset + v.shape[0]].set(v.astype(F32))


def kernel(x_prompt, x_sample, cache_k, cache_v, state_gdn_conv, state_gdn, state_hgrn, page_table, norm_mix, norm_ffn, norm_final, w_in_even, w_out_even, da_lambda, da_subln, gdn_conv_w, gdn_a_log, gdn_dt_bias, gdn_norm, w_in_odd, w_out_odd, hg_lower, hg_norm, w_gate_up, w_down):
    Bp, Lp = x_prompt.shape[:2]
    DB, Ls = x_sample.shape[:2]
    past_len = page_table.shape[1] * PAGE_SIZE
    pos_p = np.tile(np.arange(Lp), Bp)
    pos_s = np.tile(past_len + np.arange(Ls), DB)
    Mp, Ms = Bp * Lp, DB * Ls
    tm_p, tm_s = 1024, Ms

    hp = x_prompt.reshape(Mp, D_MODEL)
    hs = x_sample.reshape(Ms, D_MODEL)
    outs = {k: [] for k in ("k_p", "v_p", "conv_p", "gdn_p", "hg_p", "k_s", "v_s", "conv_s", "gdn_s", "hg_s")}

    for l in range(DEPTH):
        nw_mix = norm_mix[l].reshape(1, D_MODEL)
        nw_ffn = norm_ffn[l].reshape(1, D_MODEL)
        if l % 2 == 0:
            e = l // 2
            lam_init = 0.8 - 0.6 * math.exp(-0.3 * l)
            wab = jnp.pad(w_in_even[e, :, EVEN_MAIN:], ((0, 0), (0, LANES - 2 * GDN_HEADS)))
            alog_pad = _pad_lanes(gdn_a_log[e])
            dtb_pad = _pad_lanes(gdn_dt_bias[e])
            sub = da_subln[e].reshape(1, DA_V_DIM)
            gnorm = gdn_norm[e].reshape(1, GDN_DV)
            proj, ab = _inproj_even(hp, nw_mix, w_in_even, e, wab, pos_p, tm_p)
            oa = _attn_prompt(proj, da_lambda[e], sub, Bp, Lp, lam_init)
            ob, st = _gdn(proj, ab, gdn_conv_w[e], alog_pad, dtb_pad, gnorm, Bp, Lp, min(GDN_CHUNK, Lp))
            mix = jnp.concatenate([oa.astype(BF16), ob.astype(BF16)], axis=1)
            hp = _mm_res(mix, w_out_even, e, hp, tm_p)
            outs["k_p"].append(proj[:, DA_W:2 * DA_W].reshape(Bp, Lp, DA_HEADS, 2, DA_QK_DIM))
            outs["v_p"].append(proj[:, 2 * DA_W:3 * DA_W].reshape(Bp, Lp, DA_HEADS, DA_V_DIM))
            outs["conv_p"].append(proj[:, 3 * DA_W:3 * DA_W + GDN_QKV].reshape(Bp, Lp, GDN_QKV)[:, Lp - (GDN_CONV - 1):])
            outs["gdn_p"].append(st)
            proj, ab = _inproj_even(hs, nw_mix, w_in_even, e, wab, pos_s, tm_s)
            oa = _attn_sample(proj, cache_k, cache_v, e, page_table, da_lambda[e], sub, DB, Ls, lam_init)
            prev8 = jnp.pad(state_gdn_conv[e], ((0, 0), (SUBLANES - (GDN_CONV - 1), 0), (0, 0)))
            ob, st = _gdn(proj, ab, gdn_conv_w[e], alog_pad, dtb_pad, gnorm, DB, Ls, min(GDN_CHUNK, Ls),
                          prev8=prev8, s0=state_gdn[e])
            mix = jnp.concatenate([oa.astype(BF16), ob.astype(BF16)], axis=1)
            hs = _mm_res(mix, w_out_even, e, hs, tm_s)
            gq = proj[:, 3 * DA_W:3 * DA_W + GDN_QKV].reshape(DB, Ls, GDN_QKV)
            conv_s = jnp.concatenate([state_gdn_conv[e], gq], axis=1)[:, Ls:]
            outs["k_s"].append(proj[:, DA_W:2 * DA_W].reshape(DB, Ls, DA_HEADS, 2, DA_QK_DIM))
            outs["v_s"].append(proj[:, 2 * DA_W:3 * DA_W].reshape(DB, Ls, DA_HEADS, DA_V_DIM))
            outs["conv_s"].append(conv_s)
            outs["gdn_s"].append(st)
        else:
            o = l // 2
            hnorm = hg_norm[o].reshape(1, D_MODEL)
            proj = _norm_matmul(hp, nw_mix, w_in_odd, o, tm_p)
            y, st = _hgrn(proj, hg_lower, hnorm, l, Bp, Lp, min(HG_CHUNK, Lp))
            hp = _mm_res(y.astype(BF16), w_out_odd, o, hp, tm_p)
            outs["hg_p"].append(st)
            proj = _norm_matmul(hs, nw_mix, w_in_odd, o, tm_s)
            y, st = _hgrn(proj, hg_lower, hnorm, l, DB, Ls, min(HG_CHUNK, Ls), s0=state_hgrn[o])
            hs = _mm_res(y.astype(BF16), w_out_odd, o, hs, tm_s)
            outs["hg_s"].append(st)
        act = _ffn_up(hp, nw_ffn, w_gate_up, l, tm_p)
        hp = _mm_res(act, w_down, l, hp, tm_p, tk=D_FF // 2)
        act = _ffn_up(hs, nw_ffn, w_gate_up, l, tm_s)
        hs = _mm_res(act, w_down, l, hs, tm_s, tk=D_FF // 2)

    nwf = norm_final.reshape(1, D_MODEL)
    y_prompt = _final_norm(hp, nwf, tm_p).reshape(Bp, Lp, D_MODEL)
    y_sample = _final_norm(hs, nwf, tm_s).reshape(DB, Ls, D_MODEL)
    st = lambda k: jnp.stack(outs[k])
    return (y_prompt, y_sample, st("k_p"), st("v_p"), st("conv_p"), st("gdn_p"), st("hg_p"),
            st("k_s"), st("v_s"), st("conv_s"), st("gdn_s"), st("hg_s"))
```

```python
import functools
import math

import numpy as np
import jax
import jax.numpy as jnp
from jax import lax
from jax.experimental import pallas as pl
from jax.experimental.pallas import tpu as pltpu

F32 = jnp.float32
BF16 = jnp.bfloat16

D_MODEL = 2048
DEPTH = 2
PAGE_SIZE = 128
NORM_EPS = 1e-6

DA_HEADS = 8
DA_QK_DIM = 64
DA_V_DIM = 2 * DA_QK_DIM
DA_SCALE = DA_QK_DIM ** -0.5
ROT_DIM = DA_QK_DIM // 4
ROPE_THETA = 500000.0

GDN_HEADS = 8
GDN_DK = 128
GDN_DV = 128
GDN_QKV = GDN_HEADS * (2 * GDN_DK + GDN_DV)
GDN_CONV = 4
GDN_CHUNK = 64

HG_EXPAND = 128
HG_HEADS = D_MODEL // HG_EXPAND
HG_DI = D_MODEL // HG_HEADS
HG_CHUNK = 64

D_FF = -(-(8 * D_MODEL) // (3 * 256)) * 256

DA_W = DA_HEADS * 2 * DA_QK_DIM
EVEN_MAIN = 3 * DA_W + GDN_QKV + GDN_HEADS * GDN_DV
ODD_IN = 4 * D_MODEL

LANES = 128
SUBLANES = 8
VMEM_LIMIT = 56 * 1024 * 1024
NEG_BIG = -1e30


def _cparams(*sem):
    return pltpu.CompilerParams(dimension_semantics=sem, vmem_limit_bytes=VMEM_LIMIT)


def _bdot(a, b):
    return jnp.dot(a.astype(BF16), b.astype(BF16), preferred_element_type=F32)


def _bdot_nt(a, b):
    return lax.dot_general(a.astype(BF16), b.astype(BF16), (((1,), (1,)), ((), ())),
                           preferred_element_type=F32)


def _split2(x):
    hi = x.astype(BF16)
    lo = (x - hi.astype(F32)).astype(BF16)
    return hi, lo


def _dot3(a, b):
    ah, al = _split2(a)
    bh, bl = _split2(b)
    d = functools.partial(jnp.dot, preferred_element_type=F32)
    return d(ah, bh) + (d(ah, bl) + d(al, bh))


def _cumsum_rows(tri, x):
    hi = x.astype(BF16)
    r = x - hi.astype(F32)
    mid = r.astype(BF16)
    lo = (r - mid.astype(F32)).astype(BF16)
    d = functools.partial(jnp.dot, preferred_element_type=F32)
    return d(tri, hi) + (d(tri, mid) + d(tri, lo))


def _sigmoid(x):
    return 1.0 / (1.0 + jnp.exp(-x))


def _silu(x):
    return x * _sigmoid(x)


def _rms_rows(x, w):
    return x * lax.rsqrt(jnp.mean(x * x, axis=-1, keepdims=True) + NORM_EPS) * w


def _inproj_even_kernel(x_ref, nw_ref, w_ref, wab_ref, cos_ref, sina_ref, sinb_ref,
                        o_ref, ab_ref, xn_ref, *, n_rope_tiles, tn):
    j = pl.program_id(1)

    @pl.when(j == 0)
    def _():
        xn = _rms_rows(x_ref[...], nw_ref[...]).astype(BF16)
        xn_ref[...] = xn
        ab_ref[...] = jnp.dot(xn, wab_ref[...].astype(BF16), preferred_element_type=F32)

    acc = jnp.dot(xn_ref[...], w_ref[...].astype(BF16), preferred_element_type=F32)

    @pl.when(j < n_rope_tiles)
    def _():
        cosf, sina, sinb = cos_ref[...], sina_ref[...], sinb_ref[...]
        for c in range(tn // LANES):
            a = acc[:, c * LANES:(c + 1) * LANES]
            o_ref[:, c * LANES:(c + 1) * LANES] = (
                a * cosf + pltpu.roll(a, LANES - ROT_DIM // 2, 1) * sina
                + pltpu.roll(a, ROT_DIM // 2, 1) * sinb)

    @pl.when(j >= n_rope_tiles)
    def _():
        o_ref[...] = acc


def _rope_tables(pos):
    half = ROT_DIM // 2
    inv = np.power(np.float32(ROPE_THETA), -np.arange(half, dtype=np.float32) * np.float32(2.0) / np.float32(ROT_DIM))
    ang = (pos.astype(np.float32)[:, None] * inv[None, :]).astype(np.float32)
    cos = np.cos(ang.astype(np.float64)).astype(np.float32)
    sin = np.sin(ang.astype(np.float64)).astype(np.float32)
    n = pos.shape[0]
    cosf = np.ones((n, LANES), np.float32)
    sina = np.zeros((n, LANES), np.float32)
    sinb = np.zeros((n, LANES), np.float32)
    for base in range(0, LANES, DA_QK_DIM):
        cosf[:, base:base + half] = cos
        cosf[:, base + half:base + 2 * half] = cos
        sina[:, base:base + half] = -sin
        sinb[:, base + half:base + 2 * half] = sin
    return jnp.asarray(cosf), jnp.asarray(sina), jnp.asarray(sinb)


def _inproj_even(x, nw, w_in_all, e, wab, pos, tm, tn=512):
    M = x.shape[0]
    cosf, sina, sinb = _rope_tables(pos)
    n_tiles = EVEN_MAIN // tn
    kern = functools.partial(_inproj_even_kernel, n_rope_tiles=(2 * DA_W) // tn, tn=tn)
    row = lambda i, j: (i, 0)
    return pl.pallas_call(
        kern,
        grid=(M // tm, n_tiles),
        in_specs=[
            pl.BlockSpec((tm, D_MODEL), row),
            pl.BlockSpec((1, D_MODEL), lambda i, j: (0, 0)),
            pl.BlockSpec((None, D_MODEL, tn), lambda i, j: (e, 0, j)),
            pl.BlockSpec((D_MODEL, LANES), lambda i, j: (0, 0)),
            pl.BlockSpec((tm, LANES), row),
            pl.BlockSpec((tm, LANES), row),
            pl.BlockSpec((tm, LANES), row),
        ],
        out_specs=[pl.BlockSpec((tm, tn), lambda i, j: (i, j)),
                   pl.BlockSpec((tm, LANES), row)],
        out_shape=[jax.ShapeDtypeStruct((M, EVEN_MAIN), F32),
                   jax.ShapeDtypeStruct((M, LANES), F32)],
        scratch_shapes=[pltpu.VMEM((tm, D_MODEL), BF16)],
        compiler_params=_cparams("parallel", "arbitrary"),
        name="inproj_even",
    )(x, nw, w_in_all, wab, cosf, sina, sinb)


def _norm_matmul_kernel(x_ref, nw_ref, w_ref, o_ref, xn_ref):
    @pl.when(pl.program_id(1) == 0)
    def _():
        xn_ref[...] = _rms_rows(x_ref[...], nw_ref[...]).astype(BF16)

    o_ref[...] = jnp.dot(xn_ref[...], w_ref[...].astype(BF16), preferred_element_type=F32)


def _norm_matmul(x, nw, w_all, l, tm, tn=512):
    M = x.shape[0]
    N = w_all.shape[-1]
    return pl.pallas_call(
        _norm_matmul_kernel,
        grid=(M // tm, N // tn),
        in_specs=[
            pl.BlockSpec((tm, D_MODEL), lambda i, j: (i, 0)),
            pl.BlockSpec((1, D_MODEL), lambda i, j: (0, 0)),
            pl.BlockSpec((None, D_MODEL, tn), lambda i, j: (l, 0, j)),
        ],
        out_specs=pl.BlockSpec((tm, tn), lambda i, j: (i, j)),
        out_shape=jax.ShapeDtypeStruct((M, N), F32),
        scratch_shapes=[pltpu.VMEM((tm, D_MODEL), BF16)],
        compiler_params=_cparams("parallel", "arbitrary"),
        name="norm_matmul",
    )(x, nw, w_all)


def _ffn_up_kernel(x_ref, nw_ref, wg_ref, wu_ref, o_ref, xn_ref):
    @pl.when(pl.program_id(1) == 0)
    def _():
        xn_ref[...] = _rms_rows(x_ref[...], nw_ref[...]).astype(BF16)

    xn = xn_ref[...]
    g = jnp.dot(xn, wg_ref[...].astype(BF16), preferred_element_type=F32)
    u = jnp.dot(xn, wu_ref[...].astype(BF16), preferred_element_type=F32)
    o_ref[...] = (_silu(g) * u).astype(BF16)


def _ffn_up(x, nw, w_gate_up, l, tm, tn=512):
    M = x.shape[0]
    nj = D_FF // tn
    return pl.pallas_call(
        _ffn_up_kernel,
        grid=(M // tm, nj),
        in_specs=[
            pl.BlockSpec((tm, D_MODEL), lambda i, j: (i, 0)),
            pl.BlockSpec((1, D_MODEL), lambda i, j: (0, 0)),
            pl.BlockSpec((None, D_MODEL, tn), lambda i, j: (l, 0, j)),
            pl.BlockSpec((None, D_MODEL, tn), lambda i, j: (l, 0, j + nj)),
        ],
        out_specs=pl.BlockSpec((tm, tn), lambda i, j: (i, j)),
        out_shape=jax.ShapeDtypeStruct((M, D_FF), BF16),
        scratch_shapes=[pltpu.VMEM((tm, D_MODEL), BF16)],
        compiler_params=_cparams("parallel", "arbitrary"),
        name="ffn_up",
    )(x, nw, w_gate_up, w_gate_up)


def _mm_res_kernel(a_ref, w_ref, r_ref, o_ref, acc_ref, *, nk):
    k = pl.program_id(2)
    p = jnp.dot(a_ref[...].astype(BF16), w_ref[...].astype(BF16), preferred_element_type=F32)
    if nk == 1:
        o_ref[...] = r_ref[...] + p
    else:
        @pl.when(k == 0)
        def _():
            acc_ref[...] = p

        @pl.when(jnp.logical_and(k > 0, k < nk - 1))
        def _():
            acc_ref[...] += p

        @pl.when(k == nk - 1)
        def _():
            o_ref[...] = r_ref[...] + (acc_ref[...] + p)


def _mm_res(a, w_all, l, res, tm, tn=512, tk=None):
    M, K = a.shape
    N = w_all.shape[-1]
    tk = K if tk is None else tk
    nk = K // tk
    return pl.pallas_call(
        functools.partial(_mm_res_kernel, nk=nk),
        grid=(M // tm, N // tn, nk),
        in_specs=[
            pl.BlockSpec((tm, tk), lambda i, j, k: (i, k)),
            pl.BlockSpec((None, tk, tn), lambda i, j, k: (l, k, j)),
            pl.BlockSpec((tm, tn), lambda i, j, k: (i, j)),
        ],
        out_specs=pl.BlockSpec((tm, tn), lambda i, j, k: (i, j)),
        out_shape=jax.ShapeDtypeStruct((M, N), F32),
        scratch_shapes=[pltpu.VMEM((tm, tn), F32)],
        compiler_params=_cparams("parallel", "parallel", "arbitrary"),
        name="matmul_residual",
    )(a, w_all, res)


def _final_norm_kernel(x_ref, w_ref, o_ref):
    o_ref[...] = _rms_rows(x_ref[...], w_ref[...])


def _final_norm(x, w, tm):
    M = x.shape[0]
    return pl.pallas_call(
        _final_norm_kernel,
        grid=(M // tm,),
        in_specs=[pl.BlockSpec((tm, D_MODEL), lambda i: (i, 0)),
                  pl.BlockSpec((1, D_MODEL), lambda i: (0, 0))],
        out_specs=pl.BlockSpec((tm, D_MODEL), lambda i: (i, 0)),
        out_shape=jax.ShapeDtypeStruct((M, D_MODEL), F32),
        compiler_params=_cparams("parallel"),
        name="final_norm",
    )(x, w)


def _lambda_of(lam_ref, lam_init):
    lf = lam_ref[...]
    s1 = jnp.sum(lf[0:1] * lf[1:2], axis=-1, keepdims=True)
    s2 = jnp.sum(lf[2:3] * lf[3:4], axis=-1, keepdims=True)
    return jnp.exp(s1) - jnp.exp(s2) + lam_init


def _attn_prompt_kernel(qi_ref, ki_ref, lam_ref, q_ref, k_ref, v_ref, sub_ref, o_ref,
                        m_ref, l_ref, acc_ref, *, t, lam_init):
    step = pl.program_id(2)
    qi = qi_ref[step]
    ki = ki_ref[step]

    @pl.when(ki == 0)
    def _():
        m_ref[...] = jnp.full(m_ref.shape, NEG_BIG, F32)
        l_ref[...] = jnp.zeros(l_ref.shape, F32)
        acc_ref[...] = jnp.zeros(acc_ref.shape, F32)

    def update(diagonal):
        q = q_ref[...] * DA_SCALE
        k = k_ref[...].astype(BF16)
        v = v_ref[...].astype(BF16)
        lane_map = lax.broadcasted_iota(jnp.int32, (1, LANES), 1) // DA_QK_DIM
        if diagonal:
            causal = (lax.broadcasted_iota(jnp.int32, (1, t), 1)
                      <= lax.broadcasted_iota(jnp.int32, (t, 1), 0))
        for m in range(2):
            qm = jnp.where(lane_map == m, q, 0.0).astype(BF16)
            s = lax.dot_general(qm, k, (((1,), (1,)), ((), ())), preferred_element_type=F32)
            if diagonal:
                s = jnp.where(causal, s, NEG_BIG)
            m_prev = m_ref[m]
            m_next = jnp.maximum(m_prev, jnp.max(s, axis=1, keepdims=True))
            alpha = jnp.exp(m_prev - m_next)
            p = jnp.exp(s - pltpu.repeat(m_next, t // LANES, axis=1))
            l_ref[m] = alpha * l_ref[m] + jnp.sum(p, axis=1, keepdims=True)
            acc_ref[m] = alpha * acc_ref[m] + jnp.dot(p.astype(BF16), v, preferred_element_type=F32)
            m_ref[m] = m_next

    @pl.when(ki < qi)
    def _():
        update(False)

    @pl.when(ki == qi)
    def _():
        update(True)
        lam = _lambda_of(lam_ref, lam_init)
        o = acc_ref[0] / l_ref[0] - lam * (acc_ref[1] / l_ref[1])
        o_ref[...] = _rms_rows(o, sub_ref[...]) * (1.0 - lam_init)


def _attn_prompt(proj, da_lambda_e, subln_e, B, S, lam_init, t=512):
    nq = S // t
    pairs = [(qi, ki) for qi in range(nq) for ki in range(qi + 1)]
    qi_tab = jnp.asarray([p[0] for p in pairs], jnp.int32)
    ki_tab = jnp.asarray([p[1] for p in pairs], jnp.int32)
    kern = functools.partial(_attn_prompt_kernel, t=t, lam_init=lam_init)
    grid_spec = pltpu.PrefetchScalarGridSpec(
        num_scalar_prefetch=2,
        grid=(B, DA_HEADS, len(pairs)),
        in_specs=[
            pl.BlockSpec((4, DA_QK_DIM), lambda b, h, s, qt, kt: (0, 0)),
            pl.BlockSpec((t, LANES), lambda b, h, s, qt, kt: (b * nq + qt[s], h)),
            pl.BlockSpec((t, LANES), lambda b, h, s, qt, kt: (b * nq + kt[s], DA_HEADS + h)),
            pl.BlockSpec((t, LANES), lambda b, h, s, qt, kt: (b * nq + kt[s], 2 * DA_HEADS + h)),
            pl.BlockSpec((1, DA_V_DIM), lambda b, h, s, qt, kt: (0, 0)),
        ],
        out_specs=pl.BlockSpec((t, LANES), lambda b, h, s, qt, kt: (b * nq + qt[s], h)),
        scratch_shapes=[pltpu.VMEM((2, t, LANES), F32), pltpu.VMEM((2, t, LANES), F32),
                        pltpu.VMEM((2, t, DA_V_DIM), F32)],
    )
    return pl.pallas_call(
        kern,
        grid_spec=grid_spec,
        out_shape=jax.ShapeDtypeStruct((B * S, DA_W), F32),
        compiler_params=_cparams("parallel", "parallel", "arbitrary"),
        name="diff_attn_prompt",
    )(qi_tab, ki_tab, da_lambda_e, proj, proj, proj, subln_e)


def _attn_sample_kernel(pt_ref, lam_ref, qbd_ref, knt_ref, vn_ref, sub_ref, *rest,
                        G, n_steps, dec_seq, lam_init):
    k_refs = rest[:G]
    v_refs = rest[G:2 * G]
    o_ref = rest[2 * G]
    st_ref, mx_ref, psum_ref, oacc_ref, vpad_ref = rest[2 * G + 1:]
    t = pl.program_id(1)
    n_pages = n_steps * G

    @pl.when(t == 0)
    def _():
        mx_ref[...] = jnp.full(mx_ref.shape, NEG_BIG, F32)
        vpad_ref[...] = jnp.zeros(vpad_ref.shape, F32)
        vpad_ref[0:dec_seq, :] = vn_ref[...]

    def scores(kt_page, idx, mask=None):
        s = jnp.dot(qbd_ref[...], kt_page.astype(BF16), preferred_element_type=F32)
        if mask is not None:
            s = jnp.where(mask, s, NEG_BIG)
        st_ref[idx] = s
        mx_ref[...] = jnp.maximum(mx_ref[...], s)

    @pl.when(t < n_steps)
    def _():
        for g in range(G):
            scores(k_refs[g][...], t * G + g)

    @pl.when(t == n_steps - 1)
    def _():
        qry = lax.broadcasted_iota(jnp.int32, (LANES, PAGE_SIZE), 0) % dec_seq
        key = lax.broadcasted_iota(jnp.int32, (LANES, PAGE_SIZE), 1)
        scores(knt_ref[...], n_pages, mask=key <= qry)
        row_max = jnp.max(mx_ref[...], axis=1, keepdims=True)
        mx_ref[...] = jnp.broadcast_to(row_max, mx_ref.shape)
        psum_ref[...] = jnp.zeros(psum_ref.shape, F32)
        oacc_ref[...] = jnp.zeros(oacc_ref.shape, F32)

    def accumulate(v_flat, idx):
        p = jnp.exp(st_ref[idx] - mx_ref[...])
        psum_ref[...] += p
        oacc_ref[...] += jnp.dot(p.astype(BF16), v_flat.astype(BF16), preferred_element_type=F32)

    def flat_values(v_ref):
        return jnp.concatenate(
            [v_ref[pl.ds(h, PAGE_SIZE, stride=DA_HEADS), :] for h in range(DA_HEADS)], axis=1)

    @pl.when(t >= n_steps)
    def _():
        for g in range(G):
            accumulate(flat_values(v_refs[g]), (t - n_steps) * G + g)

    @pl.when(t == 2 * n_steps - 1)
    def _():
        accumulate(vpad_ref[...], n_pages)
        lam = _lambda_of(lam_ref, lam_init)
        denom = jnp.sum(psum_ref[...], axis=1, keepdims=True)
        rows_per_head = 2 * dec_seq
        for h in range(DA_HEADS):
            r0 = h * rows_per_head
            blk = oacc_ref[r0:r0 + rows_per_head, h * DA_V_DIM:(h + 1) * DA_V_DIM]
            n = blk / denom[r0:r0 + rows_per_head]
            o = n[0:dec_seq] - lam * n[dec_seq:rows_per_head]
            o_ref[:, h * DA_V_DIM:(h + 1) * DA_V_DIM] = _rms_rows(o, sub_ref[...]) * (1.0 - lam_init)


def _attn_sample(proj_s, cache_k, cache_v, e, page_table, da_lambda_e, subln_e, DB, dec_seq, lam_init, G=4):
    n_pages = page_table.shape[1]
    n_steps = n_pages // G
    n_rows = DA_HEADS * 2 * dec_seq
    assert n_rows == LANES and n_pages % G == 0
    n_layers, n_phys = cache_k.shape[:2]
    q = proj_s[:, :DA_W].reshape(DB, dec_seq, DA_HEADS * 2, DA_QK_DIM).transpose(0, 2, 1, 3)
    eye = jnp.eye(DA_HEADS * 2, dtype=F32)
    qbd = (q[:, :, :, None, :] * eye[None, :, None, :, None]).reshape(DB, n_rows, DA_W)
    qbd = (qbd * DA_SCALE).astype(BF16)
    knt = proj_s[:, DA_W:2 * DA_W].reshape(DB, dec_seq, DA_W).transpose(0, 2, 1)
    knt = jnp.pad(knt, ((0, 0), (0, 0), (0, PAGE_SIZE - dec_seq))).astype(BF16)
    ckt = jnp.transpose(cache_k, (0, 1, 3, 4, 5, 2)).reshape(n_layers, n_phys, DA_W, PAGE_SIZE)
    cv = cache_v.reshape(n_layers, n_phys, PAGE_SIZE * DA_HEADS, DA_V_DIM)
    pt = page_table.reshape(-1)

    def k_map(g):
        return lambda b, t, pt: (e, pt[b * n_pages + jnp.minimum(t, n_steps - 1) * G + g], 0, 0)

    def v_map(g):
        return lambda b, t, pt: (e, pt[b * n_pages + jnp.maximum(t - n_steps, 0) * G + g], 0, 0)

    kern = functools.partial(_attn_sample_kernel, G=G, n_steps=n_steps, dec_seq=dec_seq, lam_init=lam_init)
    grid_spec = pltpu.PrefetchScalarGridSpec(
        num_scalar_prefetch=1,
        grid=(DB, 2 * n_steps),
        in_specs=[
            pl.BlockSpec((4, DA_QK_DIM), lambda b, t, pt: (0, 0)),
            pl.BlockSpec((None, n_rows, DA_W), lambda b, t, pt: (b, 0, 0)),
            pl.BlockSpec((None, DA_W, PAGE_SIZE), lambda b, t, pt: (b, 0, 0)),
            pl.BlockSpec((dec_seq, DA_W), lambda b, t, pt: (b, 2)),
            pl.BlockSpec((1, DA_V_DIM), lambda b, t, pt: (0, 0)),
        ] + [pl.BlockSpec((None, None, DA_W, PAGE_SIZE), k_map(g)) for g in range(G)]
          + [pl.BlockSpec((None, None, PAGE_SIZE * DA_HEADS, DA_V_DIM), v_map(g)) for g in range(G)],
        out_specs=pl.BlockSpec((dec_seq, DA_W), lambda b, t, pt: (b, 0)),
        scratch_shapes=[
            pltpu.VMEM((n_pages + 1, n_rows, PAGE_SIZE), F32),
            pltpu.VMEM((n_rows, PAGE_SIZE), F32),
            pltpu.VMEM((n_rows, PAGE_SIZE), F32),
            pltpu.VMEM((n_rows, DA_W), F32),
            pltpu.VMEM((PAGE_SIZE, DA_W), F32),
        ],
    )
    return pl.pallas_call(
        kern,
        grid_spec=grid_spec,
        out_shape=jax.ShapeDtypeStruct((DB * dec_seq, DA_W), F32),
        compiler_params=_cparams("parallel", "arbitrary"),
        name="diff_attn_sample",
    )(pt, da_lambda_e, qbd, knt, proj_s, subln_e, *([ckt] * G), *([cv] * G))


def _bmm(a, b):
    return jnp.einsum("gmk,gkn->gmn", a.astype(BF16), b.astype(BF16), preferred_element_type=F32)


def _bmm_nt(a, b):
    return jnp.einsum("gmk,gnk->gmn", a.astype(BF16), b.astype(BF16), preferred_element_type=F32)


def _bmm3(a, b):
    ah, al = _split2(a)
    bh, bl = _split2(b)
    d = functools.partial(jnp.einsum, "gmk,gkn->gmn", preferred_element_type=F32)
    return d(ah, bh) + (d(ah, bl) + d(al, bh))


def _tri_inverse(m_strict, C):
    ti = lax.broadcasted_iota(jnp.int32, (1, C, C), 1)
    sj = lax.broadcasted_iota(jnp.int32, (1, C, C), 2)
    eye = (ti == sj).astype(F32)
    a = jnp.where(ti // SUBLANES == sj // SUBLANES, -m_strict, 0.0)
    x = eye + a
    p = _bmm3(a, a)
    x = x + _bmm3(x, p)
    p = _bmm3(p, p)
    x = x + _bmm3(x, p)
    b = SUBLANES
    while b < C:
        join = jnp.logical_and(ti // (2 * b) == sj // (2 * b), ti // b != sj // b)
        c = jnp.where(join, m_strict, 0.0)
        x = x - _bmm3(x, _bmm3(c, x))
        b *= 2
    return x


def _gdn_kernel(*refs, C, NCH, has_state):
    if has_state:
        (q_ref, k_ref, v_ref, z_ref, ab_ref, cw_ref, alog_ref, dtb_ref, nw_ref,
         prev_ref, s0_ref, o_ref, sout_ref, buf_ref, s_ref) = refs
    else:
        (q_ref, k_ref, v_ref, z_ref, ab_ref, cw_ref, alog_ref, dtb_ref, nw_ref,
         o_ref, sout_ref, buf_ref, s_ref) = refs
    ci = pl.program_id(1)
    W = GDN_HEADS * GDN_DK
    H = GDN_HEADS
    R = C * NCH
    G = H * NCH

    @pl.when(ci == 0)
    def _():
        if has_state:
            buf_ref[0:SUBLANES, :] = prev_ref[...]
            s_ref[...] = s0_ref[...]
        else:
            buf_ref[0:SUBLANES, :] = jnp.zeros((SUBLANES, GDN_QKV), F32)
            s_ref[...] = jnp.zeros(s_ref.shape, F32)

    buf_ref[SUBLANES:SUBLANES + R, 0:W] = q_ref[...]
    buf_ref[SUBLANES:SUBLANES + R, W:2 * W] = k_ref[...]
    buf_ref[SUBLANES:SUBLANES + R, 2 * W:3 * W] = v_ref[...]
    y = buf_ref[SUBLANES:SUBLANES + R, :] * cw_ref[GDN_CONV - 1:GDN_CONV, :]
    for back in range(1, GDN_CONV):
        y = y + buf_ref[pl.ds(SUBLANES - back, R), :] * cw_ref[GDN_CONV - 1 - back:GDN_CONV - back, :]
    tail = buf_ref[R:R + SUBLANES, :]
    buf_ref[0:SUBLANES, :] = tail
    y = _silu(y)

    def heads(x, off, width):
        return jnp.stack([x[:, off + h * width:off + (h + 1) * width] for h in range(H)]).reshape(G, C, width)

    def chunk(x, c):
        return x.reshape((H, NCH) + x.shape[1:])[:, c]

    ab = ab_ref[...]
    sp = ab + dtb_ref[...]
    softplus = jnp.maximum(sp, 0.0) + jnp.log(1.0 + jnp.exp(-jnp.abs(sp)))
    g_all = -jnp.exp(alog_ref[...]) * softplus
    beta_all = _sigmoid(ab)

    ti = lax.broadcasted_iota(jnp.int32, (1, C, C), 1)
    sj = lax.broadcasted_iota(jnp.int32, (1, C, C), 2)
    causal = sj <= ti
    strict = sj < ti
    eye = ti == sj
    tri = causal[0].astype(BF16)
    bc_all = jnp.concatenate([_cumsum_rows(tri, g_all[c * C:(c + 1) * C]) for c in range(NCH)], axis=0)

    q3 = heads(y, 0, GDN_DK)
    k3 = heads(y, W, GDN_DK)
    v3 = heads(y, 2 * W, GDN_DV)
    q3 = q3 * lax.rsqrt(jnp.sum(q3 * q3, axis=-1, keepdims=True) + NORM_EPS) * (GDN_DK ** -0.5)
    k3 = k3 * lax.rsqrt(jnp.sum(k3 * k3, axis=-1, keepdims=True) + NORM_EPS)
    bcol = heads(bc_all, 0, 1)
    beta = heads(beta_all, H, 1)
    brow = jnp.sum(jnp.where(eye, bcol, 0.0), axis=1, keepdims=True)
    dec = jnp.exp(jnp.where(causal, bcol - brow, NEG_BIG))
    kb = k3 * beta
    eb = jnp.exp(bcol)
    m_strict = jnp.where(strict, _bmm_nt(kb, k3) * dec, 0.0)
    tinv = _tri_inverse(m_strict, C)
    uw = _bmm(tinv, jnp.concatenate([v3 * beta, kb * eb], axis=2))
    attn = _bmm_nt(q3, k3) * dec
    blast = bcol[:, C - 1:C, :]
    kd = k3 * jnp.exp(blast - bcol)
    gl = jnp.exp(blast)
    wq = jnp.concatenate([uw[:, :, GDN_DV:], q3 * eb], axis=1)
    z3 = heads(z_ref[...], 0, GDN_DV)

    s = s_ref[...]
    for c in range(NCH):
        ws = _bmm(chunk(wq, c), s)
        v_new = chunk(uw, c)[:, :, :GDN_DV] - ws[:, :C]
        o = ws[:, C:] + _bmm(chunk(attn, c), v_new)
        s = s * chunk(gl, c) + _bmm(jnp.swapaxes(chunk(kd, c), 1, 2), v_new)
        o = _rms_rows(o, nw_ref[...]) * _silu(chunk(z3, c))
        for h in range(H):
            o_ref[c * C:(c + 1) * C, h * GDN_DV:(h + 1) * GDN_DV] = o[h]
    s_ref[...] = s

    @pl.when(ci == pl.num_programs(1) - 1)
    def _():
        sout_ref[...] = s


def _gdn(proj, ab, conv_w_e, alog_pad, dtb_pad, norm_e, B, L, C, NCH=1, prev8=None, s0=None):
    has_state = prev8 is not None
    R = C * NCH
    nc = L // R
    W = GDN_HEADS * GDN_DK
    base = (3 * DA_W) // W
    row = lambda b, c: (b * nc + c, 0)
    in_specs = [
        pl.BlockSpec((R, W), lambda b, c: (b * nc + c, base)),
        pl.BlockSpec((R, W), lambda b, c: (b * nc + c, base + 1)),
        pl.BlockSpec((R, W), lambda b, c: (b * nc + c, base + 2)),
        pl.BlockSpec((R, W), lambda b, c: (b * nc + c, base + 3)),
        pl.BlockSpec((R, LANES), row),
        pl.BlockSpec((GDN_CONV, GDN_QKV), lambda b, c: (0, 0)),
        pl.BlockSpec((1, LANES), lambda b, c: (0, 0)),
        pl.BlockSpec((1, LANES), lambda b, c: (0, 0)),
        pl.BlockSpec((1, GDN_DV), lambda b, c: (0, 0)),
    ]
    args = [proj, proj, proj, proj, ab, conv_w_e, alog_pad, dtb_pad, norm_e]
    state_block = (None, GDN_HEADS, GDN_DK, GDN_DV)
    if has_state:
        in_specs += [pl.BlockSpec((None, SUBLANES, GDN_QKV), lambda b, c: (b, 0, 0)),
                     pl.BlockSpec(state_block, lambda b, c: (b, 0, 0, 0))]
        args += [prev8, s0]
    return pl.pallas_call(
        functools.partial(_gdn_kernel, C=C, NCH=NCH, has_state=has_state),
        grid=(B, nc),
        in_specs=in_specs,
        out_specs=[pl.BlockSpec((R, W), row),
                   pl.BlockSpec(state_block, lambda b, c: (b, 0, 0, 0))],
        out_shape=[jax.ShapeDtypeStruct((B * L, W), F32),
                   jax.ShapeDtypeStruct((B, GDN_HEADS, GDN_DK, GDN_DV), F32)],
        scratch_shapes=[pltpu.VMEM((SUBLANES + R, GDN_QKV), F32),
                        pltpu.VMEM((GDN_HEADS, GDN_DK, GDN_DV), F32)],
        compiler_params=_cparams("parallel", "arbitrary"),
        name="gdn_state" if has_state else "gdn_fresh",
    )(*args)


def _hgrn_kernel(*refs, C, HB, layer, has_state):
    if has_state:
        (q_ref, f_ref, i_ref, g_ref, lower_ref, nw_ref, s0_ref, o_ref, sout_ref, st_ref) = refs
    else:
        (q_ref, f_ref, i_ref, g_ref, lower_ref, nw_ref, o_ref, sout_ref, st_ref) = refs
    ci = pl.program_id(2)

    @pl.when(ci == 0)
    def _():
        for h in range(HB):
            st_ref[h] = s0_ref[h].T if has_state else jnp.zeros((HG_DI, HG_EXPAND), F32)

    low = lower_ref[...]
    ex = jnp.exp(low - jnp.max(low, axis=0, keepdims=True))
    lb = jnp.sum(ex[1:layer + 1], axis=0, keepdims=True) / jnp.sum(ex, axis=0, keepdims=True)

    ri = lax.broadcasted_iota(jnp.int32, (C, 1), 0)
    ti = lax.broadcasted_iota(jnp.int32, (C, C), 0)
    sj = lax.broadcasted_iota(jnp.int32, (C, C), 1)
    tri = (sj <= ti).astype(BF16)
    levels = [b for b in (2 * SUBLANES, 4 * SUBLANES, 8 * SUBLANES) if b <= C]

    for h in range(HB):
        hs = slice(h * HG_EXPAND, (h + 1) * HG_EXPAND)
        qt = _silu(q_ref[:, hs]) * (HG_EXPAND ** -0.5)
        lbh = lb[:, hs]
        fg = lbh + (1.0 - lbh) * _sigmoid(f_ref[:, hs])
        logf = jnp.log(fg)
        kk = 1.0 - fg
        v = i_ref[:, hs]
        bc = _cumsum_rows(tri, logf)
        st = st_ref[h]
        o = _bdot_nt(qt * jnp.exp(bc), st)
        if levels:
            a = jnp.zeros((C, C), F32)
            for b in levels:
                ref = jnp.concatenate(
                    [jnp.broadcast_to(bc[m:m + 1, :], (b, HG_EXPAND)) for m in range(b // 2, C, b)], axis=0)
                upper = (ri % b) >= (b // 2)
                eq = jnp.exp(jnp.where(upper, bc - ref, 0.0))
                ek = jnp.exp(jnp.where(upper, 0.0, ref - bc))
                qs = jnp.where(upper, qt * eq, 0.0)
                ks = jnp.where(upper, 0.0, kk * ek)
                a = a + jnp.where(ti // b == sj // b, _bdot_nt(qs, ks), 0.0)
            o = o + _bdot(a, v)
        r8 = ri % SUBLANES
        o = o + jnp.sum(qt * kk, axis=-1, keepdims=True) * v
        for d in range(1, SUBLANES):
            valid = r8 >= d
            ks = pltpu.roll(kk, d, 0)
            bs = pltpu.roll(bc, d, 0)
            vs = pltpu.roll(v, d, 0)
            e = jnp.exp(jnp.where(valid, bc - bs, 0.0))
            wgt = jnp.where(valid, jnp.sum(qt * ks * e, axis=-1, keepdims=True), 0.0)
            o = o + wgt * vs
        blast = bc[C - 1:C, :]
        st_ref[h] = st * jnp.exp(blast) + _bdot(v.T, kk * jnp.exp(blast - bc))
        o_ref[:, hs] = _rms_rows(o, nw_ref[:, hs]) * _silu(g_ref[:, hs])

    @pl.when(ci == pl.num_programs(2) - 1)
    def _():
        for h in range(HB):
            sout_ref[h] = st_ref[h].T


def _hgrn(proj, hg_lower, hg_norm_o, layer, B, L, C, s0=None, HB=4):
    has_state = s0 is not None
    nc = L // C
    nhg = HG_HEADS // HB
    Wb = HB * HG_EXPAND

    def col(section):
        return lambda b, hg, c: (b * nc + c, section * nhg + hg)

    in_specs = [pl.BlockSpec((C, Wb), col(s)) for s in range(4)] + [
        pl.BlockSpec((DEPTH, Wb), lambda b, hg, c: (0, hg)),
        pl.BlockSpec((1, Wb), lambda b, hg, c: (0, hg)),
    ]
    args = [proj, proj, proj, proj, hg_lower, hg_norm_o]
    state_block = (None, HB, HG_EXPAND, HG_DI)
    if has_state:
        in_specs.append(pl.BlockSpec(state_block, lambda b, hg, c: (b, hg, 0, 0)))
        args.append(s0)
    return pl.pallas_call(
        functools.partial(_hgrn_kernel, C=C, HB=HB, layer=layer, has_state=has_state),
        grid=(B, nhg, nc),
        in_specs=in_specs,
        out_specs=[pl.BlockSpec((C, Wb), col(0)),
                   pl.BlockSpec(state_block, lambda b, hg, c: (b, hg, 0, 0))],
        out_shape=[jax.ShapeDtypeStruct((B * L, D_MODEL), F32),
                   jax.ShapeDtypeStruct((B, HG_HEADS, HG_EXPAND, HG_DI), F32)],
        scratch_shapes=[pltpu.VMEM((HB, HG_DI, HG_EXPAND), F32)],
        compiler_params=_cparams("parallel", "parallel", "arbitrary"),
        name="hgrn_state" if has_state else "hgrn_fresh",
    )(*args)


def _pad_lanes(v, offset=0):
    return jnp.zeros((1, LANES), F32).at[0, offset:offset + v.shape[0]].set(v.astype(F32))


def kernel(x_prompt, x_sample, cache_k, cache_v, state_gdn_conv, state_gdn, state_hgrn, page_table, norm_mix, norm_ffn, norm_final, w_in_even, w_out_even, da_lambda, da_subln, gdn_conv_w, gdn_a_log, gdn_dt_bias, gdn_norm, w_in_odd, w_out_odd, hg_lower, hg_norm, w_gate_up, w_down):
    Bp, Lp = x_prompt.shape[:2]
    DB, Ls = x_sample.shape[:2]
    past_len = page_table.shape[1] * PAGE_SIZE
    pos_p = np.tile(np.arange(Lp), Bp)
    pos_s = np.tile(past_len + np.arange(Ls), DB)
    Mp, Ms = Bp * Lp, DB * Ls
    tm_p, tm_s = 1024, Ms

    hp = x_prompt.reshape(Mp, D_MODEL)
    hs = x_sample.reshape(Ms, D_MODEL)
    outs = {k: [] for k in ("k_p", "v_p", "conv_p", "gdn_p", "hg_p", "k_s", "v_s", "conv_s", "gdn_s", "hg_s")}

    for l in range(DEPTH):
        nw_mix = norm_mix[l].reshape(1, D_MODEL)
        nw_ffn = norm_ffn[l].reshape(1, D_MODEL)
        if l % 2 == 0:
            e = l // 2
            lam_init = 0.8 - 0.6 * math.exp(-0.3 * l)
            wab = jnp.pad(w_in_even[e, :, EVEN_MAIN:], ((0, 0), (0, LANES - 2 * GDN_HEADS)))
            alog_pad = _pad_lanes(gdn_a_log[e])
            dtb_pad = _pad_lanes(gdn_dt_bias[e])
            sub = da_subln[e].reshape(1, DA_V_DIM)
            gnorm = gdn_norm[e].reshape(1, GDN_DV)
            proj, ab = _inproj_even(hp, nw_mix, w_in_even, e, wab, pos_p, tm_p)
            oa = _attn_prompt(proj, da_lambda[e], sub, Bp, Lp, lam_init)
            gdn_c = min(GDN_CHUNK, Lp)
            ob, st = _gdn(proj, ab, gdn_conv_w[e], alog_pad, dtb_pad, gnorm, Bp, Lp, gdn_c,
                          NCH=2 if Lp % (2 * gdn_c) == 0 else 1)
            mix = jnp.concatenate([oa.astype(BF16), ob.astype(BF16)], axis=1)
            hp = _mm_res(mix, w_out_even, e, hp, tm_p)
            outs["k_p"].append(proj[:, DA_W:2 * DA_W].reshape(Bp, Lp, DA_HEADS, 2, DA_QK_DIM))
            outs["v_p"].append(proj[:, 2 * DA_W:3 * DA_W].reshape(Bp, Lp, DA_HEADS, DA_V_DIM))
            outs["conv_p"].append(
                proj.reshape(Bp, Lp, EVEN_MAIN)[:, Lp - (GDN_CONV - 1):, 3 * DA_W:3 * DA_W + GDN_QKV])
            outs["gdn_p"].append(st)
            proj, ab = _inproj_even(hs, nw_mix, w_in_even, e, wab, pos_s, tm_s)
            oa = _attn_sample(proj, cache_k, cache_v, e, page_table, da_lambda[e], sub, DB, Ls, lam_init)
            prev8 = jnp.pad(state_gdn_conv[e], ((0, 0), (SUBLANES - (GDN_CONV - 1), 0), (0, 0)))
            ob, st = _gdn(proj, ab, gdn_conv_w[e], alog_pad, dtb_pad, gnorm, DB, Ls, min(GDN_CHUNK, Ls),
                          prev8=prev8, s0=state_gdn[e])
            mix = jnp.concatenate([oa.astype(BF16), ob.astype(BF16)], axis=1)
            hs = _mm_res(mix, w_out_even, e, hs, tm_s)
            gq = proj[:, 3 * DA_W:3 * DA_W + GDN_QKV].reshape(DB, Ls, GDN_QKV)
            conv_s = jnp.concatenate([state_gdn_conv[e], gq], axis=1)[:, Ls:]
            outs["k_s"].append(proj[:, DA_W:2 * DA_W].reshape(DB, Ls, DA_HEADS, 2, DA_QK_DIM))
            outs["v_s"].append(proj[:, 2 * DA_W:3 * DA_W].reshape(DB, Ls, DA_HEADS, DA_V_DIM))
            outs["conv_s"].append(conv_s)
            outs["gdn_s"].append(st)
        else:
            o = l // 2
            hnorm = hg_norm[o].reshape(1, D_MODEL)
            proj = _norm_matmul(hp, nw_mix, w_in_odd, o, tm_p)
            y, st = _hgrn(proj, hg_lower, hnorm, l, Bp, Lp, min(HG_CHUNK, Lp))
            hp = _mm_res(y.astype(BF16), w_out_odd, o, hp, tm_p)
            outs["hg_p"].append(st)
            proj = _norm_matmul(hs, nw_mix, w_in_odd, o, tm_s)
            y, st = _hgrn(proj, hg_lower, hnorm, l, DB, Ls, min(HG_CHUNK, Ls), s0=state_hgrn[o])
            hs = _mm_res(y.astype(BF16), w_out_odd, o, hs, tm_s)
            outs["hg_s"].append(st)
        act = _ffn_up(hp, nw_ffn, w_gate_up, l, tm_p)
        hp = _mm_res(act, w_down, l, hp, tm_p, tk=D_FF // 2)
        act = _ffn_up(hs, nw_ffn, w_gate_up, l, tm_s)
        hs = _mm_res(act, w_down, l, hs, tm_s, tk=D_FF // 2)

    nwf = norm_final.reshape(1, D_MODEL)
    y_prompt = _final_norm(hp, nwf, tm_p).reshape(Bp, Lp, D_MODEL)
    y_sample = _final_norm(hs, nwf, tm_s).reshape(DB, Ls, D_MODEL)
    st = lambda k: jnp.stack(outs[k])
    return (y_prompt, y_sample, st("k_p"), st("v_p"), st("conv_p"), st("gdn_p"), st("hg_p"),
            st("k_s"), st("v_s"), st("conv_s"), st("gdn_s"), st("hg_s"))
```

```python
import functools
import math

import numpy as np
import jax
import jax.numpy as jnp
from jax import lax
from jax.experimental import pallas as pl
from jax.experimental.pallas import tpu as pltpu

F32 = jnp.float32
BF16 = jnp.bfloat16

D_MODEL = 2048
DEPTH = 2
PAGE_SIZE = 128
NORM_EPS = 1e-6

DA_HEADS = 8
DA_QK_DIM = 64
DA_V_DIM = 2 * DA_QK_DIM
DA_SCALE = DA_QK_DIM ** -0.5
ROT_DIM = DA_QK_DIM // 4
ROPE_THETA = 500000.0

GDN_HEADS = 8
GDN_DK = 128
GDN_DV = 128
GDN_QKV = GDN_HEADS * (2 * GDN_DK + GDN_DV)
GDN_CONV = 4
GDN_CHUNK = 64

HG_EXPAND = 128
HG_HEADS = D_MODEL // HG_EXPAND
HG_DI = D_MODEL // HG_HEADS
HG_CHUNK = 64

D_FF = -(-(8 * D_MODEL) // (3 * 256)) * 256

DA_W = DA_HEADS * 2 * DA_QK_DIM
EVEN_MAIN = 3 * DA_W + GDN_QKV + GDN_HEADS * GDN_DV
ODD_IN = 4 * D_MODEL

LANES = 128
SUBLANES = 8
VMEM_LIMIT = 56 * 1024 * 1024
NEG_BIG = -1e30


def _cparams(*sem):
    return pltpu.CompilerParams(dimension_semantics=sem, vmem_limit_bytes=VMEM_LIMIT)


def _bdot(a, b):
    return jnp.dot(a.astype(BF16), b.astype(BF16), preferred_element_type=F32)


def _bdot_nt(a, b):
    return lax.dot_general(a.astype(BF16), b.astype(BF16), (((1,), (1,)), ((), ())),
                           preferred_element_type=F32)


def _split2(x):
    hi = x.astype(BF16)
    lo = (x - hi.astype(F32)).astype(BF16)
    return hi, lo


def _dot3(a, b):
    ah, al = _split2(a)
    bh, bl = _split2(b)
    d = functools.partial(jnp.dot, preferred_element_type=F32)
    return d(ah, bh) + (d(ah, bl) + d(al, bh))


def _cumsum_rows(tri, x):
    hi = x.astype(BF16)
    r = x - hi.astype(F32)
    mid = r.astype(BF16)
    lo = (r - mid.astype(F32)).astype(BF16)
    d = functools.partial(jnp.dot, preferred_element_type=F32)
    return d(tri, hi) + (d(tri, mid) + d(tri, lo))


def _sigmoid(x):
    return 0.5 * jnp.tanh(0.5 * x) + 0.5


def _silu(x):
    return x * _sigmoid(x)


def _rms_rows(x, w):
    return x * lax.rsqrt(jnp.mean(x * x, axis=-1, keepdims=True) + NORM_EPS) * w


W_SLABS = 4


def _slab_specs(k_rows, tn, index):
    def spec(s):
        def index_map(*g):
            layer, kt, col = index(*g)
            return (layer, kt * W_SLABS + s, col)
        return pl.BlockSpec((None, k_rows // W_SLABS, tn), index_map)
    return [spec(s) for s in range(W_SLABS)]


def _load_weight(w_refs, wbuf_ref):
    rows = wbuf_ref.shape[0] // len(w_refs)
    for s, r in enumerate(w_refs):
        wbuf_ref[s * rows:(s + 1) * rows, :] = r[...].astype(BF16)
    return wbuf_ref[...]


def _inproj_even_kernel(x_ref, nw_ref, *rest, n_rope_tiles, tn):
    w_refs = rest[:W_SLABS]
    wab_ref, cos_ref, sina_ref, sinb_ref, o_ref, ab_ref, xn_ref, wbuf_ref = rest[W_SLABS:]
    j = pl.program_id(1)

    @pl.when(j == 0)
    def _():
        xn = _rms_rows(x_ref[...], nw_ref[...]).astype(BF16)
        xn_ref[...] = xn
        ab_ref[...] = jnp.dot(xn, wab_ref[...].astype(BF16), preferred_element_type=F32)

    acc = jnp.dot(xn_ref[...], _load_weight(w_refs, wbuf_ref), preferred_element_type=F32)

    @pl.when(j < n_rope_tiles)
    def _():
        cosf, sina, sinb = cos_ref[...], sina_ref[...], sinb_ref[...]
        for c in range(tn // LANES):
            a = acc[:, c * LANES:(c + 1) * LANES]
            o_ref[:, c * LANES:(c + 1) * LANES] = (
                a * cosf + pltpu.roll(a, LANES - ROT_DIM // 2, 1) * sina
                + pltpu.roll(a, ROT_DIM // 2, 1) * sinb)

    @pl.when(j >= n_rope_tiles)
    def _():
        o_ref[...] = acc


def _rope_tables(pos):
    half = ROT_DIM // 2
    inv = np.power(np.float32(ROPE_THETA), -np.arange(half, dtype=np.float32) * np.float32(2.0) / np.float32(ROT_DIM))
    ang = (pos.astype(np.float32)[:, None] * inv[None, :]).astype(np.float32)
    cos = np.cos(ang.astype(np.float64)).astype(np.float32)
    sin = np.sin(ang.astype(np.float64)).astype(np.float32)
    n = pos.shape[0]
    cosf = np.ones((n, LANES), np.float32)
    sina = np.zeros((n, LANES), np.float32)
    sinb = np.zeros((n, LANES), np.float32)
    for base in range(0, LANES, DA_QK_DIM):
        cosf[:, base:base + half] = cos
        cosf[:, base + half:base + 2 * half] = cos
        sina[:, base:base + half] = -sin
        sinb[:, base + half:base + 2 * half] = sin
    return jnp.asarray(cosf), jnp.asarray(sina), jnp.asarray(sinb)


def _inproj_even(x, nw, w_in_all, e, wab, pos, tm, tn=512):
    M = x.shape[0]
    cosf, sina, sinb = _rope_tables(pos)
    n_tiles = EVEN_MAIN // tn
    kern = functools.partial(_inproj_even_kernel, n_rope_tiles=(2 * DA_W) // tn, tn=tn)
    row = lambda i, j: (i, 0)
    return pl.pallas_call(
        kern,
        grid=(M // tm, n_tiles),
        in_specs=[
            pl.BlockSpec((tm, D_MODEL), row),
            pl.BlockSpec((1, D_MODEL), lambda i, j: (0, 0)),
        ] + _slab_specs(D_MODEL, tn, lambda i, j: (e, 0, j)) + [
            pl.BlockSpec((D_MODEL, LANES), lambda i, j: (0, 0)),
            pl.BlockSpec((tm, LANES), row),
            pl.BlockSpec((tm, LANES), row),
            pl.BlockSpec((tm, LANES), row),
        ],
        out_specs=[pl.BlockSpec((tm, tn), lambda i, j: (i, j)),
                   pl.BlockSpec((tm, LANES), row)],
        out_shape=[jax.ShapeDtypeStruct((M, EVEN_MAIN), F32),
                   jax.ShapeDtypeStruct((M, LANES), F32)],
        scratch_shapes=[pltpu.VMEM((tm, D_MODEL), BF16), pltpu.VMEM((D_MODEL, tn), BF16)],
        compiler_params=_cparams("parallel", "arbitrary"),
        name="inproj_even",
    )(x, nw, *([w_in_all] * W_SLABS), wab, cosf, sina, sinb)


def _norm_matmul_kernel(x_ref, nw_ref, *rest):
    w_refs = rest[:W_SLABS]
    o_ref, xn_ref, wbuf_ref = rest[W_SLABS:]

    @pl.when(pl.program_id(1) == 0)
    def _():
        xn_ref[...] = _rms_rows(x_ref[...], nw_ref[...]).astype(BF16)

    o_ref[...] = jnp.dot(xn_ref[...], _load_weight(w_refs, wbuf_ref), preferred_element_type=F32)


def _norm_matmul(x, nw, w_all, l, tm, tn=512):
    M = x.shape[0]
    N = w_all.shape[-1]
    return pl.pallas_call(
        _norm_matmul_kernel,
        grid=(M // tm, N // tn),
        in_specs=[
            pl.BlockSpec((tm, D_MODEL), lambda i, j: (i, 0)),
            pl.BlockSpec((1, D_MODEL), lambda i, j: (0, 0)),
        ] + _slab_specs(D_MODEL, tn, lambda i, j: (l, 0, j)),
        out_specs=pl.BlockSpec((tm, tn), lambda i, j: (i, j)),
        out_shape=jax.ShapeDtypeStruct((M, N), F32),
        scratch_shapes=[pltpu.VMEM((tm, D_MODEL), BF16), pltpu.VMEM((D_MODEL, tn), BF16)],
        compiler_params=_cparams("parallel", "arbitrary"),
        name="norm_matmul",
    )(x, nw, *([w_all] * W_SLABS))


def _ffn_up_kernel(x_ref, nw_ref, *rest):
    wg_refs = rest[:W_SLABS]
    wu_refs = rest[W_SLABS:2 * W_SLABS]
    o_ref, xn_ref, wgbuf_ref, wubuf_ref = rest[2 * W_SLABS:]

    @pl.when(pl.program_id(1) == 0)
    def _():
        xn_ref[...] = _rms_rows(x_ref[...], nw_ref[...]).astype(BF16)

    xn = xn_ref[...]
    g = jnp.dot(xn, _load_weight(wg_refs, wgbuf_ref), preferred_element_type=F32)
    u = jnp.dot(xn, _load_weight(wu_refs, wubuf_ref), preferred_element_type=F32)
    o_ref[...] = (_silu(g) * u).astype(BF16)


def _ffn_up(x, nw, w_gate_up, l, tm, tn=512):
    M = x.shape[0]
    nj = D_FF // tn
    return pl.pallas_call(
        _ffn_up_kernel,
        grid=(M // tm, nj),
        in_specs=[
            pl.BlockSpec((tm, D_MODEL), lambda i, j: (i, 0)),
            pl.BlockSpec((1, D_MODEL), lambda i, j: (0, 0)),
        ] + _slab_specs(D_MODEL, tn, lambda i, j: (l, 0, j))
          + _slab_specs(D_MODEL, tn, lambda i, j: (l, 0, j + nj)),
        out_specs=pl.BlockSpec((tm, tn), lambda i, j: (i, j)),
        out_shape=jax.ShapeDtypeStruct((M, D_FF), BF16),
        scratch_shapes=[pltpu.VMEM((tm, D_MODEL), BF16), pltpu.VMEM((D_MODEL, tn), BF16),
                        pltpu.VMEM((D_MODEL, tn), BF16)],
        compiler_params=_cparams("parallel", "arbitrary"),
        name="ffn_up",
    )(x, nw, *([w_gate_up] * (2 * W_SLABS)))


def _mm_res_kernel(a_ref, *rest, nk):
    w_refs = rest[:W_SLABS]
    r_ref, o_ref, acc_ref, wbuf_ref = rest[W_SLABS:]
    k = pl.program_id(2)
    p = jnp.dot(a_ref[...].astype(BF16), _load_weight(w_refs, wbuf_ref), preferred_element_type=F32)
    if nk == 1:
        o_ref[...] = r_ref[...] + p
    else:
        @pl.when(k == 0)
        def _():
            acc_ref[...] = p

        @pl.when(jnp.logical_and(k > 0, k < nk - 1))
        def _():
            acc_ref[...] += p

        @pl.when(k == nk - 1)
        def _():
            o_ref[...] = r_ref[...] + (acc_ref[...] + p)


def _mm_res(a, w_all, l, res, tm, tn=512, tk=None):
    M, K = a.shape
    N = w_all.shape[-1]
    tk = K if tk is None else tk
    nk = K // tk
    return pl.pallas_call(
        functools.partial(_mm_res_kernel, nk=nk),
        grid=(M // tm, N // tn, nk),
        in_specs=[pl.BlockSpec((tm, tk), lambda i, j, k: (i, k))]
                 + _slab_specs(tk, tn, lambda i, j, k: (l, k, j))
                 + [pl.BlockSpec((tm, tn), lambda i, j, k: (i, j))],
        out_specs=pl.BlockSpec((tm, tn), lambda i, j, k: (i, j)),
        out_shape=jax.ShapeDtypeStruct((M, N), F32),
        scratch_shapes=[pltpu.VMEM((tm, tn), F32), pltpu.VMEM((tk, tn), BF16)],
        compiler_params=_cparams("parallel", "parallel", "arbitrary"),
        name="matmul_residual",
    )(a, *([w_all] * W_SLABS), res)


def _final_norm_kernel(x_ref, w_ref, o_ref):
    o_ref[...] = _rms_rows(x_ref[...], w_ref[...])


def _final_norm(x, w, tm):
    M = x.shape[0]
    return pl.pallas_call(
        _final_norm_kernel,
        grid=(M // tm,),
        in_specs=[pl.BlockSpec((tm, D_MODEL), lambda i: (i, 0)),
                  pl.BlockSpec((1, D_MODEL), lambda i: (0, 0))],
        out_specs=pl.BlockSpec((tm, D_MODEL), lambda i: (i, 0)),
        out_shape=jax.ShapeDtypeStruct((M, D_MODEL), F32),
        compiler_params=_cparams("parallel"),
        name="final_norm",
    )(x, w)


def _lambda_of(lam_ref, lam_init):
    lf = lam_ref[...]
    s1 = jnp.sum(lf[0:1] * lf[1:2], axis=-1, keepdims=True)
    s2 = jnp.sum(lf[2:3] * lf[3:4], axis=-1, keepdims=True)
    return jnp.exp(s1) - jnp.exp(s2) + lam_init


def _attn_prompt_kernel(qi_ref, ki_ref, lam_ref, q_ref, k_ref, v_ref, sub_ref, o_ref,
                        m_ref, l_ref, acc_ref, *, t, lam_init):
    step = pl.program_id(2)
    qi = qi_ref[step]
    ki = ki_ref[step]

    @pl.when(ki == 0)
    def _():
        m_ref[...] = jnp.full(m_ref.shape, NEG_BIG, F32)
        l_ref[...] = jnp.zeros(l_ref.shape, F32)
        acc_ref[...] = jnp.zeros(acc_ref.shape, F32)

    def update(diagonal):
        q = q_ref[...] * DA_SCALE
        k = k_ref[...].astype(BF16)
        v = v_ref[...].astype(BF16)
        lane_map = lax.broadcasted_iota(jnp.int32, (1, LANES), 1) // DA_QK_DIM
        if diagonal:
            causal = (lax.broadcasted_iota(jnp.int32, (1, t), 1)
                      <= lax.broadcasted_iota(jnp.int32, (t, 1), 0))
        for m in range(2):
            qm = jnp.where(lane_map == m, q, 0.0).astype(BF16)
            s = lax.dot_general(qm, k, (((1,), (1,)), ((), ())), preferred_element_type=F32)
            if diagonal:
                s = jnp.where(causal, s, NEG_BIG)
            m_prev = m_ref[m]
            m_next = jnp.maximum(m_prev, jnp.max(s, axis=1, keepdims=True))
            alpha = jnp.exp(m_prev - m_next)
            p = jnp.exp(s - jnp.concatenate([m_next] * (t // LANES), axis=1))
            l_ref[m] = alpha * l_ref[m] + jnp.sum(p, axis=1, keepdims=True)
            acc_ref[m] = alpha * acc_ref[m] + jnp.dot(p.astype(BF16), v, preferred_element_type=F32)
            m_ref[m] = m_next

    @pl.when(ki < qi)
    def _():
        update(False)

    @pl.when(ki == qi)
    def _():
        update(True)
        lam = _lambda_of(lam_ref, lam_init)
        o = acc_ref[0] / l_ref[0] - lam * (acc_ref[1] / l_ref[1])
        o_ref[...] = _rms_rows(o, sub_ref[...]) * (1.0 - lam_init)


def _attn_prompt(proj, da_lambda_e, subln_e, B, S, lam_init, t=512):
    nq = S // t
    pairs = [(qi, ki) for qi in range(nq) for ki in range(qi + 1)]
    qi_tab = jnp.asarray([p[0] for p in pairs], jnp.int32)
    ki_tab = jnp.asarray([p[1] for p in pairs], jnp.int32)
    kern = functools.partial(_attn_prompt_kernel, t=t, lam_init=lam_init)
    grid_spec = pltpu.PrefetchScalarGridSpec(
        num_scalar_prefetch=2,
        grid=(B, DA_HEADS, len(pairs)),
        in_specs=[
            pl.BlockSpec((4, DA_QK_DIM), lambda b, h, s, qt, kt: (0, 0)),
            pl.BlockSpec((t, LANES), lambda b, h, s, qt, kt: (b * nq + qt[s], h)),
            pl.BlockSpec((t, LANES), lambda b, h, s, qt, kt: (b * nq + kt[s], DA_HEADS + h)),
            pl.BlockSpec((t, LANES), lambda b, h, s, qt, kt: (b * nq + kt[s], 2 * DA_HEADS + h)),
            pl.BlockSpec((1, DA_V_DIM), lambda b, h, s, qt, kt: (0, 0)),
        ],
        out_specs=pl.BlockSpec((t, LANES), lambda b, h, s, qt, kt: (b * nq + qt[s], h)),
        scratch_shapes=[pltpu.VMEM((2, t, LANES), F32), pltpu.VMEM((2, t, LANES), F32),
                        pltpu.VMEM((2, t, DA_V_DIM), F32)],
    )
    return pl.pallas_call(
        kern,
        grid_spec=grid_spec,
        out_shape=jax.ShapeDtypeStruct((B * S, DA_W), F32),
        compiler_params=_cparams("parallel", "parallel", "arbitrary"),
        name="diff_attn_prompt",
    )(qi_tab, ki_tab, da_lambda_e, proj, proj, proj, subln_e)


def _attn_sample_kernel(pt_ref, lam_ref, qbd_ref, knt_ref, vn_ref, sub_ref, *rest,
                        G, n_steps, dec_seq, lam_init):
    k_refs = rest[:G]
    v_refs = rest[G:2 * G]
    o_ref = rest[2 * G]
    st_ref, mx_ref, psum_ref, oacc_ref, vpad_ref = rest[2 * G + 1:]
    t = pl.program_id(1)
    n_pages = n_steps * G

    @pl.when(t == 0)
    def _():
        mx_ref[...] = jnp.full(mx_ref.shape, NEG_BIG, F32)
        vpad_ref[...] = jnp.zeros(vpad_ref.shape, F32)
        vpad_ref[0:dec_seq, :] = vn_ref[...]

    def scores(kt_page, idx, mask=None):
        s = jnp.dot(qbd_ref[...], kt_page.astype(BF16), preferred_element_type=F32)
        if mask is not None:
            s = jnp.where(mask, s, NEG_BIG)
        st_ref[idx] = s
        mx_ref[...] = jnp.maximum(mx_ref[...], s)

    @pl.when(t < n_steps)
    def _():
        for g in range(G):
            scores(k_refs[g][...], t * G + g)

    @pl.when(t == n_steps - 1)
    def _():
        qry = lax.broadcasted_iota(jnp.int32, (LANES, PAGE_SIZE), 0) % dec_seq
        key = lax.broadcasted_iota(jnp.int32, (LANES, PAGE_SIZE), 1)
        scores(knt_ref[...], n_pages, mask=key <= qry)
        row_max = jnp.max(mx_ref[...], axis=1, keepdims=True)
        mx_ref[...] = jnp.broadcast_to(row_max, mx_ref.shape)
        psum_ref[...] = jnp.zeros(psum_ref.shape, F32)
        oacc_ref[...] = jnp.zeros(oacc_ref.shape, F32)

    def accumulate(v_flat, idx):
        p = jnp.exp(st_ref[idx] - mx_ref[...])
        psum_ref[...] += p
        oacc_ref[...] += jnp.dot(p.astype(BF16), v_flat.astype(BF16), preferred_element_type=F32)

    def flat_values(v_ref):
        return jnp.concatenate(
            [v_ref[pl.ds(h, PAGE_SIZE, stride=DA_HEADS), :] for h in range(DA_HEADS)], axis=1)

    @pl.when(t >= n_steps)
    def _():
        for g in range(G):
            accumulate(flat_values(v_refs[g]), (t - n_steps) * G + g)

    @pl.when(t == 2 * n_steps - 1)
    def _():
        accumulate(vpad_ref[...], n_pages)
        lam = _lambda_of(lam_ref, lam_init)
        denom = jnp.sum(psum_ref[...], axis=1, keepdims=True)
        rows_per_head = 2 * dec_seq
        for h in range(DA_HEADS):
            r0 = h * rows_per_head
            blk = oacc_ref[r0:r0 + rows_per_head, h * DA_V_DIM:(h + 1) * DA_V_DIM]
            n = blk / denom[r0:r0 + rows_per_head]
            o = n[0:dec_seq] - lam * n[dec_seq:rows_per_head]
            o_ref[:, h * DA_V_DIM:(h + 1) * DA_V_DIM] = _rms_rows(o, sub_ref[...]) * (1.0 - lam_init)


def _attn_sample(proj_s, cache_k, cache_v, e, page_table, da_lambda_e, subln_e, DB, dec_seq, lam_init, G=16):
    n_pages = page_table.shape[1]
    n_steps = n_pages // G
    n_rows = DA_HEADS * 2 * dec_seq
    assert n_rows == LANES and n_pages % G == 0
    n_layers, n_phys = cache_k.shape[:2]
    q = proj_s[:, :DA_W].reshape(DB, dec_seq, DA_HEADS * 2, DA_QK_DIM).transpose(0, 2, 1, 3)
    eye = jnp.eye(DA_HEADS * 2, dtype=F32)
    qbd = (q[:, :, :, None, :] * eye[None, :, None, :, None]).reshape(DB, n_rows, DA_W)
    qbd = (qbd * DA_SCALE).astype(BF16)
    knt = proj_s[:, DA_W:2 * DA_W].reshape(DB, dec_seq, DA_W).transpose(0, 2, 1)
    knt = jnp.pad(knt, ((0, 0), (0, 0), (0, PAGE_SIZE - dec_seq))).astype(BF16)
    ckt = jnp.transpose(cache_k, (0, 1, 3, 4, 5, 2)).reshape(n_layers, n_phys, DA_W, PAGE_SIZE)
    cv = cache_v.reshape(n_layers, n_phys, PAGE_SIZE * DA_HEADS, DA_V_DIM)
    pt = page_table.reshape(-1)

    def k_map(g):
        return lambda b, t, pt: (e, pt[b * n_pages + jnp.minimum(t, n_steps - 1) * G + g], 0, 0)

    def v_map(g):
        return lambda b, t, pt: (e, pt[b * n_pages + jnp.maximum(t - n_steps, 0) * G + g], 0, 0)

    kern = functools.partial(_attn_sample_kernel, G=G, n_steps=n_steps, dec_seq=dec_seq, lam_init=lam_init)
    grid_spec = pltpu.PrefetchScalarGridSpec(
        num_scalar_prefetch=1,
        grid=(DB, 2 * n_steps),
        in_specs=[
            pl.BlockSpec((4, DA_QK_DIM), lambda b, t, pt: (0, 0)),
            pl.BlockSpec((None, n_rows, DA_W), lambda b, t, pt: (b, 0, 0)),
            pl.BlockSpec((None, DA_W, PAGE_SIZE), lambda b, t, pt: (b, 0, 0)),
            pl.BlockSpec((dec_seq, DA_W), lambda b, t, pt: (b, 2)),
            pl.BlockSpec((1, DA_V_DIM), lambda b, t, pt: (0, 0)),
        ] + [pl.BlockSpec((None, None, DA_W, PAGE_SIZE), k_map(g)) for g in range(G)]
          + [pl.BlockSpec((None, None, PAGE_SIZE * DA_HEADS, DA_V_DIM), v_map(g)) for g in range(G)],
        out_specs=pl.BlockSpec((dec_seq, DA_W), lambda b, t, pt: (b, 0)),
        scratch_shapes=[
            pltpu.VMEM((n_pages + 1, n_rows, PAGE_SIZE), F32),
            pltpu.VMEM((n_rows, PAGE_SIZE), F32),
            pltpu.VMEM((n_rows, PAGE_SIZE), F32),
            pltpu.VMEM((n_rows, DA_W), F32),
            pltpu.VMEM((PAGE_SIZE, DA_W), F32),
        ],
    )
    return pl.pallas_call(
        kern,
        grid_spec=grid_spec,
        out_shape=jax.ShapeDtypeStruct((DB * dec_seq, DA_W), F32),
        compiler_params=_cparams("parallel", "arbitrary"),
        name="diff_attn_sample",
    )(pt, da_lambda_e, qbd, knt, proj_s, subln_e, *([ckt] * G), *([cv] * G))


def _bmm(a, b):
    return jnp.einsum("gmk,gkn->gmn", a.astype(BF16), b.astype(BF16), preferred_element_type=F32)


def _bmm_nt(a, b):
    return jnp.einsum("gmk,gnk->gmn", a.astype(BF16), b.astype(BF16), preferred_element_type=F32)


def _bmm3(a, b):
    ah, al = _split2(a)
    bh, bl = _split2(b)
    d = functools.partial(jnp.einsum, "gmk,gkn->gmn", preferred_element_type=F32)
    return d(ah, bh) + (d(ah, bl) + d(al, bh))


def _tri_inverse(m_strict, C):
    ti = lax.broadcasted_iota(jnp.int32, (1, C, C), 1)
    sj = lax.broadcasted_iota(jnp.int32, (1, C, C), 2)
    eye = (ti == sj).astype(F32)
    a = jnp.where(ti // SUBLANES == sj // SUBLANES, -m_strict, 0.0)
    x = eye + a
    p = _bmm3(a, a)
    x = x + _bmm3(x, p)
    p = _bmm3(p, p)
    x = x + _bmm3(x, p)
    b = SUBLANES
    while b < C:
        join = jnp.logical_and(ti // (2 * b) == sj // (2 * b), ti // b != sj // b)
        c = jnp.where(join, m_strict, 0.0)
        x = x - _bmm3(x, _bmm3(c, x))
        b *= 2
    return x


def _gdn_kernel(*refs, C, NCH, has_state):
    if has_state:
        (q_ref, k_ref, v_ref, z_ref, ab_ref, cw_ref, alog_ref, dtb_ref, nw_ref,
         prev_ref, s0_ref, o_ref, sout_ref, buf_ref, s_ref) = refs
    else:
        (q_ref, k_ref, v_ref, z_ref, ab_ref, cw_ref, alog_ref, dtb_ref, nw_ref,
         o_ref, sout_ref, buf_ref, s_ref) = refs
    ci = pl.program_id(1)
    W = GDN_HEADS * GDN_DK
    H = GDN_HEADS
    R = C * NCH
    G = H * NCH

    @pl.when(ci == 0)
    def _():
        if has_state:
            buf_ref[0:SUBLANES, :] = prev_ref[...]
            s_ref[...] = s0_ref[...]
        else:
            buf_ref[0:SUBLANES, :] = jnp.zeros((SUBLANES, GDN_QKV), F32)
            s_ref[...] = jnp.zeros(s_ref.shape, F32)

    buf_ref[SUBLANES:SUBLANES + R, 0:W] = q_ref[...]
    buf_ref[SUBLANES:SUBLANES + R, W:2 * W] = k_ref[...]
    buf_ref[SUBLANES:SUBLANES + R, 2 * W:3 * W] = v_ref[...]
    y = buf_ref[SUBLANES:SUBLANES + R, :] * cw_ref[GDN_CONV - 1:GDN_CONV, :]
    for back in range(1, GDN_CONV):
        y = y + buf_ref[pl.ds(SUBLANES - back, R), :] * cw_ref[GDN_CONV - 1 - back:GDN_CONV - back, :]
    tail = buf_ref[R:R + SUBLANES, :]
    buf_ref[0:SUBLANES, :] = tail
    y = _silu(y)

    def heads(x, off, width):
        return jnp.stack([x[:, off + h * width:off + (h + 1) * width] for h in range(H)]).reshape(G, C, width)

    def chunk(x, c):
        return x.reshape((H, NCH) + x.shape[1:])[:, c]

    ab = ab_ref[...]
    sp = ab + dtb_ref[...]
    softplus = jnp.maximum(sp, 0.0) + jnp.log(1.0 + jnp.exp(-jnp.abs(sp)))
    g_all = -jnp.exp(alog_ref[...]) * softplus
    beta_all = _sigmoid(ab)

    ti = lax.broadcasted_iota(jnp.int32, (1, C, C), 1)
    sj = lax.broadcasted_iota(jnp.int32, (1, C, C), 2)
    causal = sj <= ti
    strict = sj < ti
    eye = ti == sj
    tri = causal[0].astype(BF16)
    bc_all = jnp.concatenate([_cumsum_rows(tri, g_all[c * C:(c + 1) * C]) for c in range(NCH)], axis=0)

    q3 = heads(y, 0, GDN_DK)
    k3 = heads(y, W, GDN_DK)
    v3 = heads(y, 2 * W, GDN_DV)
    q3 = q3 * lax.rsqrt(jnp.sum(q3 * q3, axis=-1, keepdims=True) + NORM_EPS) * (GDN_DK ** -0.5)
    k3 = k3 * lax.rsqrt(jnp.sum(k3 * k3, axis=-1, keepdims=True) + NORM_EPS)
    bcol = heads(bc_all, 0, 1)
    beta = heads(beta_all, H, 1)
    brow = jnp.sum(jnp.where(eye, bcol, 0.0), axis=1, keepdims=True)
    dec = jnp.exp(jnp.where(causal, bcol - brow, NEG_BIG))
    kb = k3 * beta
    eb = jnp.exp(bcol)
    m_strict = jnp.where(strict, _bmm_nt(kb, k3) * dec, 0.0)
    tinv = _tri_inverse(m_strict, C)
    uw = _bmm(tinv, jnp.concatenate([v3 * beta, kb * eb], axis=2))
    attn = _bmm_nt(q3, k3) * dec
    blast = bcol[:, C - 1:C, :]
    kd = k3 * jnp.exp(blast - bcol)
    gl = jnp.exp(blast)
    wq = jnp.concatenate([uw[:, :, GDN_DV:], q3 * eb], axis=1)
    z3 = heads(z_ref[...], 0, GDN_DV)

    s = s_ref[...]
    for c in range(NCH):
        ws = _bmm(chunk(wq, c), s)
        v_new = chunk(uw, c)[:, :, :GDN_DV] - ws[:, :C]
        o = ws[:, C:] + _bmm(chunk(attn, c), v_new)
        s = s * chunk(gl, c) + _bmm(jnp.swapaxes(chunk(kd, c), 1, 2), v_new)
        o = _rms_rows(o, nw_ref[...]) * _silu(chunk(z3, c))
        for h in range(H):
            o_ref[c * C:(c + 1) * C, h * GDN_DV:(h + 1) * GDN_DV] = o[h]
    s_ref[...] = s

    @pl.when(ci == pl.num_programs(1) - 1)
    def _():
        sout_ref[...] = s


def _gdn(proj, ab, conv_w_e, alog_pad, dtb_pad, norm_e, B, L, C, NCH=1, prev8=None, s0=None):
    has_state = prev8 is not None
    R = C * NCH
    nc = L // R
    W = GDN_HEADS * GDN_DK
    base = (3 * DA_W) // W
    row = lambda b, c: (b * nc + c, 0)
    in_specs = [
        pl.BlockSpec((R, W), lambda b, c: (b * nc + c, base)),
        pl.BlockSpec((R, W), lambda b, c: (b * nc + c, base + 1)),
        pl.BlockSpec((R, W), lambda b, c: (b * nc + c, base + 2)),
        pl.BlockSpec((R, W), lambda b, c: (b * nc + c, base + 3)),
        pl.BlockSpec((R, LANES), row),
        pl.BlockSpec((GDN_CONV, GDN_QKV), lambda b, c: (0, 0)),
        pl.BlockSpec((1, LANES), lambda b, c: (0, 0)),
        pl.BlockSpec((1, LANES), lambda b, c: (0, 0)),
        pl.BlockSpec((1, GDN_DV), lambda b, c: (0, 0)),
    ]
    args = [proj, proj, proj, proj, ab, conv_w_e, alog_pad, dtb_pad, norm_e]
    state_block = (None, GDN_HEADS, GDN_DK, GDN_DV)
    if has_state:
        in_specs += [pl.BlockSpec((None, SUBLANES, GDN_QKV), lambda b, c: (b, 0, 0)),
                     pl.BlockSpec(state_block, lambda b, c: (b, 0, 0, 0))]
        args += [prev8, s0]
    return pl.pallas_call(
        functools.partial(_gdn_kernel, C=C, NCH=NCH, has_state=has_state),
        grid=(B, nc),
        in_specs=in_specs,
        out_specs=[pl.BlockSpec((R, W), row),
                   pl.BlockSpec(state_block, lambda b, c: (b, 0, 0, 0))],
        out_shape=[jax.ShapeDtypeStruct((B * L, W), F32),
                   jax.ShapeDtypeStruct((B, GDN_HEADS, GDN_DK, GDN_DV), F32)],
        scratch_shapes=[pltpu.VMEM((SUBLANES + R, GDN_QKV), F32),
                        pltpu.VMEM((GDN_HEADS, GDN_DK, GDN_DV), F32)],
        compiler_params=_cparams("parallel", "arbitrary"),
        name="gdn_state" if has_state else "gdn_fresh",
    )(*args)


def _sub_block_rows(x4, b):
    sub = lax.broadcasted_iota(jnp.int32, (1, 1, SUBLANES, 1), 2)
    out = None
    for start in range(0, SUBLANES, b):
        mid = start + b // 2
        piece = jnp.broadcast_to(x4[:, :, mid:mid + 1, :], x4.shape)
        out = piece if out is None else jnp.where(sub >= start, piece, out)
    return out


def _hgrn_kernel(*refs, C, HB, SB, layer, has_state):
    if has_state:
        (q_ref, f_ref, i_ref, g_ref, lower_ref, nw_ref, s0_ref, o_ref, sout_ref, st_ref) = refs
    else:
        (q_ref, f_ref, i_ref, g_ref, lower_ref, nw_ref, o_ref, sout_ref, st_ref) = refs
    ci = pl.program_id(2)
    G = HB * SB
    K = HG_EXPAND

    @pl.when(ci == 0)
    def _():
        for h in range(HB):
            for sb in range(SB):
                st_ref[h * SB + sb] = s0_ref[sb, h].T if has_state else jnp.zeros((HG_DI, K), F32)

    def heads(x):
        return jnp.stack([x[:, h * K:(h + 1) * K] for h in range(HB)]).reshape(G, C, K)

    low = lower_ref[...]
    ex = jnp.exp(low - jnp.max(low, axis=0, keepdims=True))
    lb = jnp.sum(ex[1:layer + 1], axis=0, keepdims=True) / jnp.sum(ex, axis=0, keepdims=True)

    fg2 = lb + (1.0 - lb) * _sigmoid(f_ref[...])
    logf2 = jnp.log(fg2)
    ti = lax.broadcasted_iota(jnp.int32, (1, C, C), 1)
    sj = lax.broadcasted_iota(jnp.int32, (1, C, C), 2)
    tri = (sj[0] <= ti[0]).astype(BF16)
    bc2 = jnp.concatenate([_cumsum_rows(tri, logf2[sb * C:(sb + 1) * C]) for sb in range(SB)], axis=0)

    qt = heads(_silu(q_ref[...]) * (K ** -0.5))
    kk = heads(1.0 - fg2)
    v = heads(i_ref[...])
    bc = heads(bc2)
    st = st_ref[...]
    o = _bmm_nt(qt * jnp.exp(bc), st)

    ri = lax.broadcasted_iota(jnp.int32, (1, C, 1), 1)
    a = jnp.where(ti == sj, jnp.sum(qt * kk, axis=-1, keepdims=True), 0.0)
    bc4 = bc.reshape(G, C // SUBLANES, SUBLANES, K)
    b = 2
    while b <= C:
        if b <= SUBLANES:
            ref = _sub_block_rows(bc4, b).reshape(G, C, K)
        else:
            ref = jnp.concatenate(
                [jnp.broadcast_to(bc[:, m:m + 1, :], (G, b, K)) for m in range(b // 2, C, b)], axis=1)
        upper = (ri % b) >= (b // 2)
        d = bc - ref
        e = jnp.exp(jnp.where(upper, d, -d))
        qs = jnp.where(upper, qt * e, 0.0)
        ks = jnp.where(upper, 0.0, kk * e)
        a = a + jnp.where(ti // b == sj // b, _bmm_nt(qs, ks), 0.0)
        b *= 2
    o = o + _bmm(a, v)

    blast = bc[:, C - 1:C, :]
    st = st * jnp.exp(blast) + _bmm(jnp.swapaxes(v, 1, 2), kk * jnp.exp(blast - bc))
    st_ref[...] = st
    nw = jnp.stack([nw_ref[:, h * K:(h + 1) * K] for h in range(HB) for _ in range(SB)])
    o = _rms_rows(o, nw) * heads(_silu(g_ref[...]))
    for h in range(HB):
        for sb in range(SB):
            o_ref[sb * C:(sb + 1) * C, h * K:(h + 1) * K] = o[h * SB + sb]

    @pl.when(ci == pl.num_programs(2) - 1)
    def _():
        for h in range(HB):
            for sb in range(SB):
                sout_ref[sb, h] = st[h * SB + sb].T


def _hgrn(proj, hg_lower, hg_norm_o, layer, B, L, C, s0=None, HB=8, SB=1):
    has_state = s0 is not None
    nc = L // C
    assert SB == 1 or nc == 1
    nhg = HG_HEADS // HB
    Wb = HB * HG_EXPAND
    R = SB * C

    def col(section):
        return lambda b, hg, c: (b * nc + c, section * nhg + hg)

    in_specs = [pl.BlockSpec((R, Wb), col(s)) for s in range(4)] + [
        pl.BlockSpec((DEPTH, Wb), lambda b, hg, c: (0, hg)),
        pl.BlockSpec((1, Wb), lambda b, hg, c: (0, hg)),
    ]
    args = [proj, proj, proj, proj, hg_lower, hg_norm_o]
    state_block = (SB, HB, HG_EXPAND, HG_DI)
    if has_state:
        in_specs.append(pl.BlockSpec(state_block, lambda b, hg, c: (b, hg, 0, 0)))
        args.append(s0)
    return pl.pallas_call(
        functools.partial(_hgrn_kernel, C=C, HB=HB, SB=SB, layer=layer, has_state=has_state),
        grid=(B // SB, nhg, nc),
        in_specs=in_specs,
        out_specs=[pl.BlockSpec((R, Wb), col(0)),
                   pl.BlockSpec(state_block, lambda b, hg, c: (b, hg, 0, 0))],
        out_shape=[jax.ShapeDtypeStruct((B * L, D_MODEL), F32),
                   jax.ShapeDtypeStruct((B, HG_HEADS, HG_EXPAND, HG_DI), F32)],
        scratch_shapes=[pltpu.VMEM((HB * SB, HG_DI, HG_EXPAND), F32)],
        compiler_params=_cparams("parallel", "parallel", "arbitrary"),
        name="hgrn_state" if has_state else "hgrn_fresh",
    )(*args)


def _pad_lanes(v, offset=0):
    return jnp.zeros((1, LANES), F32).at[0, offset:offset + v.shape[0]].set(v.astype(F32))


def kernel(x_prompt, x_sample, cache_k, cache_v, state_gdn_conv, state_gdn, state_hgrn, page_table, norm_mix, norm_ffn, norm_final, w_in_even, w_out_even, da_lambda, da_subln, gdn_conv_w, gdn_a_log, gdn_dt_bias, gdn_norm, w_in_odd, w_out_odd, hg_lower, hg_norm, w_gate_up, w_down):
    Bp, Lp = x_prompt.shape[:2]
    DB, Ls = x_sample.shape[:2]
    past_len = page_table.shape[1] * PAGE_SIZE
    pos_p = np.tile(np.arange(Lp), Bp)
    pos_s = np.tile(past_len + np.arange(Ls), DB)
    Mp, Ms = Bp * Lp, DB * Ls
    tm_p, tm_s = 1024, Ms

    hp = x_prompt.reshape(Mp, D_MODEL)
    hs = x_sample.reshape(Ms, D_MODEL)
    outs = {k: [] for k in ("k_p", "v_p", "conv_p", "gdn_p", "hg_p", "k_s", "v_s", "conv_s", "gdn_s", "hg_s")}

    for l in range(DEPTH):
        nw_mix = norm_mix[l].reshape(1, D_MODEL)
        nw_ffn = norm_ffn[l].reshape(1, D_MODEL)
        if l % 2 == 0:
            e = l // 2
            lam_init = 0.8 - 0.6 * math.exp(-0.3 * l)
            wab = jnp.pad(w_in_even[e, :, EVEN_MAIN:], ((0, 0), (0, LANES - 2 * GDN_HEADS)))
            alog_pad = _pad_lanes(gdn_a_log[e])
            dtb_pad = _pad_lanes(gdn_dt_bias[e])
            sub = da_subln[e].reshape(1, DA_V_DIM)
            gnorm = gdn_norm[e].reshape(1, GDN_DV)
            proj, ab = _inproj_even(hp, nw_mix, w_in_even, e, wab, pos_p, tm_p)
            oa = _attn_prompt(proj, da_lambda[e], sub, Bp, Lp, lam_init)
            gdn_c = min(GDN_CHUNK, Lp)
            ob, st = _gdn(proj, ab, gdn_conv_w[e], alog_pad, dtb_pad, gnorm, Bp, Lp, gdn_c,
                          NCH=2 if Lp % (2 * gdn_c) == 0 else 1)
            mix = jnp.concatenate([oa.astype(BF16), ob.astype(BF16)], axis=1)
            hp = _mm_res(mix, w_out_even, e, hp, tm_p)
            outs["k_p"].append(proj[:, DA_W:2 * DA_W].reshape(Bp, Lp, DA_HEADS, 2, DA_QK_DIM))
            outs["v_p"].append(proj[:, 2 * DA_W:3 * DA_W].reshape(Bp, Lp, DA_HEADS, DA_V_DIM))
            outs["conv_p"].append(
                proj.reshape(Bp, Lp, EVEN_MAIN)[:, Lp - (GDN_CONV - 1):, 3 * DA_W:3 * DA_W + GDN_QKV])
            outs["gdn_p"].append(st)
            proj, ab = _inproj_even(hs, nw_mix, w_in_even, e, wab, pos_s, tm_s)
            oa = _attn_sample(proj, cache_k, cache_v, e, page_table, da_lambda[e], sub, DB, Ls, lam_init)
            prev8 = jnp.pad(state_gdn_conv[e], ((0, 0), (SUBLANES - (GDN_CONV - 1), 0), (0, 0)))
            ob, st = _gdn(proj, ab, gdn_conv_w[e], alog_pad, dtb_pad, gnorm, DB, Ls, min(GDN_CHUNK, Ls),
                          prev8=prev8, s0=state_gdn[e])
            mix = jnp.concatenate([oa.astype(BF16), ob.astype(BF16)], axis=1)
            hs = _mm_res(mix, w_out_even, e, hs, tm_s)
            gq = proj[:, 3 * DA_W:3 * DA_W + GDN_QKV].reshape(DB, Ls, GDN_QKV)
            conv_s = jnp.concatenate([state_gdn_conv[e], gq], axis=1)[:, Ls:]
            outs["k_s"].append(proj[:, DA_W:2 * DA_W].reshape(DB, Ls, DA_HEADS, 2, DA_QK_DIM))
            outs["v_s"].append(proj[:, 2 * DA_W:3 * DA_W].reshape(DB, Ls, DA_HEADS, DA_V_DIM))
            outs["conv_s"].append(conv_s)
            outs["gdn_s"].append(st)
        else:
            o = l // 2
            hnorm = hg_norm[o].reshape(1, D_MODEL)
            proj = _norm_matmul(hp, nw_mix, w_in_odd, o, tm_p)
            y, st = _hgrn(proj, hg_lower, hnorm, l, Bp, Lp, min(HG_CHUNK, Lp))
            hp = _mm_res(y.astype(BF16), w_out_odd, o, hp, tm_p)
            outs["hg_p"].append(st)
            proj = _norm_matmul(hs, nw_mix, w_in_odd, o, tm_s)
            y, st = _hgrn(proj, hg_lower, hnorm, l, DB, Ls, min(HG_CHUNK, Ls), s0=state_hgrn[o],
                          SB=4 if Ls <= HG_CHUNK and DB % 4 == 0 else 1)
            hs = _mm_res(y.astype(BF16), w_out_odd, o, hs, tm_s)
            outs["hg_s"].append(st)
        act = _ffn_up(hp, nw_ffn, w_gate_up, l, tm_p)
        hp = _mm_res(act, w_down, l, hp, tm_p, tk=D_FF // 2)
        act = _ffn_up(hs, nw_ffn, w_gate_up, l, tm_s)
        hs = _mm_res(act, w_down, l, hs, tm_s, tk=D_FF // 2)

    nwf = norm_final.reshape(1, D_MODEL)
    y_prompt = _final_norm(hp, nwf, tm_p).reshape(Bp, Lp, D_MODEL)
    y_sample = _final_norm(hs, nwf, tm_s).reshape(DB, Ls, D_MODEL)
    st = lambda k: jnp.stack(outs[k])
    return (y_prompt, y_sample, st("k_p"), st("v_p"), st("conv_p"), st("gdn_p"), st("hg_p"),
            st("k_s"), st("v_s"), st("conv_s"), st("gdn_s"), st("hg_s"))
```

```python
import functools
import math

import numpy as np
import jax
import jax.numpy as jnp
from jax import lax
from jax.experimental import pallas as pl
from jax.experimental.pallas import tpu as pltpu

F32 = jnp.float32
BF16 = jnp.bfloat16

D_MODEL = 2048
DEPTH = 2
PAGE_SIZE = 128
NORM_EPS = 1e-6

DA_HEADS = 8
DA_QK_DIM = 64
DA_V_DIM = 2 * DA_QK_DIM
DA_SCALE = DA_QK_DIM ** -0.5
ROT_DIM = DA_QK_DIM // 4
ROPE_THETA = 500000.0

GDN_HEADS = 8
GDN_DK = 128
GDN_DV = 128
GDN_QKV = GDN_HEADS * (2 * GDN_DK + GDN_DV)
GDN_CONV = 4
GDN_CHUNK = 64

HG_EXPAND = 128
HG_HEADS = D_MODEL // HG_EXPAND
HG_DI = D_MODEL // HG_HEADS
HG_CHUNK = 64

D_FF = -(-(8 * D_MODEL) // (3 * 256)) * 256

DA_W = DA_HEADS * 2 * DA_QK_DIM
EVEN_MAIN = 3 * DA_W + GDN_QKV + GDN_HEADS * GDN_DV
EVEN_ROWS = EVEN_MAIN - DA_W

LANES = 128
SUBLANES = 8
VMEM_LIMIT = 56 * 1024 * 1024
NEG_BIG = -1e30


def _cparams(*sem):
    return pltpu.CompilerParams(dimension_semantics=sem, vmem_limit_bytes=VMEM_LIMIT)


def _split2(x):
    hi = x.astype(BF16)
    lo = (x - hi.astype(F32)).astype(BF16)
    return hi, lo


def _cumsum_rows(tri, x):
    hi = x.astype(BF16)
    r = x - hi.astype(F32)
    mid = r.astype(BF16)
    lo = (r - mid.astype(F32)).astype(BF16)
    d = functools.partial(jnp.dot, preferred_element_type=F32)
    return d(tri, hi) + (d(tri, mid) + d(tri, lo))


def _sigmoid(x):
    return 0.5 * jnp.tanh(0.5 * x) + 0.5


def _silu(x):
    return x * _sigmoid(x)


def _rms_rows(x, w):
    return x * lax.rsqrt(jnp.mean(x * x, axis=-1, keepdims=True) + NORM_EPS) * w


W_SLABS = 4


def _slab_specs(k_rows, tn, index, transposed=False):
    def spec(s):
        def index_map(*g):
            layer, kt, col = index(*g)
            return (layer, col * W_SLABS + s, kt) if transposed else (layer, kt * W_SLABS + s, col)
        block = (None, tn // W_SLABS, k_rows) if transposed else (None, k_rows // W_SLABS, tn)
        return pl.BlockSpec(block, index_map)
    return [spec(s) for s in range(W_SLABS)]


def _dot_nt(a, b):
    return lax.dot_general(a, b, (((1,), (1,)), ((), ())), preferred_element_type=F32)


def _load_weight(w_refs, wbuf_ref):
    rows = wbuf_ref.shape[0] // len(w_refs)
    for s, r in enumerate(w_refs):
        wbuf_ref[s * rows:(s + 1) * rows, :] = r[...].astype(BF16)
    return wbuf_ref[...]


def _inproj_even_kernel(x_ref, nw_ref, *rest, n_q_tiles, n_k_tiles, tn):
    w_refs = rest[:W_SLABS]
    (wab_ref, cos_ref, sina_ref, sinb_ref, cost_ref, sinat_ref, sinbt_ref,
     o_ref, kt_ref, ab_ref, xn_ref, wbuf_ref) = rest[W_SLABS:]
    j = pl.program_id(1)
    half = ROT_DIM // 2

    @pl.when(j == 0)
    def _():
        xn = _rms_rows(x_ref[...], nw_ref[...]).astype(BF16)
        xn_ref[...] = xn
        ab_ref[...] = _dot_nt(xn, wab_ref[...].astype(BF16))

    w = _load_weight(w_refs, wbuf_ref)

    @pl.when(j < n_q_tiles)
    def _():
        acc = _dot_nt(xn_ref[...], w)
        cosf, sina, sinb = cos_ref[...], sina_ref[...], sinb_ref[...]
        for c in range(tn // LANES):
            a = acc[:, c * LANES:(c + 1) * LANES]
            o_ref[:, c * LANES:(c + 1) * LANES] = (
                a * cosf + pltpu.roll(a, LANES - half, 1) * sina + pltpu.roll(a, half, 1) * sinb)

    @pl.when(jnp.logical_and(j >= n_q_tiles, j < n_q_tiles + n_k_tiles))
    def _():
        acc = _dot_nt(w, xn_ref[...])
        cosf, sina, sinb = cost_ref[...], sinat_ref[...], sinbt_ref[...]
        for c in range(tn // LANES):
            a = acc[c * LANES:(c + 1) * LANES, :]
            kt_ref[c * LANES:(c + 1) * LANES, :] = (
                a * cosf + pltpu.roll(a, LANES - half, 0) * sina + pltpu.roll(a, half, 0) * sinb)

    @pl.when(j >= n_q_tiles + n_k_tiles)
    def _():
        o_ref[...] = _dot_nt(xn_ref[...], w)


def _rope_tables(pos):
    half = ROT_DIM // 2
    inv = np.power(np.float32(ROPE_THETA), -np.arange(half, dtype=np.float32) * np.float32(2.0) / np.float32(ROT_DIM))
    ang = (pos.astype(np.float32)[:, None] * inv[None, :]).astype(np.float32)
    cos = np.cos(ang.astype(np.float64)).astype(np.float32)
    sin = np.sin(ang.astype(np.float64)).astype(np.float32)
    n = pos.shape[0]
    cosf = np.ones((n, LANES), np.float32)
    sina = np.zeros((n, LANES), np.float32)
    sinb = np.zeros((n, LANES), np.float32)
    for base in range(0, LANES, DA_QK_DIM):
        cosf[:, base:base + half] = cos
        cosf[:, base + half:base + 2 * half] = cos
        sina[:, base:base + half] = -sin
        sinb[:, base + half:base + 2 * half] = sin
    return cosf, sina, sinb


def _inproj_even(x, nw, w_in_t, e, wab, pos, B, tm, tn=512):
    M = x.shape[0]
    L = M // B
    lt = L // tm
    tables = _rope_tables(pos)
    lane_tabs = [jnp.asarray(t) for t in tables]
    sub_tabs = [jnp.asarray(np.ascontiguousarray(t.T)) for t in tables]
    n_q, n_k = DA_W // tn, DA_W // tn
    n_tiles = EVEN_MAIN // tn
    kern = functools.partial(_inproj_even_kernel, n_q_tiles=n_q, n_k_tiles=n_k, tn=tn)
    row = lambda i, j: (i, 0)
    col = lambda i, j: (0, i)
    return pl.pallas_call(
        kern,
        grid=(M // tm, n_tiles),
        in_specs=[
            pl.BlockSpec((tm, D_MODEL), row),
            pl.BlockSpec((1, D_MODEL), lambda i, j: (0, 0)),
        ] + _slab_specs(D_MODEL, tn, lambda i, j: (e, 0, j), transposed=True) + [
            pl.BlockSpec((LANES, D_MODEL), lambda i, j: (0, 0)),
            pl.BlockSpec((tm, LANES), row),
            pl.BlockSpec((tm, LANES), row),
            pl.BlockSpec((tm, LANES), row),
            pl.BlockSpec((LANES, tm), col),
            pl.BlockSpec((LANES, tm), col),
            pl.BlockSpec((LANES, tm), col),
        ],
        out_specs=[
            pl.BlockSpec((tm, tn), lambda i, j: (i, jnp.where(j < n_q, j, jnp.maximum(j - n_k, n_q - 1)))),
            pl.BlockSpec((None, tn, tm), lambda i, j: (i // lt, jnp.clip(j - n_q, 0, n_k - 1), i % lt)),
            pl.BlockSpec((tm, LANES), row),
        ],
        out_shape=[jax.ShapeDtypeStruct((M, EVEN_ROWS), F32),
                   jax.ShapeDtypeStruct((B, DA_W, L), F32),
                   jax.ShapeDtypeStruct((M, LANES), F32)],
        scratch_shapes=[pltpu.VMEM((tm, D_MODEL), BF16), pltpu.VMEM((tn, D_MODEL), BF16)],
        compiler_params=_cparams("parallel", "arbitrary"),
        name="inproj_even",
    )(x, nw, *([w_in_t] * W_SLABS), wab, *lane_tabs, *sub_tabs)


def _norm_matmul_kernel(x_ref, nw_ref, *rest):
    w_refs = rest[:W_SLABS]
    o_ref, xn_ref, wbuf_ref = rest[W_SLABS:]

    @pl.when(pl.program_id(1) == 0)
    def _():
        xn_ref[...] = _rms_rows(x_ref[...], nw_ref[...]).astype(BF16)

    o_ref[...] = jnp.dot(xn_ref[...], _load_weight(w_refs, wbuf_ref), preferred_element_type=F32)


def _norm_matmul(x, nw, w_all, l, tm, tn=512):
    M = x.shape[0]
    N = w_all.shape[-1]
    return pl.pallas_call(
        _norm_matmul_kernel,
        grid=(M // tm, N // tn),
        in_specs=[
            pl.BlockSpec((tm, D_MODEL), lambda i, j: (i, 0)),
            pl.BlockSpec((1, D_MODEL), lambda i, j: (0, 0)),
        ] + _slab_specs(D_MODEL, tn, lambda i, j: (l, 0, j)),
        out_specs=pl.BlockSpec((tm, tn), lambda i, j: (i, j)),
        out_shape=jax.ShapeDtypeStruct((M, N), F32),
        scratch_shapes=[pltpu.VMEM((tm, D_MODEL), BF16), pltpu.VMEM((D_MODEL, tn), BF16)],
        compiler_params=_cparams("parallel", "arbitrary"),
        name="norm_matmul",
    )(x, nw, *([w_all] * W_SLABS))


def _ffn_up_kernel(x_ref, nw_ref, *rest):
    wg_refs = rest[:W_SLABS]
    wu_refs = rest[W_SLABS:2 * W_SLABS]
    o_ref, xn_ref, wgbuf_ref, wubuf_ref = rest[2 * W_SLABS:]

    @pl.when(pl.program_id(1) == 0)
    def _():
        xn_ref[...] = _rms_rows(x_ref[...], nw_ref[...]).astype(BF16)

    xn = xn_ref[...]
    g = jnp.dot(xn, _load_weight(wg_refs, wgbuf_ref), preferred_element_type=F32)
    u = jnp.dot(xn, _load_weight(wu_refs, wubuf_ref), preferred_element_type=F32)
    o_ref[...] = (_silu(g) * u).astype(BF16)


def _ffn_up(x, nw, w_gate_up, l, tm, tn=512):
    M = x.shape[0]
    nj = D_FF // tn
    return pl.pallas_call(
        _ffn_up_kernel,
        grid=(M // tm, nj),
        in_specs=[
            pl.BlockSpec((tm, D_MODEL), lambda i, j: (i, 0)),
            pl.BlockSpec((1, D_MODEL), lambda i, j: (0, 0)),
        ] + _slab_specs(D_MODEL, tn, lambda i, j: (l, 0, j))
          + _slab_specs(D_MODEL, tn, lambda i, j: (l, 0, j + nj)),
        out_specs=pl.BlockSpec((tm, tn), lambda i, j: (i, j)),
        out_shape=jax.ShapeDtypeStruct((M, D_FF), BF16),
        scratch_shapes=[pltpu.VMEM((tm, D_MODEL), BF16), pltpu.VMEM((D_MODEL, tn), BF16),
                        pltpu.VMEM((D_MODEL, tn), BF16)],
        compiler_params=_cparams("parallel", "arbitrary"),
        name="ffn_up",
    )(x, nw, *([w_gate_up] * (2 * W_SLABS)))


def _mm_res_kernel(*refs, nk, na):
    a_refs = refs[:na]
    w_refs = refs[na:na + W_SLABS]
    r_ref, o_ref, acc_ref, wbuf_ref, abuf_ref = refs[na + W_SLABS:]
    k = pl.program_id(2)
    if na == 1:
        a = a_refs[0][...].astype(BF16)
    else:
        off = 0
        for r in a_refs:
            abuf_ref[:, off:off + r.shape[1]] = r[...].astype(BF16)
            off += r.shape[1]
        a = abuf_ref[...]
    p = jnp.dot(a, _load_weight(w_refs, wbuf_ref), preferred_element_type=F32)
    if nk == 1:
        o_ref[...] = r_ref[...] + p
    else:
        @pl.when(k == 0)
        def _():
            acc_ref[...] = p

        @pl.when(jnp.logical_and(k > 0, k < nk - 1))
        def _():
            acc_ref[...] += p

        @pl.when(k == nk - 1)
        def _():
            o_ref[...] = r_ref[...] + (acc_ref[...] + p)


def _mm_res(a_list, w_all, l, res, tm, tn=512, tk=None):
    M = res.shape[0]
    K = sum(a.shape[1] for a in a_list)
    N = w_all.shape[-1]
    tk = K if tk is None else tk
    nk = K // tk
    na = len(a_list)
    assert na == 1 or nk == 1
    if na == 1:
        a_specs = [pl.BlockSpec((tm, tk), lambda i, j, k: (i, k))]
    else:
        a_specs = [pl.BlockSpec((tm, a.shape[1]), lambda i, j, k: (i, 0)) for a in a_list]
    return pl.pallas_call(
        functools.partial(_mm_res_kernel, nk=nk, na=na),
        grid=(M // tm, N // tn, nk),
        in_specs=a_specs + _slab_specs(tk, tn, lambda i, j, k: (l, k, j))
                 + [pl.BlockSpec((tm, tn), lambda i, j, k: (i, j))],
        out_specs=pl.BlockSpec((tm, tn), lambda i, j, k: (i, j)),
        out_shape=jax.ShapeDtypeStruct((M, N), F32),
        scratch_shapes=[pltpu.VMEM((tm, tn), F32), pltpu.VMEM((tk, tn), BF16),
                        pltpu.VMEM((tm, tk) if na > 1 else (SUBLANES * 2, LANES), BF16)],
        compiler_params=_cparams("parallel", "parallel", "arbitrary"),
        name="matmul_residual",
    )(*a_list, *([w_all] * W_SLABS), res)


def _final_norm_kernel(x_ref, w_ref, o_ref):
    o_ref[...] = _rms_rows(x_ref[...], w_ref[...])


def _final_norm(x, w, tm):
    M = x.shape[0]
    return pl.pallas_call(
        _final_norm_kernel,
        grid=(M // tm,),
        in_specs=[pl.BlockSpec((tm, D_MODEL), lambda i: (i, 0)),
                  pl.BlockSpec((1, D_MODEL), lambda i: (0, 0))],
        out_specs=pl.BlockSpec((tm, D_MODEL), lambda i: (i, 0)),
        out_shape=jax.ShapeDtypeStruct((M, D_MODEL), F32),
        compiler_params=_cparams("parallel"),
        name="final_norm",
    )(x, w)


def _lambda_of(lam_ref, lam_init):
    lf = lam_ref[...]
    s1 = jnp.sum(lf[0:1] * lf[1:2], axis=-1, keepdims=True)
    s2 = jnp.sum(lf[2:3] * lf[3:4], axis=-1, keepdims=True)
    return jnp.exp(s1) - jnp.exp(s2) + lam_init


def _attn_prompt_kernel(qi_ref, ki_ref, lam_ref, q_ref, k_ref, v_ref, sub_ref, o_ref,
                        m_ref, l_ref, acc_ref, *, t, lam_init):
    step = pl.program_id(2)
    qi = qi_ref[step]
    ki = ki_ref[step]

    @pl.when(ki == 0)
    def _():
        m_ref[...] = jnp.full(m_ref.shape, NEG_BIG, F32)
        l_ref[...] = jnp.zeros(l_ref.shape, F32)
        acc_ref[...] = jnp.zeros(acc_ref.shape, F32)

    def update(diagonal):
        q = q_ref[...] * DA_SCALE
        kt = k_ref[...].astype(BF16)
        v = v_ref[...].astype(BF16)
        lane_map = lax.broadcasted_iota(jnp.int32, (1, LANES), 1) // DA_QK_DIM
        if diagonal:
            causal = (lax.broadcasted_iota(jnp.int32, (1, t), 1)
                      <= lax.broadcasted_iota(jnp.int32, (t, 1), 0))
        for m in range(2):
            qm = jnp.where(lane_map == m, q, 0.0).astype(BF16)
            s = jnp.dot(qm, kt, preferred_element_type=F32)
            if diagonal:
                s = jnp.where(causal, s, NEG_BIG)
            m_prev = m_ref[m]
            m_next = jnp.maximum(m_prev, jnp.max(s, axis=1, keepdims=True))
            alpha = jnp.exp(m_prev - m_next)
            p = jnp.exp(s - jnp.concatenate([m_next] * (t // LANES), axis=1))
            l_ref[m] = alpha * l_ref[m] + jnp.sum(p, axis=1, keepdims=True)
            acc_ref[m] = alpha * acc_ref[m] + jnp.dot(p.astype(BF16), v, preferred_element_type=F32)
            m_ref[m] = m_next

    @pl.when(ki < qi)
    def _():
        update(False)

    @pl.when(ki == qi)
    def _():
        update(True)
        lam = _lambda_of(lam_ref, lam_init)
        o = acc_ref[0] / l_ref[0] - lam * (acc_ref[1] / l_ref[1])
        o_ref[...] = (_rms_rows(o, sub_ref[...]) * (1.0 - lam_init)).astype(o_ref.dtype)


def _attn_prompt(rows, kt, da_lambda_e, subln_e, B, S, lam_init, t=512):
    nq = S // t
    pairs = [(qi, ki) for qi in range(nq) for ki in range(qi + 1)]
    qi_tab = jnp.asarray([p[0] for p in pairs], jnp.int32)
    ki_tab = jnp.asarray([p[1] for p in pairs], jnp.int32)
    kern = functools.partial(_attn_prompt_kernel, t=t, lam_init=lam_init)
    grid_spec = pltpu.PrefetchScalarGridSpec(
        num_scalar_prefetch=2,
        grid=(B, DA_HEADS, len(pairs)),
        in_specs=[
            pl.BlockSpec((4, DA_QK_DIM), lambda b, h, s, qt, kt: (0, 0)),
            pl.BlockSpec((t, LANES), lambda b, h, s, qt, kt: (b * nq + qt[s], h)),
            pl.BlockSpec((None, LANES, t), lambda b, h, s, qt, kt: (b, h, kt[s])),
            pl.BlockSpec((t, LANES), lambda b, h, s, qt, kt: (b * nq + kt[s], DA_HEADS + h)),
            pl.BlockSpec((1, DA_V_DIM), lambda b, h, s, qt, kt: (0, 0)),
        ],
        out_specs=pl.BlockSpec((t, LANES), lambda b, h, s, qt, kt: (b * nq + qt[s], h)),
        scratch_shapes=[pltpu.VMEM((2, t, LANES), F32), pltpu.VMEM((2, t, LANES), F32),
                        pltpu.VMEM((2, t, DA_V_DIM), F32)],
    )
    return pl.pallas_call(
        kern,
        grid_spec=grid_spec,
        out_shape=jax.ShapeDtypeStruct((B * S, DA_W), BF16),
        compiler_params=_cparams("parallel", "parallel", "arbitrary"),
        name="diff_attn_prompt",
    )(qi_tab, ki_tab, da_lambda_e, rows, kt, rows, subln_e)


def _attn_sample_kernel(pt_ref, lam_ref, qbd_ref, knt_ref, vn_ref, sub_ref, *rest,
                        G, n_steps, dec_seq, lam_init):
    k_refs = rest[:G]
    v_refs = rest[G:2 * G]
    o_ref = rest[2 * G]
    st_ref, mx_ref, psum_ref, oacc_ref, vpad_ref = rest[2 * G + 1:]
    t = pl.program_id(1)
    n_pages = n_steps * G

    @pl.when(t == 0)
    def _():
        mx_ref[...] = jnp.full(mx_ref.shape, NEG_BIG, F32)
        vpad_ref[...] = jnp.zeros(vpad_ref.shape, F32)
        vpad_ref[0:dec_seq, :] = vn_ref[...]

    def scores(kt_page, idx, mask=None):
        s = jnp.dot(qbd_ref[...], kt_page.astype(BF16), preferred_element_type=F32)
        if mask is not None:
            s = jnp.where(mask, s, NEG_BIG)
        st_ref[idx] = s
        mx_ref[...] = jnp.maximum(mx_ref[...], s)

    @pl.when(t < n_steps)
    def _():
        for g in range(G):
            scores(k_refs[g][...], t * G + g)

    @pl.when(t == n_steps - 1)
    def _():
        qry = lax.broadcasted_iota(jnp.int32, (LANES, PAGE_SIZE), 0) % dec_seq
        key = lax.broadcasted_iota(jnp.int32, (LANES, PAGE_SIZE), 1)
        scores(knt_ref[...], n_pages, mask=key <= qry)
        row_max = jnp.max(mx_ref[...], axis=1, keepdims=True)
        mx_ref[...] = jnp.broadcast_to(row_max, mx_ref.shape)
        psum_ref[...] = jnp.zeros(psum_ref.shape, F32)
        oacc_ref[...] = jnp.zeros(oacc_ref.shape, F32)

    def accumulate(v_flat, idx):
        p = jnp.exp(st_ref[idx] - mx_ref[...])
        psum_ref[...] += p
        oacc_ref[...] += jnp.dot(p.astype(BF16), v_flat.astype(BF16), preferred_element_type=F32)

    def flat_values(v_ref):
        return jnp.concatenate(
            [v_ref[pl.ds(h, PAGE_SIZE, stride=DA_HEADS), :] for h in range(DA_HEADS)], axis=1)

    @pl.when(t >= n_steps)
    def _():
        for g in range(G):
            accumulate(flat_values(v_refs[g]), (t - n_steps) * G + g)

    @pl.when(t == 2 * n_steps - 1)
    def _():
        accumulate(vpad_ref[...], n_pages)
        lam = _lambda_of(lam_ref, lam_init)
        denom = jnp.sum(psum_ref[...], axis=1, keepdims=True)
        rows_per_head = 2 * dec_seq
        for h in range(DA_HEADS):
            r0 = h * rows_per_head
            blk = oacc_ref[r0:r0 + rows_per_head, h * DA_V_DIM:(h + 1) * DA_V_DIM]
            n = blk / denom[r0:r0 + rows_per_head]
            o = n[0:dec_seq] - lam * n[dec_seq:rows_per_head]
            o_ref[:, h * DA_V_DIM:(h + 1) * DA_V_DIM] = _rms_rows(o, sub_ref[...]) * (1.0 - lam_init)


def _attn_sample(proj_s, kt_s, cache_k, cache_v, e, page_table, da_lambda_e, subln_e, DB, dec_seq, lam_init, G=16):
    n_pages = page_table.shape[1]
    n_steps = n_pages // G
    n_rows = DA_HEADS * 2 * dec_seq
    assert n_rows == LANES and n_pages % G == 0
    n_layers, n_phys = cache_k.shape[:2]
    q = proj_s[:, :DA_W].reshape(DB, dec_seq, DA_HEADS * 2, DA_QK_DIM).transpose(0, 2, 1, 3)
    eye = jnp.eye(DA_HEADS * 2, dtype=F32)
    qbd = (q[:, :, :, None, :] * eye[None, :, None, :, None]).reshape(DB, n_rows, DA_W)
    qbd = (qbd * DA_SCALE).astype(BF16)
    knt = kt_s.reshape(DA_W, DB, dec_seq).transpose(1, 0, 2)
    knt = jnp.pad(knt, ((0, 0), (0, 0), (0, PAGE_SIZE - dec_seq))).astype(BF16)
    ckt = jnp.transpose(cache_k, (0, 1, 3, 4, 5, 2)).reshape(n_layers, n_phys, DA_W, PAGE_SIZE)
    cv = cache_v.reshape(n_layers, n_phys, PAGE_SIZE * DA_HEADS, DA_V_DIM)
    pt = page_table.reshape(-1)

    def k_map(g):
        return lambda b, t, pt: (e, pt[b * n_pages + jnp.minimum(t, n_steps - 1) * G + g], 0, 0)

    def v_map(g):
        return lambda b, t, pt: (e, pt[b * n_pages + jnp.maximum(t - n_steps, 0) * G + g], 0, 0)

    kern = functools.partial(_attn_sample_kernel, G=G, n_steps=n_steps, dec_seq=dec_seq, lam_init=lam_init)
    grid_spec = pltpu.PrefetchScalarGridSpec(
        num_scalar_prefetch=1,
        grid=(DB, 2 * n_steps),
        in_specs=[
            pl.BlockSpec((4, DA_QK_DIM), lambda b, t, pt: (0, 0)),
            pl.BlockSpec((None, n_rows, DA_W), lambda b, t, pt: (b, 0, 0)),
            pl.BlockSpec((None, DA_W, PAGE_SIZE), lambda b, t, pt: (b, 0, 0)),
            pl.BlockSpec((dec_seq, DA_W), lambda b, t, pt: (b, 1)),
            pl.BlockSpec((1, DA_V_DIM), lambda b, t, pt: (0, 0)),
        ] + [pl.BlockSpec((None, None, DA_W, PAGE_SIZE), k_map(g)) for g in range(G)]
          + [pl.BlockSpec((None, None, PAGE_SIZE * DA_HEADS, DA_V_DIM), v_map(g)) for g in range(G)],
        out_specs=pl.BlockSpec((dec_seq, DA_W), lambda b, t, pt: (b, 0)),
        scratch_shapes=[
            pltpu.VMEM((n_pages + 1, n_rows, PAGE_SIZE), F32),
            pltpu.VMEM((n_rows, PAGE_SIZE), F32),
            pltpu.VMEM((n_rows, PAGE_SIZE), F32),
            pltpu.VMEM((n_rows, DA_W), F32),
            pltpu.VMEM((PAGE_SIZE, DA_W), F32),
        ],
    )
    return pl.pallas_call(
        kern,
        grid_spec=grid_spec,
        out_shape=jax.ShapeDtypeStruct((DB * dec_seq, DA_W), F32),
        compiler_params=_cparams("parallel", "arbitrary"),
        name="diff_attn_sample",
    )(pt, da_lambda_e, qbd, knt, proj_s, subln_e, *([ckt] * G), *([cv] * G))


def _bmm(a, b):
    return jnp.einsum("gmk,gkn->gmn", a.astype(BF16), b.astype(BF16), preferred_element_type=F32)


def _bmm_nt(a, b):
    return jnp.einsum("gmk,gnk->gmn", a.astype(BF16), b.astype(BF16), preferred_element_type=F32)


def _bmm3(a, b):
    ah, al = _split2(a)
    bh, bl = _split2(b)
    d = functools.partial(jnp.einsum, "gmk,gkn->gmn", preferred_element_type=F32)
    return d(ah, bh) + (d(ah, bl) + d(al, bh))


def _tri_inverse(m_strict, C):
    ti = lax.broadcasted_iota(jnp.int32, (1, C, C), 1)
    sj = lax.broadcasted_iota(jnp.int32, (1, C, C), 2)
    eye = (ti == sj).astype(F32)
    a = jnp.where(ti // SUBLANES == sj // SUBLANES, -m_strict, 0.0)
    x = eye + a
    p = _bmm3(a, a)
    x = x + _bmm3(x, p)
    p = _bmm3(p, p)
    x = x + _bmm3(x, p)
    b = SUBLANES
    while b < C:
        join = jnp.logical_and(ti // (2 * b) == sj // (2 * b), ti // b != sj // b)
        c = jnp.where(join, m_strict, 0.0)
        x = x - _bmm3(x, _bmm3(c, x))
        b *= 2
    return x


def _gdn_kernel(*refs, C, NCH, has_state):
    if has_state:
        (q_ref, k_ref, v_ref, z_ref, ab_ref, cw_ref, alog_ref, dtb_ref, nw_ref,
         prev_ref, s0_ref, o_ref, sout_ref, buf_ref, s_ref) = refs
    else:
        (q_ref, k_ref, v_ref, z_ref, ab_ref, cw_ref, alog_ref, dtb_ref, nw_ref,
         o_ref, sout_ref, buf_ref, s_ref) = refs
    ci = pl.program_id(1)
    W = GDN_HEADS * GDN_DK
    H = GDN_HEADS
    R = C * NCH
    G = H * NCH

    @pl.when(ci == 0)
    def _():
        if has_state:
            buf_ref[0:SUBLANES, :] = prev_ref[...]
            s_ref[...] = s0_ref[...]
        else:
            buf_ref[0:SUBLANES, :] = jnp.zeros((SUBLANES, GDN_QKV), F32)
            s_ref[...] = jnp.zeros(s_ref.shape, F32)

    buf_ref[SUBLANES:SUBLANES + R, 0:W] = q_ref[...]
    buf_ref[SUBLANES:SUBLANES + R, W:2 * W] = k_ref[...]
    buf_ref[SUBLANES:SUBLANES + R, 2 * W:3 * W] = v_ref[...]
    y = buf_ref[SUBLANES:SUBLANES + R, :] * cw_ref[GDN_CONV - 1:GDN_CONV, :]
    for back in range(1, GDN_CONV):
        y = y + buf_ref[pl.ds(SUBLANES - back, R), :] * cw_ref[GDN_CONV - 1 - back:GDN_CONV - back, :]
    tail = buf_ref[R:R + SUBLANES, :]
    buf_ref[0:SUBLANES, :] = tail
    y = _silu(y)

    def heads(x, off, width):
        return jnp.stack([x[:, off + h * width:off + (h + 1) * width] for h in range(H)]).reshape(G, C, width)

    def chunk(x, c):
        return x.reshape((H, NCH) + x.shape[1:])[:, c]

    ab = ab_ref[...]
    sp = ab + dtb_ref[...]
    softplus = jnp.maximum(sp, 0.0) + jnp.log(1.0 + jnp.exp(-jnp.abs(sp)))
    g_all = -jnp.exp(alog_ref[...]) * softplus
    beta_all = _sigmoid(ab)

    ti = lax.broadcasted_iota(jnp.int32, (1, C, C), 1)
    sj = lax.broadcasted_iota(jnp.int32, (1, C, C), 2)
    causal = sj <= ti
    strict = sj < ti
    eye = ti == sj
    tri = causal[0].astype(BF16)
    bc_all = jnp.concatenate([_cumsum_rows(tri, g_all[c * C:(c + 1) * C]) for c in range(NCH)], axis=0)

    q3 = heads(y, 0, GDN_DK)
    k3 = heads(y, W, GDN_DK)
    v3 = heads(y, 2 * W, GDN_DV)
    q3 = q3 * lax.rsqrt(jnp.sum(q3 * q3, axis=-1, keepdims=True) + NORM_EPS) * (GDN_DK ** -0.5)
    k3 = k3 * lax.rsqrt(jnp.sum(k3 * k3, axis=-1, keepdims=True) + NORM_EPS)
    bcol = heads(bc_all, 0, 1)
    beta = heads(beta_all, H, 1)
    brow = jnp.sum(jnp.where(eye, bcol, 0.0), axis=1, keepdims=True)
    dec = jnp.exp(jnp.where(causal, bcol - brow, NEG_BIG))
    kb = k3 * beta
    eb = jnp.exp(bcol)
    m_strict = jnp.where(strict, _bmm_nt(kb, k3) * dec, 0.0)
    tinv = _tri_inverse(m_strict, C)
    uw = _bmm(tinv, jnp.concatenate([v3 * beta, kb * eb], axis=2))
    attn = _bmm_nt(q3, k3) * dec
    blast = bcol[:, C - 1:C, :]
    kd = k3 * jnp.exp(blast - bcol)
    gl = jnp.exp(blast)
    wq = jnp.concatenate([uw[:, :, GDN_DV:], q3 * eb], axis=1)
    z3 = heads(z_ref[...], 0, GDN_DV)

    s = s_ref[...]
    for c in range(NCH):
        ws = _bmm(chunk(wq, c), s)
        v_new = chunk(uw, c)[:, :, :GDN_DV] - ws[:, :C]
        o = ws[:, C:] + _bmm(chunk(attn, c), v_new)
        s = s * chunk(gl, c) + _bmm(jnp.swapaxes(chunk(kd, c), 1, 2), v_new)
        o = _rms_rows(o, nw_ref[...]) * _silu(chunk(z3, c))
        for h in range(H):
            o_ref[c * C:(c + 1) * C, h * GDN_DV:(h + 1) * GDN_DV] = o[h].astype(o_ref.dtype)
    s_ref[...] = s

    @pl.when(ci == pl.num_programs(1) - 1)
    def _():
        sout_ref[...] = s


def _gdn(proj, ab, conv_w_e, alog_pad, dtb_pad, norm_e, B, L, C, NCH=1, prev8=None, s0=None, out_dtype=F32):
    has_state = prev8 is not None
    R = C * NCH
    nc = L // R
    W = GDN_HEADS * GDN_DK
    base = (2 * DA_W) // W
    row = lambda b, c: (b * nc + c, 0)
    in_specs = [
        pl.BlockSpec((R, W), lambda b, c: (b * nc + c, base)),
        pl.BlockSpec((R, W), lambda b, c: (b * nc + c, base + 1)),
        pl.BlockSpec((R, W), lambda b, c: (b * nc + c, base + 2)),
        pl.BlockSpec((R, W), lambda b, c: (b * nc + c, base + 3)),
        pl.BlockSpec((R, LANES), row),
        pl.BlockSpec((GDN_CONV, GDN_QKV), lambda b, c: (0, 0)),
        pl.BlockSpec((1, LANES), lambda b, c: (0, 0)),
        pl.BlockSpec((1, LANES), lambda b, c: (0, 0)),
        pl.BlockSpec((1, GDN_DV), lambda b, c: (0, 0)),
    ]
    args = [proj, proj, proj, proj, ab, conv_w_e, alog_pad, dtb_pad, norm_e]
    state_block = (None, GDN_HEADS, GDN_DK, GDN_DV)
    if has_state:
        in_specs += [pl.BlockSpec((None, SUBLANES, GDN_QKV), lambda b, c: (b, 0, 0)),
                     pl.BlockSpec(state_block, lambda b, c: (b, 0, 0, 0))]
        args += [prev8, s0]
    return pl.pallas_call(
        functools.partial(_gdn_kernel, C=C, NCH=NCH, has_state=has_state),
        grid=(B, nc),
        in_specs=in_specs,
        out_specs=[pl.BlockSpec((R, W), row),
                   pl.BlockSpec(state_block, lambda b, c: (b, 0, 0, 0))],
        out_shape=[jax.ShapeDtypeStruct((B * L, W), out_dtype),
                   jax.ShapeDtypeStruct((B, GDN_HEADS, GDN_DK, GDN_DV), F32)],
        scratch_shapes=[pltpu.VMEM((SUBLANES + R, GDN_QKV), F32),
                        pltpu.VMEM((GDN_HEADS, GDN_DK, GDN_DV), F32)],
        compiler_params=_cparams("parallel", "arbitrary"),
        name="gdn_state" if has_state else "gdn_fresh",
    )(*args)


def _sub_block_rows(x4, b):
    sub = lax.broadcasted_iota(jnp.int32, (1, 1, SUBLANES, 1), 2)
    out = None
    for start in range(0, SUBLANES, b):
        mid = start + b // 2
        piece = jnp.broadcast_to(x4[:, :, mid:mid + 1, :], x4.shape)
        out = piece if out is None else jnp.where(sub >= start, piece, out)
    return out


def _hgrn_kernel(*refs, C, HB, SB, layer, has_state):
    if has_state:
        (q_ref, f_ref, i_ref, g_ref, lower_ref, nw_ref, s0_ref, o_ref, sout_ref, st_ref) = refs
    else:
        (q_ref, f_ref, i_ref, g_ref, lower_ref, nw_ref, o_ref, sout_ref, st_ref) = refs
    ci = pl.program_id(2)
    G = HB * SB
    K = HG_EXPAND

    @pl.when(ci == 0)
    def _():
        for h in range(HB):
            for sb in range(SB):
                st_ref[h * SB + sb] = s0_ref[sb, h].T if has_state else jnp.zeros((HG_DI, K), F32)

    def heads(x):
        return jnp.stack([x[:, h * K:(h + 1) * K] for h in range(HB)]).reshape(G, C, K)

    low = lower_ref[...]
    ex = jnp.exp(low - jnp.max(low, axis=0, keepdims=True))
    lb = jnp.sum(ex[1:layer + 1], axis=0, keepdims=True) / jnp.sum(ex, axis=0, keepdims=True)

    fg2 = lb + (1.0 - lb) * _sigmoid(f_ref[...])
    logf2 = jnp.log(fg2)
    ti = lax.broadcasted_iota(jnp.int32, (1, C, C), 1)
    sj = lax.broadcasted_iota(jnp.int32, (1, C, C), 2)
    tri = (sj[0] <= ti[0]).astype(BF16)
    bc2 = jnp.concatenate([_cumsum_rows(tri, logf2[sb * C:(sb + 1) * C]) for sb in range(SB)], axis=0)

    qt = heads(_silu(q_ref[...]) * (K ** -0.5))
    kk = heads(1.0 - fg2)
    v = heads(i_ref[...])
    bc = heads(bc2)
    st = st_ref[...]
    o = _bmm_nt(qt * jnp.exp(bc), st)

    ri = lax.broadcasted_iota(jnp.int32, (1, C, 1), 1)
    a = jnp.where(ti == sj, jnp.sum(qt * kk, axis=-1, keepdims=True), 0.0)
    bc4 = bc.reshape(G, C // SUBLANES, SUBLANES, K)
    b = 2
    while b <= C:
        if b <= SUBLANES:
            ref = _sub_block_rows(bc4, b).reshape(G, C, K)
        else:
            ref = jnp.concatenate(
                [jnp.broadcast_to(bc[:, m:m + 1, :], (G, b, K)) for m in range(b // 2, C, b)], axis=1)
        upper = (ri % b) >= (b // 2)
        d = bc - ref
        e = jnp.exp(jnp.where(upper, d, -d))
        qs = jnp.where(upper, qt * e, 0.0)
        ks = jnp.where(upper, 0.0, kk * e)
        a = a + jnp.where(ti // b == sj // b, _bmm_nt(qs, ks), 0.0)
        b *= 2
    o = o + _bmm(a, v)

    blast = bc[:, C - 1:C, :]
    st = st * jnp.exp(blast) + _bmm(jnp.swapaxes(v, 1, 2), kk * jnp.exp(blast - bc))
    st_ref[...] = st
    nw = jnp.stack([nw_ref[:, h * K:(h + 1) * K] for h in range(HB) for _ in range(SB)])
    o = _rms_rows(o, nw) * heads(_silu(g_ref[...]))
    for h in range(HB):
        for sb in range(SB):
            o_ref[sb * C:(sb + 1) * C, h * K:(h + 1) * K] = o[h * SB + sb].astype(o_ref.dtype)

    @pl.when(ci == pl.num_programs(2) - 1)
    def _():
        for h in range(HB):
            for sb in range(SB):
                sout_ref[sb, h] = st[h * SB + sb].T


def _hgrn(proj, hg_lower, hg_norm_o, layer, B, L, C, s0=None, HB=8, SB=1):
    has_state = s0 is not None
    nc = L // C
    assert SB == 1 or nc == 1
    nhg = HG_HEADS // HB
    Wb = HB * HG_EXPAND
    R = SB * C

    def col(section):
        return lambda b, hg, c: (b * nc + c, section * nhg + hg)

    in_specs = [pl.BlockSpec((R, Wb), col(s)) for s in range(4)] + [
        pl.BlockSpec((DEPTH, Wb), lambda b, hg, c: (0, hg)),
        pl.BlockSpec((1, Wb), lambda b, hg, c: (0, hg)),
    ]
    args = [proj, proj, proj, proj, hg_lower, hg_norm_o]
    state_block = (SB, HB, HG_EXPAND, HG_DI)
    if has_state:
        in_specs.append(pl.BlockSpec(state_block, lambda b, hg, c: (b, hg, 0, 0)))
        args.append(s0)
    return pl.pallas_call(
        functools.partial(_hgrn_kernel, C=C, HB=HB, SB=SB, layer=layer, has_state=has_state),
        grid=(B // SB, nhg, nc),
        in_specs=in_specs,
        out_specs=[pl.BlockSpec((R, Wb), col(0)),
                   pl.BlockSpec(state_block, lambda b, hg, c: (b, hg, 0, 0))],
        out_shape=[jax.ShapeDtypeStruct((B * L, D_MODEL), BF16),
                   jax.ShapeDtypeStruct((B, HG_HEADS, HG_EXPAND, HG_DI), F32)],
        scratch_shapes=[pltpu.VMEM((HB * SB, HG_DI, HG_EXPAND), F32)],
        compiler_params=_cparams("parallel", "parallel", "arbitrary"),
        name="hgrn_state" if has_state else "hgrn_fresh",
    )(*args)


def _pad_lanes(v, offset=0):
    return jnp.zeros((1, LANES), F32).at[0, offset:offset + v.shape[0]].set(v.astype(F32))


def kernel(x_prompt, x_sample, cache_k, cache_v, state_gdn_conv, state_gdn, state_hgrn, page_table, norm_mix, norm_ffn, norm_final, w_in_even, w_out_even, da_lambda, da_subln, gdn_conv_w, gdn_a_log, gdn_dt_bias, gdn_norm, w_in_odd, w_out_odd, hg_lower, hg_norm, w_gate_up, w_down):
    Bp, Lp = x_prompt.shape[:2]
    DB, Ls = x_sample.shape[:2]
    past_len = page_table.shape[1] * PAGE_SIZE
    pos_p = np.tile(np.arange(Lp), Bp)
    pos_s = np.tile(past_len + np.arange(Ls), DB)
    Mp, Ms = Bp * Lp, DB * Ls
    tm_p, tm_s = 1024, Ms

    hp = x_prompt.reshape(Mp, D_MODEL)
    hs = x_sample.reshape(Ms, D_MODEL)
    outs = {k: [] for k in ("k_p", "v_p", "conv_p", "gdn_p", "hg_p", "k_s", "v_s", "conv_s", "gdn_s", "hg_s")}

    tn_s = 1024
    for l in range(DEPTH):
        nw_mix = norm_mix[l].reshape(1, D_MODEL)
        nw_ffn = norm_ffn[l].reshape(1, D_MODEL)
        if l % 2 == 0:
            e = l // 2
            lam_init = 0.8 - 0.6 * math.exp(-0.3 * l)
            w_in_t = jnp.transpose(w_in_even, (0, 2, 1))
            wab = jnp.pad(w_in_t[e, EVEN_MAIN:], ((0, LANES - 2 * GDN_HEADS), (0, 0)))
            alog_pad = _pad_lanes(gdn_a_log[e])
            dtb_pad = _pad_lanes(gdn_dt_bias[e])
            sub = da_subln[e].reshape(1, DA_V_DIM)
            gnorm = gdn_norm[e].reshape(1, GDN_DV)
            conv_cols = slice(2 * DA_W, 2 * DA_W + GDN_QKV)
            proj, kt, ab = _inproj_even(hp, nw_mix, w_in_t, e, wab, pos_p, Bp, tm_p)
            oa = _attn_prompt(proj, kt, da_lambda[e], sub, Bp, Lp, lam_init)
            gdn_c = min(GDN_CHUNK, Lp)
            ob, st = _gdn(proj, ab, gdn_conv_w[e], alog_pad, dtb_pad, gnorm, Bp, Lp, gdn_c,
                          NCH=2 if Lp % (2 * gdn_c) == 0 else 1, out_dtype=BF16)
            hp = _mm_res([oa, ob], w_out_even, e, hp, tm_p)
            outs["k_p"].append(kt.reshape(Bp, DA_HEADS, 2, DA_QK_DIM, Lp).transpose(0, 4, 1, 2, 3))
            outs["v_p"].append(proj[:, DA_W:2 * DA_W].reshape(Bp, Lp, DA_HEADS, DA_V_DIM))
            outs["conv_p"].append(proj.reshape(Bp, Lp, EVEN_ROWS)[:, Lp - (GDN_CONV - 1):, conv_cols])
            outs["gdn_p"].append(st)
            proj, kt, ab = _inproj_even(hs, nw_mix, w_in_t, e, wab, pos_s, 1, tm_s, tn=tn_s)
            oa = _attn_sample(proj, kt[0], cache_k, cache_v, e, page_table, da_lambda[e], sub, DB, Ls, lam_init)
            prev8 = jnp.pad(state_gdn_conv[e], ((0, 0), (SUBLANES - (GDN_CONV - 1), 0), (0, 0)))
            ob, st = _gdn(proj, ab, gdn_conv_w[e], alog_pad, dtb_pad, gnorm, DB, Ls, min(GDN_CHUNK, Ls),
                          prev8=prev8, s0=state_gdn[e])
            hs = _mm_res([oa, ob], w_out_even, e, hs, tm_s, tn=tn_s)
            gq = proj[:, conv_cols].reshape(DB, Ls, GDN_QKV)
            conv_s = jnp.concatenate([state_gdn_conv[e], gq], axis=1)[:, Ls:]
            outs["k_s"].append(kt[0].reshape(DA_HEADS, 2, DA_QK_DIM, DB, Ls).transpose(3, 4, 0, 1, 2))
            outs["v_s"].append(proj[:, DA_W:2 * DA_W].reshape(DB, Ls, DA_HEADS, DA_V_DIM))
            outs["conv_s"].append(conv_s)
            outs["gdn_s"].append(st)
        else:
            o = l // 2
            hnorm = hg_norm[o].reshape(1, D_MODEL)
            proj = _norm_matmul(hp, nw_mix, w_in_odd, o, tm_p)
            y, st = _hgrn(proj, hg_lower, hnorm, l, Bp, Lp, min(HG_CHUNK, Lp))
            hp = _mm_res([y], w_out_odd, o, hp, tm_p)
            outs["hg_p"].append(st)
            proj = _norm_matmul(hs, nw_mix, w_in_odd, o, tm_s, tn=tn_s)
            y, st = _hgrn(proj, hg_lower, hnorm, l, DB, Ls, min(HG_CHUNK, Ls), s0=state_hgrn[o],
                          SB=4 if Ls <= HG_CHUNK and DB % 4 == 0 else 1)
            hs = _mm_res([y], w_out_odd, o, hs, tm_s, tn=tn_s)
            outs["hg_s"].append(st)
        act = _ffn_up(hp, nw_ffn, w_gate_up, l, tm_p)
        hp = _mm_res([act], w_down, l, hp, tm_p, tk=D_FF // 2)
        act = _ffn_up(hs, nw_ffn, w_gate_up, l, tm_s)
        hs = _mm_res([act], w_down, l, hs, tm_s, tn=tn_s, tk=D_FF // 2)

    nwf = norm_final.reshape(1, D_MODEL)
    y_prompt = _final_norm(hp, nwf, tm_p).reshape(Bp, Lp, D_MODEL)
    y_sample = _final_norm(hs, nwf, tm_s).reshape(DB, Ls, D_MODEL)
    st = lambda k: jnp.stack(outs[k])
    return (y_prompt, y_sample, st("k_p"), st("v_p"), st("conv_p"), st("gdn_p"), st("hg_p"),
            st("k_s"), st("v_s"), st("conv_s"), st("gdn_s"), st("hg_s"))
```

```python
import functools
import math

import numpy as np
import jax
import jax.numpy as jnp
from jax import lax
from jax.experimental import pallas as pl
from jax.experimental.pallas import tpu as pltpu

F32 = jnp.float32
BF16 = jnp.bfloat16

D_MODEL = 2048
DEPTH = 2
PAGE_SIZE = 128
NORM_EPS = 1e-6

DA_HEADS = 8
DA_QK_DIM = 64
DA_V_DIM = 2 * DA_QK_DIM
DA_SCALE = DA_QK_DIM ** -0.5
ROT_DIM = DA_QK_DIM // 4
ROPE_THETA = 500000.0

GDN_HEADS = 8
GDN_DK = 128
GDN_DV = 128
GDN_QKV = GDN_HEADS * (2 * GDN_DK + GDN_DV)
GDN_CONV = 4
GDN_CHUNK = 64

HG_EXPAND = 128
HG_HEADS = D_MODEL // HG_EXPAND
HG_DI = D_MODEL // HG_HEADS
HG_CHUNK = 64

D_FF = -(-(8 * D_MODEL) // (3 * 256)) * 256

DA_W = DA_HEADS * 2 * DA_QK_DIM
EVEN_MAIN = 3 * DA_W + GDN_QKV + GDN_HEADS * GDN_DV
EVEN_ROWS = EVEN_MAIN - DA_W

LANES = 128
SUBLANES = 8
VMEM_LIMIT = 56 * 1024 * 1024
NEG_BIG = -1e30


def _cparams(*sem):
    return pltpu.CompilerParams(dimension_semantics=sem, vmem_limit_bytes=VMEM_LIMIT)


def _split2(x):
    hi = x.astype(BF16)
    lo = (x - hi.astype(F32)).astype(BF16)
    return hi, lo


def _cumsum_rows(tri, x):
    hi = x.astype(BF16)
    r = x - hi.astype(F32)
    mid = r.astype(BF16)
    lo = (r - mid.astype(F32)).astype(BF16)
    d = functools.partial(jnp.dot, preferred_element_type=F32)
    return d(tri, hi) + (d(tri, mid) + d(tri, lo))


def _sigmoid(x):
    return 0.5 * jnp.tanh(0.5 * x) + 0.5


def _silu(x):
    return x * _sigmoid(x)


def _rms_rows(x, w):
    return x * lax.rsqrt(jnp.mean(x * x, axis=-1, keepdims=True) + NORM_EPS) * w


W_SLABS = 4


def _slab_specs(k_rows, tn, index, transposed=False):
    def spec(s):
        def index_map(*g):
            layer, kt, col = index(*g)
            return (layer, col * W_SLABS + s, kt) if transposed else (layer, kt * W_SLABS + s, col)
        block = (None, tn // W_SLABS, k_rows) if transposed else (None, k_rows // W_SLABS, tn)
        return pl.BlockSpec(block, index_map)
    return [spec(s) for s in range(W_SLABS)]


def _dot_nt(a, b):
    return lax.dot_general(a, b, (((1,), (1,)), ((), ())), preferred_element_type=F32)


def _load_weight(w_refs, wbuf_ref):
    rows = wbuf_ref.shape[0] // len(w_refs)
    for s, r in enumerate(w_refs):
        wbuf_ref[s * rows:(s + 1) * rows, :] = r[...].astype(BF16)
    return wbuf_ref[...]


def _small_col(i, j, n_i):
    return jnp.where(i == n_i - 1, j, 0)


def _inproj_even_kernel(xp_ref, xs_ref, nw_ref, *rest, n_i, n_q_tiles, n_k_tiles, tn):
    w_refs = rest[:W_SLABS]
    (wab_ref, lane_p0, lane_p1, sub_p0, sub_p1, lane_s0, lane_s1, sub_s0, sub_s1,
     op_ref, ktp_ref, abp_ref, os_ref, kts_ref, abs_ref,
     xnp_ref, xns_ref, wbuf_ref) = rest[W_SLABS:]
    i = pl.program_id(0)
    j = pl.program_id(1)
    last = i == n_i - 1
    half = ROT_DIM // 2

    def prologue(x_ref, xn_ref, ab_ref):
        xn = _rms_rows(x_ref[...], nw_ref[...]).astype(BF16)
        xn_ref[...] = xn
        ab_ref[...] = _dot_nt(xn, wab_ref[...].astype(BF16))

    @pl.when(j == 0)
    def _():
        prologue(xp_ref, xnp_ref, abp_ref)

    @pl.when(jnp.logical_and(last, j == 0))
    def _():
        prologue(xs_ref, xns_ref, abs_ref)

    w = _load_weight(w_refs, wbuf_ref)

    def rotate(a, cosf, sinf, axis):
        ch = lax.broadcasted_iota(jnp.int32, a.shape, axis) % DA_QK_DIM
        partner = jnp.where(ch < half, -pltpu.roll(a, LANES - half, axis), pltpu.roll(a, half, axis))
        return a * cosf + partner * sinf

    def q_tile(xn_ref, o_ref, tabs):
        acc = _dot_nt(xn_ref[...], w)
        cosf, sinf = (t[...] for t in tabs)
        for c in range(tn // LANES):
            o_ref[:, c * LANES:(c + 1) * LANES] = rotate(acc[:, c * LANES:(c + 1) * LANES], cosf, sinf, 1)

    def k_tile(xn_ref, kt_ref, tabs):
        acc = _dot_nt(w, xn_ref[...])
        cosf, sinf = (t[...] for t in tabs)
        for c in range(tn // LANES):
            kt_ref[c * LANES:(c + 1) * LANES, :] = rotate(acc[c * LANES:(c + 1) * LANES, :], cosf, sinf, 0)

    def plain_tile(xn_ref, o_ref):
        o_ref[...] = _dot_nt(xn_ref[...], w)

    is_q = j < n_q_tiles
    is_k = jnp.logical_and(j >= n_q_tiles, j < n_q_tiles + n_k_tiles)
    is_plain = j >= n_q_tiles + n_k_tiles

    @pl.when(is_q)
    def _():
        q_tile(xnp_ref, op_ref, (lane_p0, lane_p1))

    @pl.when(is_k)
    def _():
        k_tile(xnp_ref, ktp_ref, (sub_p0, sub_p1))

    @pl.when(is_plain)
    def _():
        plain_tile(xnp_ref, op_ref)

    @pl.when(jnp.logical_and(last, is_q))
    def _():
        q_tile(xns_ref, os_ref, (lane_s0, lane_s1))

    @pl.when(jnp.logical_and(last, is_k))
    def _():
        k_tile(xns_ref, kts_ref, (sub_s0, sub_s1))

    @pl.when(jnp.logical_and(last, is_plain))
    def _():
        plain_tile(xns_ref, os_ref)


def _rope_tables(pos):
    half = ROT_DIM // 2
    inv = np.power(np.float32(ROPE_THETA), -np.arange(half, dtype=np.float32) * np.float32(2.0) / np.float32(ROT_DIM))
    ang = (pos.astype(np.float32)[:, None] * inv[None, :]).astype(np.float32)
    cos = np.cos(ang.astype(np.float64)).astype(np.float32)
    sin = np.sin(ang.astype(np.float64)).astype(np.float32)
    n = pos.shape[0]
    cosf = np.ones((n, LANES), np.float32)
    sinf = np.zeros((n, LANES), np.float32)
    for base in range(0, LANES, DA_QK_DIM):
        for off in (0, half):
            cosf[:, base + off:base + off + half] = cos
            sinf[:, base + off:base + off + half] = sin
    return cosf, sinf


def _inproj_even(xp, xs, nw, w_in_t, e, wab, pos_p, pos_s, B, tm, tn=512):
    M = xp.shape[0]
    Ms = xs.shape[0]
    L = M // B
    lt = L // tm
    n_i = M // tm
    tabs_p = _rope_tables(pos_p)
    tabs_s = _rope_tables(pos_s)
    lane = lambda tabs: [jnp.asarray(t) for t in tabs]
    sub = lambda tabs: [jnp.asarray(np.ascontiguousarray(t.T)) for t in tabs]
    n_q, n_k = DA_W // tn, DA_W // tn
    n_tiles = EVEN_MAIN // tn
    kern = functools.partial(_inproj_even_kernel, n_i=n_i, n_q_tiles=n_q, n_k_tiles=n_k, tn=tn)
    row = lambda i, j: (i, 0)
    col = lambda i, j: (0, i)
    fixed = lambda i, j: (0, 0)
    once = pl.Buffered(1)
    row_col = lambda j: jnp.where(j < n_q, j, jnp.maximum(j - n_k, n_q - 1))
    kt_col = lambda j: jnp.clip(j - n_q, 0, n_k - 1)
    return pl.pallas_call(
        kern,
        grid=(n_i, n_tiles),
        in_specs=[
            pl.BlockSpec((tm, D_MODEL), row),
            pl.BlockSpec((Ms, D_MODEL), fixed, pipeline_mode=once),
            pl.BlockSpec((1, D_MODEL), fixed, pipeline_mode=once),
        ] + _slab_specs(D_MODEL, tn, lambda i, j: (e, 0, j), transposed=True) + [
            pl.BlockSpec((LANES, D_MODEL), fixed, pipeline_mode=once),
        ] + [pl.BlockSpec((tm, LANES), row, pipeline_mode=once)] * 2
          + [pl.BlockSpec((LANES, tm), col, pipeline_mode=once)] * 2
          + [pl.BlockSpec((Ms, LANES), fixed, pipeline_mode=once)] * 2
          + [pl.BlockSpec((LANES, Ms), fixed, pipeline_mode=once)] * 2,
        out_specs=[
            pl.BlockSpec((tm, tn), lambda i, j: (i, row_col(j))),
            pl.BlockSpec((None, tn, tm), lambda i, j: (i // lt, kt_col(j), i % lt)),
            pl.BlockSpec((tm, LANES), row),
            pl.BlockSpec((Ms, tn), lambda i, j: (0, _small_col(i, row_col(j), n_i))),
            pl.BlockSpec((None, tn, Ms), lambda i, j: (0, _small_col(i, kt_col(j), n_i), 0)),
            pl.BlockSpec((Ms, LANES), fixed),
        ],
        out_shape=[jax.ShapeDtypeStruct((M, EVEN_ROWS), F32),
                   jax.ShapeDtypeStruct((B, DA_W, L), F32),
                   jax.ShapeDtypeStruct((M, LANES), F32),
                   jax.ShapeDtypeStruct((Ms, EVEN_ROWS), F32),
                   jax.ShapeDtypeStruct((1, DA_W, Ms), F32),
                   jax.ShapeDtypeStruct((Ms, LANES), F32)],
        scratch_shapes=[pltpu.VMEM((tm, D_MODEL), BF16), pltpu.VMEM((Ms, D_MODEL), BF16),
                        pltpu.VMEM((tn, D_MODEL), BF16)],
        compiler_params=_cparams("arbitrary", "arbitrary"),
        name="inproj_even",
    )(xp, xs, nw, *([w_in_t] * W_SLABS), wab, *lane(tabs_p), *sub(tabs_p), *lane(tabs_s), *sub(tabs_s))


def _norm_matmul_kernel(xp_ref, xs_ref, nw_ref, *rest, n_i):
    w_refs = rest[:W_SLABS]
    op_ref, os_ref, xnp_ref, xns_ref, wbuf_ref = rest[W_SLABS:]
    i = pl.program_id(0)
    j = pl.program_id(1)
    last = i == n_i - 1

    @pl.when(j == 0)
    def _():
        xnp_ref[...] = _rms_rows(xp_ref[...], nw_ref[...]).astype(BF16)

    @pl.when(jnp.logical_and(last, j == 0))
    def _():
        xns_ref[...] = _rms_rows(xs_ref[...], nw_ref[...]).astype(BF16)

    w = _load_weight(w_refs, wbuf_ref)
    op_ref[...] = jnp.dot(xnp_ref[...], w, preferred_element_type=F32)

    @pl.when(last)
    def _():
        os_ref[...] = jnp.dot(xns_ref[...], w, preferred_element_type=F32)


def _norm_matmul(xp, xs, nw, w_all, l, tm, tn=512):
    M = xp.shape[0]
    Ms = xs.shape[0]
    N = w_all.shape[-1]
    n_i = M // tm
    return pl.pallas_call(
        functools.partial(_norm_matmul_kernel, n_i=n_i),
        grid=(n_i, N // tn),
        in_specs=[
            pl.BlockSpec((tm, D_MODEL), lambda i, j: (i, 0)),
            pl.BlockSpec((Ms, D_MODEL), lambda i, j: (0, 0)),
            pl.BlockSpec((1, D_MODEL), lambda i, j: (0, 0)),
        ] + _slab_specs(D_MODEL, tn, lambda i, j: (l, 0, j)),
        out_specs=[pl.BlockSpec((tm, tn), lambda i, j: (i, j)),
                   pl.BlockSpec((Ms, tn), lambda i, j: (0, _small_col(i, j, n_i)))],
        out_shape=[jax.ShapeDtypeStruct((M, N), F32), jax.ShapeDtypeStruct((Ms, N), F32)],
        scratch_shapes=[pltpu.VMEM((tm, D_MODEL), BF16), pltpu.VMEM((Ms, D_MODEL), BF16),
                        pltpu.VMEM((D_MODEL, tn), BF16)],
        compiler_params=_cparams("arbitrary", "arbitrary"),
        name="norm_matmul",
    )(xp, xs, nw, *([w_all] * W_SLABS))


def _ffn_up_kernel(xp_ref, xs_ref, nw_ref, *rest, n_i):
    wg_refs = rest[:W_SLABS]
    wu_refs = rest[W_SLABS:2 * W_SLABS]
    op_ref, os_ref, xnp_ref, xns_ref, wgbuf_ref, wubuf_ref = rest[2 * W_SLABS:]
    i = pl.program_id(0)
    j = pl.program_id(1)
    last = i == n_i - 1

    @pl.when(j == 0)
    def _():
        xnp_ref[...] = _rms_rows(xp_ref[...], nw_ref[...]).astype(BF16)

    @pl.when(jnp.logical_and(last, j == 0))
    def _():
        xns_ref[...] = _rms_rows(xs_ref[...], nw_ref[...]).astype(BF16)

    wg = _load_weight(wg_refs, wgbuf_ref)
    wu = _load_weight(wu_refs, wubuf_ref)

    def tile(xn_ref, o_ref):
        xn = xn_ref[...]
        g = jnp.dot(xn, wg, preferred_element_type=F32)
        u = jnp.dot(xn, wu, preferred_element_type=F32)
        o_ref[...] = (_silu(g) * u).astype(BF16)

    tile(xnp_ref, op_ref)

    @pl.when(last)
    def _():
        tile(xns_ref, os_ref)


def _ffn_up(xp, xs, nw, w_gate_up, l, tm, tn=512):
    M = xp.shape[0]
    Ms = xs.shape[0]
    nj = D_FF // tn
    n_i = M // tm
    return pl.pallas_call(
        functools.partial(_ffn_up_kernel, n_i=n_i),
        grid=(n_i, nj),
        in_specs=[
            pl.BlockSpec((tm, D_MODEL), lambda i, j: (i, 0)),
            pl.BlockSpec((Ms, D_MODEL), lambda i, j: (0, 0), pipeline_mode=pl.Buffered(1)),
            pl.BlockSpec((1, D_MODEL), lambda i, j: (0, 0), pipeline_mode=pl.Buffered(1)),
        ] + _slab_specs(D_MODEL, tn, lambda i, j: (l, 0, j))
          + _slab_specs(D_MODEL, tn, lambda i, j: (l, 0, j + nj)),
        out_specs=[pl.BlockSpec((tm, tn), lambda i, j: (i, j)),
                   pl.BlockSpec((Ms, tn), lambda i, j: (0, _small_col(i, j, n_i)))],
        out_shape=[jax.ShapeDtypeStruct((M, D_FF), BF16), jax.ShapeDtypeStruct((Ms, D_FF), BF16)],
        scratch_shapes=[pltpu.VMEM((tm, D_MODEL), BF16), pltpu.VMEM((Ms, D_MODEL), BF16),
                        pltpu.VMEM((D_MODEL, tn), BF16), pltpu.VMEM((D_MODEL, tn), BF16)],
        compiler_params=_cparams("arbitrary", "arbitrary"),
        name="ffn_up",
    )(xp, xs, nw, *([w_gate_up] * (2 * W_SLABS)))


def _mm_res_kernel(*refs, nk, na, n_i):
    ap_refs = refs[:na]
    as_refs = refs[na:2 * na]
    w_refs = refs[2 * na:2 * na + W_SLABS]
    (rp_ref, rs_ref, op_ref, os_ref,
     accp_ref, accs_ref, wbuf_ref, abufp_ref, abufs_ref) = refs[2 * na + W_SLABS:]
    i = pl.program_id(0)
    k = pl.program_id(2)
    last = i == n_i - 1
    w = _load_weight(w_refs, wbuf_ref)

    def tile(a_refs, abuf_ref, r_ref, o_ref, acc_ref):
        if na == 1:
            a = a_refs[0][...].astype(BF16)
        else:
            off = 0
            for r in a_refs:
                abuf_ref[:, off:off + r.shape[1]] = r[...].astype(BF16)
                off += r.shape[1]
            a = abuf_ref[...]
        p = jnp.dot(a, w, preferred_element_type=F32)
        if nk == 1:
            o_ref[...] = r_ref[...] + p
        else:
            @pl.when(k == 0)
            def _():
                acc_ref[...] = p

            @pl.when(jnp.logical_and(k > 0, k < nk - 1))
            def _():
                acc_ref[...] += p

            @pl.when(k == nk - 1)
            def _():
                o_ref[...] = r_ref[...] + (acc_ref[...] + p)

    tile(ap_refs, abufp_ref, rp_ref, op_ref, accp_ref)

    @pl.when(last)
    def _():
        tile(as_refs, abufs_ref, rs_ref, os_ref, accs_ref)


def _mm_res(ap_list, as_list, w_all, l, res_p, res_s, tm, tn=512, tk=None):
    M = res_p.shape[0]
    Ms = res_s.shape[0]
    K = sum(a.shape[1] for a in ap_list)
    N = w_all.shape[-1]
    tk = K if tk is None else tk
    nk = K // tk
    na = len(ap_list)
    n_i = M // tm
    assert na == 1 or nk == 1
    small = lambda i, j, k: (0, _small_col(i, j, n_i))
    if na == 1:
        a_specs = [pl.BlockSpec((tm, tk), lambda i, j, k: (i, k)),
                   pl.BlockSpec((Ms, tk), lambda i, j, k: (0, jnp.where(i == n_i - 1, k, 0)))]
    else:
        a_specs = ([pl.BlockSpec((tm, a.shape[1]), lambda i, j, k: (i, 0)) for a in ap_list]
                   + [pl.BlockSpec((Ms, a.shape[1]), lambda i, j, k: (0, 0)) for a in as_list])
    abuf = lambda rows: pltpu.VMEM((rows, tk) if na > 1 else (SUBLANES * 2, LANES), BF16)
    return pl.pallas_call(
        functools.partial(_mm_res_kernel, nk=nk, na=na, n_i=n_i),
        grid=(n_i, N // tn, nk),
        in_specs=a_specs + _slab_specs(tk, tn, lambda i, j, k: (l, k, j))
                 + [pl.BlockSpec((tm, tn), lambda i, j, k: (i, j)), pl.BlockSpec((Ms, tn), small)],
        out_specs=[pl.BlockSpec((tm, tn), lambda i, j, k: (i, j)), pl.BlockSpec((Ms, tn), small)],
        out_shape=[jax.ShapeDtypeStruct((M, N), F32), jax.ShapeDtypeStruct((Ms, N), F32)],
        scratch_shapes=[pltpu.VMEM((tm, tn), F32), pltpu.VMEM((Ms, tn), F32), pltpu.VMEM((tk, tn), BF16),
                        abuf(tm), abuf(Ms)],
        compiler_params=_cparams("arbitrary", "arbitrary", "arbitrary"),
        name="matmul_residual",
    )(*ap_list, *as_list, *([w_all] * W_SLABS), res_p, res_s)


def _final_norm_kernel(x_ref, w_ref, o_ref):
    o_ref[...] = _rms_rows(x_ref[...], w_ref[...])


def _final_norm(x, w, tm):
    M = x.shape[0]
    return pl.pallas_call(
        _final_norm_kernel,
        grid=(M // tm,),
        in_specs=[pl.BlockSpec((tm, D_MODEL), lambda i: (i, 0)),
                  pl.BlockSpec((1, D_MODEL), lambda i: (0, 0))],
        out_specs=pl.BlockSpec((tm, D_MODEL), lambda i: (i, 0)),
        out_shape=jax.ShapeDtypeStruct((M, D_MODEL), F32),
        compiler_params=_cparams("parallel"),
        name="final_norm",
    )(x, w)


def _lambda_of(lam_ref, lam_init):
    lf = lam_ref[...]
    s1 = jnp.sum(lf[0:1] * lf[1:2], axis=-1, keepdims=True)
    s2 = jnp.sum(lf[2:3] * lf[3:4], axis=-1, keepdims=True)
    return jnp.exp(s1) - jnp.exp(s2) + lam_init


def _attn_prompt_kernel(qi_ref, ki_ref, lam_ref, q_ref, k_ref, v_ref, sub_ref, o_ref,
                        m_ref, l_ref, acc_ref, *, t, lam_init):
    step = pl.program_id(2)
    qi = qi_ref[step]
    ki = ki_ref[step]

    @pl.when(ki == 0)
    def _():
        m_ref[...] = jnp.full(m_ref.shape, NEG_BIG, F32)
        l_ref[...] = jnp.zeros(l_ref.shape, F32)
        acc_ref[...] = jnp.zeros(acc_ref.shape, F32)

    def update(diagonal):
        q = q_ref[...] * DA_SCALE
        kt = k_ref[...].astype(BF16)
        v = v_ref[...].astype(BF16)
        lane_map = lax.broadcasted_iota(jnp.int32, (1, LANES), 1) // DA_QK_DIM
        if diagonal:
            causal = (lax.broadcasted_iota(jnp.int32, (1, t), 1)
                      <= lax.broadcasted_iota(jnp.int32, (t, 1), 0))
        for m in range(2):
            qm = jnp.where(lane_map == m, q, 0.0).astype(BF16)
            s = jnp.dot(qm, kt, preferred_element_type=F32)
            if diagonal:
                s = jnp.where(causal, s, NEG_BIG)
            m_prev = m_ref[m]
            m_next = jnp.maximum(m_prev, jnp.max(s, axis=1, keepdims=True))
            alpha = jnp.exp(m_prev - m_next)
            p = jnp.exp(s - jnp.concatenate([m_next] * (t // LANES), axis=1))
            l_ref[m] = alpha * l_ref[m] + jnp.sum(p, axis=1, keepdims=True)
            acc_ref[m] = alpha * acc_ref[m] + jnp.dot(p.astype(BF16), v, preferred_element_type=F32)
            m_ref[m] = m_next

    @pl.when(ki < qi)
    def _():
        update(False)

    @pl.when(ki == qi)
    def _():
        update(True)
        lam = _lambda_of(lam_ref, lam_init)
        o = acc_ref[0] / l_ref[0] - lam * (acc_ref[1] / l_ref[1])
        o_ref[...] = (_rms_rows(o, sub_ref[...]) * (1.0 - lam_init)).astype(o_ref.dtype)


def _attn_prompt(rows, kt, da_lambda_e, subln_e, B, S, lam_init, t=512):
    nq = S // t
    pairs = [(qi, ki) for qi in range(nq) for ki in range(qi + 1)]
    qi_tab = jnp.asarray([p[0] for p in pairs], jnp.int32)
    ki_tab = jnp.asarray([p[1] for p in pairs], jnp.int32)
    kern = functools.partial(_attn_prompt_kernel, t=t, lam_init=lam_init)
    grid_spec = pltpu.PrefetchScalarGridSpec(
        num_scalar_prefetch=2,
        grid=(B, DA_HEADS, len(pairs)),
        in_specs=[
            pl.BlockSpec((4, DA_QK_DIM), lambda b, h, s, qt, kt: (0, 0)),
            pl.BlockSpec((t, LANES), lambda b, h, s, qt, kt: (b * nq + qt[s], h)),
            pl.BlockSpec((None, LANES, t), lambda b, h, s, qt, kt: (b, h, kt[s])),
            pl.BlockSpec((t, LANES), lambda b, h, s, qt, kt: (b * nq + kt[s], DA_HEADS + h)),
            pl.BlockSpec((1, DA_V_DIM), lambda b, h, s, qt, kt: (0, 0)),
        ],
        out_specs=pl.BlockSpec((t, LANES), lambda b, h, s, qt, kt: (b * nq + qt[s], h)),
        scratch_shapes=[pltpu.VMEM((2, t, LANES), F32), pltpu.VMEM((2, t, LANES), F32),
                        pltpu.VMEM((2, t, DA_V_DIM), F32)],
    )
    return pl.pallas_call(
        kern,
        grid_spec=grid_spec,
        out_shape=jax.ShapeDtypeStruct((B * S, DA_W), BF16),
        compiler_params=_cparams("parallel", "parallel", "arbitrary"),
        name="diff_attn_prompt",
    )(qi_tab, ki_tab, da_lambda_e, rows, kt, rows, subln_e)


def _attn_sample_kernel(pt_ref, lam_ref, qbd_ref, knt_ref, vn_ref, sub_ref, *rest,
                        G, n_steps, dec_seq, lam_init):
    k_refs = rest[:G]
    v_refs = rest[G:2 * G]
    o_ref = rest[2 * G]
    st_ref, mx_ref, psum_ref, oacc_ref, vpad_ref = rest[2 * G + 1:]
    t = pl.program_id(1)
    n_pages = n_steps * G

    @pl.when(t == 0)
    def _():
        mx_ref[...] = jnp.full(mx_ref.shape, NEG_BIG, F32)
        vpad_ref[...] = jnp.zeros(vpad_ref.shape, F32)
        vpad_ref[0:dec_seq, :] = vn_ref[...]

    def scores(kt_page, idx, mask=None):
        s = jnp.dot(qbd_ref[...], kt_page.astype(BF16), preferred_element_type=F32)
        if mask is not None:
            s = jnp.where(mask, s, NEG_BIG)
        st_ref[idx] = s
        mx_ref[...] = jnp.maximum(mx_ref[...], s)

    @pl.when(t < n_steps)
    def _():
        for g in range(G):
            scores(k_refs[g][...], t * G + g)

    @pl.when(t == n_steps - 1)
    def _():
        qry = lax.broadcasted_iota(jnp.int32, (LANES, PAGE_SIZE), 0) % dec_seq
        key = lax.broadcasted_iota(jnp.int32, (LANES, PAGE_SIZE), 1)
        scores(knt_ref[...], n_pages, mask=key <= qry)
        row_max = jnp.max(mx_ref[...], axis=1, keepdims=True)
        mx_ref[...] = jnp.broadcast_to(row_max, mx_ref.shape)
        psum_ref[...] = jnp.zeros(psum_ref.shape, F32)
        oacc_ref[...] = jnp.zeros(oacc_ref.shape, F32)

    def accumulate(v_flat, idx):
        p = jnp.exp(st_ref[idx] - mx_ref[...])
        psum_ref[...] += p
        oacc_ref[...] += jnp.dot(p.astype(BF16), v_flat.astype(BF16), preferred_element_type=F32)

    def flat_values(v_ref):
        return jnp.concatenate(
            [v_ref[pl.ds(h, PAGE_SIZE, stride=DA_HEADS), :] for h in range(DA_HEADS)], axis=1)

    @pl.when(t >= n_steps)
    def _():
        for g in range(G):
            accumulate(flat_values(v_refs[g]), (t - n_steps) * G + g)

    @pl.when(t == 2 * n_steps - 1)
    def _():
        accumulate(vpad_ref[...], n_pages)
        lam = _lambda_of(lam_ref, lam_init)
        denom = jnp.sum(psum_ref[...], axis=1, keepdims=True)
        rows_per_head = 2 * dec_seq
        for h in range(DA_HEADS):
            r0 = h * rows_per_head
            blk = oacc_ref[r0:r0 + rows_per_head, h * DA_V_DIM:(h + 1) * DA_V_DIM]
            n = blk / denom[r0:r0 + rows_per_head]
            o = n[0:dec_seq] - lam * n[dec_seq:rows_per_head]
            o_ref[:, h * DA_V_DIM:(h + 1) * DA_V_DIM] = _rms_rows(o, sub_ref[...]) * (1.0 - lam_init)


def _attn_sample(proj_s, kt_s, cache_k, cache_v, e, page_table, da_lambda_e, subln_e, DB, dec_seq, lam_init, G=16):
    n_pages = page_table.shape[1]
    n_steps = n_pages // G
    n_rows = DA_HEADS * 2 * dec_seq
    assert n_rows == LANES and n_pages % G == 0
    n_layers, n_phys = cache_k.shape[:2]
    q = proj_s[:, :DA_W].reshape(DB, dec_seq, DA_HEADS * 2, DA_QK_DIM).transpose(0, 2, 1, 3)
    eye = jnp.eye(DA_HEADS * 2, dtype=F32)
    qbd = (q[:, :, :, None, :] * eye[None, :, None, :, None]).reshape(DB, n_rows, DA_W)
    qbd = (qbd * DA_SCALE).astype(BF16)
    knt = kt_s.reshape(DA_W, DB, dec_seq).transpose(1, 0, 2)
    knt = jnp.pad(knt, ((0, 0), (0, 0), (0, PAGE_SIZE - dec_seq))).astype(BF16)
    ckt = jnp.transpose(cache_k, (0, 1, 3, 4, 5, 2)).reshape(n_layers, n_phys, DA_W, PAGE_SIZE)
    cv = cache_v.reshape(n_layers, n_phys, PAGE_SIZE * DA_HEADS, DA_V_DIM)
    pt = page_table.reshape(-1)

    def k_map(g):
        return lambda b, t, pt: (e, pt[b * n_pages + jnp.minimum(t, n_steps - 1) * G + g], 0, 0)

    def v_map(g):
        return lambda b, t, pt: (e, pt[b * n_pages + jnp.maximum(t - n_steps, 0) * G + g], 0, 0)

    kern = functools.partial(_attn_sample_kernel, G=G, n_steps=n_steps, dec_seq=dec_seq, lam_init=lam_init)
    grid_spec = pltpu.PrefetchScalarGridSpec(
        num_scalar_prefetch=1,
        grid=(DB, 2 * n_steps),
        in_specs=[
            pl.BlockSpec((4, DA_QK_DIM), lambda b, t, pt: (0, 0)),
            pl.BlockSpec((None, n_rows, DA_W), lambda b, t, pt: (b, 0, 0)),
            pl.BlockSpec((None, DA_W, PAGE_SIZE), lambda b, t, pt: (b, 0, 0)),
            pl.BlockSpec((dec_seq, DA_W), lambda b, t, pt: (b, 1)),
            pl.BlockSpec((1, DA_V_DIM), lambda b, t, pt: (0, 0)),
        ] + [pl.BlockSpec((None, None, DA_W, PAGE_SIZE), k_map(g)) for g in range(G)]
          + [pl.BlockSpec((None, None, PAGE_SIZE * DA_HEADS, DA_V_DIM), v_map(g)) for g in range(G)],
        out_specs=pl.BlockSpec((dec_seq, DA_W), lambda b, t, pt: (b, 0)),
        scratch_shapes=[
            pltpu.VMEM((n_pages + 1, n_rows, PAGE_SIZE), F32),
            pltpu.VMEM((n_rows, PAGE_SIZE), F32),
            pltpu.VMEM((n_rows, PAGE_SIZE), F32),
            pltpu.VMEM((n_rows, DA_W), F32),
            pltpu.VMEM((PAGE_SIZE, DA_W), F32),
        ],
    )
    return pl.pallas_call(
        kern,
        grid_spec=grid_spec,
        out_shape=jax.ShapeDtypeStruct((DB * dec_seq, DA_W), F32),
        compiler_params=_cparams("parallel", "arbitrary"),
        name="diff_attn_sample",
    )(pt, da_lambda_e, qbd, knt, proj_s, subln_e, *([ckt] * G), *([cv] * G))


def _bmm(a, b):
    return jnp.einsum("gmk,gkn->gmn", a.astype(BF16), b.astype(BF16), preferred_element_type=F32)


def _bmm_nt(a, b):
    return jnp.einsum("gmk,gnk->gmn", a.astype(BF16), b.astype(BF16), preferred_element_type=F32)


def _bmm3(a, b):
    ah, al = _split2(a)
    bh, bl = _split2(b)
    d = functools.partial(jnp.einsum, "gmk,gkn->gmn", preferred_element_type=F32)
    return d(ah, bh) + (d(ah, bl) + d(al, bh))


def _tri_inverse(m_strict, C):
    ti = lax.broadcasted_iota(jnp.int32, (1, C, C), 1)
    sj = lax.broadcasted_iota(jnp.int32, (1, C, C), 2)
    eye = (ti == sj).astype(F32)
    a = jnp.where(ti // SUBLANES == sj // SUBLANES, -m_strict, 0.0)
    x = eye + a
    p = _bmm3(a, a)
    x = x + _bmm3(x, p)
    p = _bmm3(p, p)
    x = x + _bmm3(x, p)
    b = SUBLANES
    while b < C:
        join = jnp.logical_and(ti // (2 * b) == sj // (2 * b), ti // b != sj // b)
        c = jnp.where(join, m_strict, 0.0)
        x = x - _bmm3(x, _bmm3(c, x))
        b *= 2
    return x


def _gdn_kernel(*refs, C, NCH, has_state):
    if has_state:
        (q_ref, k_ref, v_ref, z_ref, ab_ref, cw_ref, alog_ref, dtb_ref, nw_ref,
         prev_ref, s0_ref, o_ref, sout_ref, buf_ref, s_ref) = refs
    else:
        (q_ref, k_ref, v_ref, z_ref, ab_ref, cw_ref, alog_ref, dtb_ref, nw_ref,
         o_ref, sout_ref, buf_ref, s_ref) = refs
    ci = pl.program_id(1)
    W = GDN_HEADS * GDN_DK
    H = GDN_HEADS
    R = C * NCH
    G = H * NCH

    @pl.when(ci == 0)
    def _():
        if has_state:
            buf_ref[0:SUBLANES, :] = prev_ref[...]
            s_ref[...] = s0_ref[...]
        else:
            buf_ref[0:SUBLANES, :] = jnp.zeros((SUBLANES, GDN_QKV), F32)
            s_ref[...] = jnp.zeros(s_ref.shape, F32)

    buf_ref[SUBLANES:SUBLANES + R, 0:W] = q_ref[...]
    buf_ref[SUBLANES:SUBLANES + R, W:2 * W] = k_ref[...]
    buf_ref[SUBLANES:SUBLANES + R, 2 * W:3 * W] = v_ref[...]
    y = buf_ref[SUBLANES:SUBLANES + R, :] * cw_ref[GDN_CONV - 1:GDN_CONV, :]
    for back in range(1, GDN_CONV):
        y = y + buf_ref[pl.ds(SUBLANES - back, R), :] * cw_ref[GDN_CONV - 1 - back:GDN_CONV - back, :]
    tail = buf_ref[R:R + SUBLANES, :]
    buf_ref[0:SUBLANES, :] = tail
    y = _silu(y)

    def heads(x, off, width):
        return jnp.stack([x[:, off + h * width:off + (h + 1) * width] for h in range(H)]).reshape(G, C, width)

    def chunk(x, c):
        return x.reshape((H, NCH) + x.shape[1:])[:, c]

    ab = ab_ref[...]
    sp = ab + dtb_ref[...]
    softplus = jnp.maximum(sp, 0.0) + jnp.log(1.0 + jnp.exp(-jnp.abs(sp)))
    g_all = -jnp.exp(alog_ref[...]) * softplus
    beta_all = _sigmoid(ab)

    ti = lax.broadcasted_iota(jnp.int32, (1, C, C), 1)
    sj = lax.broadcasted_iota(jnp.int32, (1, C, C), 2)
    causal = sj <= ti
    strict = sj < ti
    eye = ti == sj
    tri = causal[0].astype(BF16)
    bc_all = jnp.concatenate([_cumsum_rows(tri, g_all[c * C:(c + 1) * C]) for c in range(NCH)], axis=0)

    q3 = heads(y, 0, GDN_DK)
    k3 = heads(y, W, GDN_DK)
    v3 = heads(y, 2 * W, GDN_DV)
    q3 = q3 * lax.rsqrt(jnp.sum(q3 * q3, axis=-1, keepdims=True) + NORM_EPS) * (GDN_DK ** -0.5)
    k3 = k3 * lax.rsqrt(jnp.sum(k3 * k3, axis=-1, keepdims=True) + NORM_EPS)
    bcol = heads(bc_all, 0, 1)
    beta = heads(beta_all, H, 1)
    brow = jnp.sum(jnp.where(eye, bcol, 0.0), axis=1, keepdims=True)
    dec = jnp.exp(jnp.where(causal, bcol - brow, NEG_BIG))
    kb = k3 * beta
    eb = jnp.exp(bcol)
    m_strict = jnp.where(strict, _bmm_nt(kb, k3) * dec, 0.0)
    tinv = _tri_inverse(m_strict, C)
    uw = _bmm(tinv, jnp.concatenate([v3 * beta, kb * eb], axis=2))
    attn = _bmm_nt(q3, k3) * dec
    blast = bcol[:, C - 1:C, :]
    kd = k3 * jnp.exp(blast - bcol)
    gl = jnp.exp(blast)
    wq = jnp.concatenate([uw[:, :, GDN_DV:], q3 * eb], axis=1)
    z3 = heads(z_ref[...], 0, GDN_DV)

    s = s_ref[...]
    for c in range(NCH):
        ws = _bmm(chunk(wq, c), s)
        v_new = chunk(uw, c)[:, :, :GDN_DV] - ws[:, :C]
        o = ws[:, C:] + _bmm(chunk(attn, c), v_new)
        s = s * chunk(gl, c) + _bmm(jnp.swapaxes(chunk(kd, c), 1, 2), v_new)
        o = _rms_rows(o, nw_ref[...]) * _silu(chunk(z3, c))
        for h in range(H):
            o_ref[c * C:(c + 1) * C, h * GDN_DV:(h + 1) * GDN_DV] = o[h].astype(o_ref.dtype)
    s_ref[...] = s

    @pl.when(ci == pl.num_programs(1) - 1)
    def _():
        sout_ref[...] = s


def _gdn(proj, ab, conv_w_e, alog_pad, dtb_pad, norm_e, B, L, C, NCH=1, prev8=None, s0=None, out_dtype=F32):
    has_state = prev8 is not None
    R = C * NCH
    nc = L // R
    W = GDN_HEADS * GDN_DK
    base = (2 * DA_W) // W
    row = lambda b, c: (b * nc + c, 0)
    in_specs = [
        pl.BlockSpec((R, W), lambda b, c: (b * nc + c, base)),
        pl.BlockSpec((R, W), lambda b, c: (b * nc + c, base + 1)),
        pl.BlockSpec((R, W), lambda b, c: (b * nc + c, base + 2)),
        pl.BlockSpec((R, W), lambda b, c: (b * nc + c, base + 3)),
        pl.BlockSpec((R, LANES), row),
        pl.BlockSpec((GDN_CONV, GDN_QKV), lambda b, c: (0, 0)),
        pl.BlockSpec((1, LANES), lambda b, c: (0, 0)),
        pl.BlockSpec((1, LANES), lambda b, c: (0, 0)),
        pl.BlockSpec((1, GDN_DV), lambda b, c: (0, 0)),
    ]
    args = [proj, proj, proj, proj, ab, conv_w_e, alog_pad, dtb_pad, norm_e]
    state_block = (None, GDN_HEADS, GDN_DK, GDN_DV)
    if has_state:
        in_specs += [pl.BlockSpec((None, SUBLANES, GDN_QKV), lambda b, c: (b, 0, 0)),
                     pl.BlockSpec(state_block, lambda b, c: (b, 0, 0, 0))]
        args += [prev8, s0]
    return pl.pallas_call(
        functools.partial(_gdn_kernel, C=C, NCH=NCH, has_state=has_state),
        grid=(B, nc),
        in_specs=in_specs,
        out_specs=[pl.BlockSpec((R, W), row),
                   pl.BlockSpec(state_block, lambda b, c: (b, 0, 0, 0))],
        out_shape=[jax.ShapeDtypeStruct((B * L, W), out_dtype),
                   jax.ShapeDtypeStruct((B, GDN_HEADS, GDN_DK, GDN_DV), F32)],
        scratch_shapes=[pltpu.VMEM((SUBLANES + R, GDN_QKV), F32),
                        pltpu.VMEM((GDN_HEADS, GDN_DK, GDN_DV), F32)],
        compiler_params=_cparams("parallel", "arbitrary"),
        name="gdn_state" if has_state else "gdn_fresh",
    )(*args)


def _sub_block_rows(x4, b):
    sub = lax.broadcasted_iota(jnp.int32, (1, 1, SUBLANES, 1), 2)
    out = None
    for start in range(0, SUBLANES, b):
        mid = start + b // 2
        piece = jnp.broadcast_to(x4[:, :, mid:mid + 1, :], x4.shape)
        out = piece if out is None else jnp.where(sub >= start, piece, out)
    return out


def _hgrn_kernel(*refs, C, HB, SB, layer, has_state):
    if has_state:
        (q_ref, f_ref, i_ref, g_ref, lower_ref, nw_ref, s0_ref, o_ref, sout_ref, st_ref) = refs
    else:
        (q_ref, f_ref, i_ref, g_ref, lower_ref, nw_ref, o_ref, sout_ref, st_ref) = refs
    ci = pl.program_id(2)
    G = HB * SB
    K = HG_EXPAND

    @pl.when(ci == 0)
    def _():
        for h in range(HB):
            for sb in range(SB):
                st_ref[h * SB + sb] = s0_ref[sb, h].T if has_state else jnp.zeros((HG_DI, K), F32)

    def heads(x):
        return jnp.stack([x[:, h * K:(h + 1) * K] for h in range(HB)]).reshape(G, C, K)

    low = lower_ref[...]
    ex = jnp.exp(low - jnp.max(low, axis=0, keepdims=True))
    lb = jnp.sum(ex[1:layer + 1], axis=0, keepdims=True) / jnp.sum(ex, axis=0, keepdims=True)

    fg2 = lb + (1.0 - lb) * _sigmoid(f_ref[...])
    logf2 = jnp.log(fg2)
    ti = lax.broadcasted_iota(jnp.int32, (1, C, C), 1)
    sj = lax.broadcasted_iota(jnp.int32, (1, C, C), 2)
    tri = (sj[0] <= ti[0]).astype(BF16)
    bc2 = jnp.concatenate([_cumsum_rows(tri, logf2[sb * C:(sb + 1) * C]) for sb in range(SB)], axis=0)

    qt = heads(_silu(q_ref[...]) * (K ** -0.5))
    kk = heads(1.0 - fg2)
    v = heads(i_ref[...])
    bc = heads(bc2)
    st = st_ref[...]
    o = _bmm_nt(qt * jnp.exp(bc), st)

    ri = lax.broadcasted_iota(jnp.int32, (1, C, 1), 1)
    a = jnp.where(ti == sj, jnp.sum(qt * kk, axis=-1, keepdims=True), 0.0)
    bc4 = bc.reshape(G, C // SUBLANES, SUBLANES, K)
    b = 2
    while b <= C:
        if b <= SUBLANES:
            ref = _sub_block_rows(bc4, b).reshape(G, C, K)
        else:
            ref = jnp.concatenate(
                [jnp.broadcast_to(bc[:, m:m + 1, :], (G, b, K)) for m in range(b // 2, C, b)], axis=1)
        upper = (ri % b) >= (b // 2)
        d = bc - ref
        e = jnp.exp(jnp.where(upper, d, -d))
        qs = jnp.where(upper, qt * e, 0.0)
        ks = jnp.where(upper, 0.0, kk * e)
        a = a + jnp.where(ti // b == sj // b, _bmm_nt(qs, ks), 0.0)
        b *= 2
    o = o + _bmm(a, v)

    blast = bc[:, C - 1:C, :]
    st = st * jnp.exp(blast) + _bmm(jnp.swapaxes(v, 1, 2), kk * jnp.exp(blast - bc))
    st_ref[...] = st
    nw = jnp.stack([nw_ref[:, h * K:(h + 1) * K] for h in range(HB) for _ in range(SB)])
    o = _rms_rows(o, nw) * heads(_silu(g_ref[...]))
    for h in range(HB):
        for sb in range(SB):
            o_ref[sb * C:(sb + 1) * C, h * K:(h + 1) * K] = o[h * SB + sb].astype(o_ref.dtype)

    @pl.when(ci == pl.num_programs(2) - 1)
    def _():
        for h in range(HB):
            for sb in range(SB):
                sout_ref[sb, h] = st[h * SB + sb].T


def _hgrn(proj, hg_lower, hg_norm_o, layer, B, L, C, s0=None, HB=8, SB=1):
    has_state = s0 is not None
    nc = L // C
    assert SB == 1 or nc == 1
    nhg = HG_HEADS // HB
    Wb = HB * HG_EXPAND
    R = SB * C

    def col(section):
        return lambda b, hg, c: (b * nc + c, section * nhg + hg)

    in_specs = [pl.BlockSpec((R, Wb), col(s)) for s in range(4)] + [
        pl.BlockSpec((DEPTH, Wb), lambda b, hg, c: (0, hg)),
        pl.BlockSpec((1, Wb), lambda b, hg, c: (0, hg)),
    ]
    args = [proj, proj, proj, proj, hg_lower, hg_norm_o]
    state_block = (SB, HB, HG_EXPAND, HG_DI)
    if has_state:
        in_specs.append(pl.BlockSpec(state_block, lambda b, hg, c: (b, hg, 0, 0)))
        args.append(s0)
    return pl.pallas_call(
        functools.partial(_hgrn_kernel, C=C, HB=HB, SB=SB, layer=layer, has_state=has_state),
        grid=(B // SB, nhg, nc),
        in_specs=in_specs,
        out_specs=[pl.BlockSpec((R, Wb), col(0)),
                   pl.BlockSpec(state_block, lambda b, hg, c: (b, hg, 0, 0))],
        out_shape=[jax.ShapeDtypeStruct((B * L, D_MODEL), BF16),
                   jax.ShapeDtypeStruct((B, HG_HEADS, HG_EXPAND, HG_DI), F32)],
        scratch_shapes=[pltpu.VMEM((HB * SB, HG_DI, HG_EXPAND), F32)],
        compiler_params=_cparams("parallel", "parallel", "arbitrary"),
        name="hgrn_state" if has_state else "hgrn_fresh",
    )(*args)


def _pad_lanes(v, offset=0):
    return jnp.zeros((1, LANES), F32).at[0, offset:offset + v.shape[0]].set(v.astype(F32))


def kernel(x_prompt, x_sample, cache_k, cache_v, state_gdn_conv, state_gdn, state_hgrn, page_table, norm_mix, norm_ffn, norm_final, w_in_even, w_out_even, da_lambda, da_subln, gdn_conv_w, gdn_a_log, gdn_dt_bias, gdn_norm, w_in_odd, w_out_odd, hg_lower, hg_norm, w_gate_up, w_down):
    Bp, Lp = x_prompt.shape[:2]
    DB, Ls = x_sample.shape[:2]
    past_len = page_table.shape[1] * PAGE_SIZE
    pos_p = np.tile(np.arange(Lp), Bp)
    pos_s = np.tile(past_len + np.arange(Ls), DB)
    Mp, Ms = Bp * Lp, DB * Ls
    tm_p, tm_s = 1024, Ms

    hp = x_prompt.reshape(Mp, D_MODEL)
    hs = x_sample.reshape(Ms, D_MODEL)
    outs = {k: [] for k in ("k_p", "v_p", "conv_p", "gdn_p", "hg_p", "k_s", "v_s", "conv_s", "gdn_s", "hg_s")}

    for l in range(DEPTH):
        nw_mix = norm_mix[l].reshape(1, D_MODEL)
        nw_ffn = norm_ffn[l].reshape(1, D_MODEL)
        if l % 2 == 0:
            e = l // 2
            lam_init = 0.8 - 0.6 * math.exp(-0.3 * l)
            w_in_t = jnp.transpose(w_in_even, (0, 2, 1))
            wab = jnp.pad(w_in_t[e, EVEN_MAIN:], ((0, LANES - 2 * GDN_HEADS), (0, 0)))
            alog_pad = _pad_lanes(gdn_a_log[e])
            dtb_pad = _pad_lanes(gdn_dt_bias[e])
            sub = da_subln[e].reshape(1, DA_V_DIM)
            gnorm = gdn_norm[e].reshape(1, GDN_DV)
            conv_cols = slice(2 * DA_W, 2 * DA_W + GDN_QKV)
            proj_p, kt_p, ab_p, proj_s, kt_s, ab_s = _inproj_even(
                hp, hs, nw_mix, w_in_t, e, wab, pos_p, pos_s, Bp, tm_p)
            oa_p = _attn_prompt(proj_p, kt_p, da_lambda[e], sub, Bp, Lp, lam_init)
            gdn_c = min(GDN_CHUNK, Lp)
            ob_p, st = _gdn(proj_p, ab_p, gdn_conv_w[e], alog_pad, dtb_pad, gnorm, Bp, Lp, gdn_c,
                            NCH=2 if Lp % (2 * gdn_c) == 0 else 1, out_dtype=BF16)
            outs["k_p"].append(kt_p.reshape(Bp, DA_HEADS, 2, DA_QK_DIM, Lp).transpose(0, 4, 1, 2, 3))
            outs["v_p"].append(proj_p[:, DA_W:2 * DA_W].reshape(Bp, Lp, DA_HEADS, DA_V_DIM))
            outs["conv_p"].append(proj_p.reshape(Bp, Lp, EVEN_ROWS)[:, Lp - (GDN_CONV - 1):, conv_cols])
            outs["gdn_p"].append(st)
            oa_s = _attn_sample(proj_s, kt_s[0], cache_k, cache_v, e, page_table, da_lambda[e], sub,
                                DB, Ls, lam_init)
            prev8 = jnp.pad(state_gdn_conv[e], ((0, 0), (SUBLANES - (GDN_CONV - 1), 0), (0, 0)))
            ob_s, st = _gdn(proj_s, ab_s, gdn_conv_w[e], alog_pad, dtb_pad, gnorm, DB, Ls, min(GDN_CHUNK, Ls),
                            prev8=prev8, s0=state_gdn[e])
            gq = proj_s[:, conv_cols].reshape(DB, Ls, GDN_QKV)
            conv_s = jnp.concatenate([state_gdn_conv[e], gq], axis=1)[:, Ls:]
            outs["k_s"].append(kt_s[0].reshape(DA_HEADS, 2, DA_QK_DIM, DB, Ls).transpose(3, 4, 0, 1, 2))
            outs["v_s"].append(proj_s[:, DA_W:2 * DA_W].reshape(DB, Ls, DA_HEADS, DA_V_DIM))
            outs["conv_s"].append(conv_s)
            outs["gdn_s"].append(st)
            hp, hs = _mm_res([oa_p, ob_p], [oa_s, ob_s], w_out_even, e, hp, hs, tm_p)
        else:
            o = l // 2
            hnorm = hg_norm[o].reshape(1, D_MODEL)
            proj_p, proj_s = _norm_matmul(hp, hs, nw_mix, w_in_odd, o, tm_p)
            y_p, st = _hgrn(proj_p, hg_lower, hnorm, l, Bp, Lp, min(HG_CHUNK, Lp))
            outs["hg_p"].append(st)
            y_s, st = _hgrn(proj_s, hg_lower, hnorm, l, DB, Ls, min(HG_CHUNK, Ls), s0=state_hgrn[o],
                            SB=4 if Ls <= HG_CHUNK and DB % 4 == 0 else 1)
            outs["hg_s"].append(st)
            hp, hs = _mm_res([y_p], [y_s], w_out_odd, o, hp, hs, tm_p)
        act_p, act_s = _ffn_up(hp, hs, nw_ffn, w_gate_up, l, tm_p)
        hp, hs = _mm_res([act_p], [act_s], w_down, l, hp, hs, tm_p, tk=D_FF // 2)

    nwf = norm_final.reshape(1, D_MODEL)
    y_prompt = _final_norm(hp, nwf, tm_p).reshape(Bp, Lp, D_MODEL)
    y_sample = _final_norm(hs, nwf, tm_s).reshape(DB, Ls, D_MODEL)
    st = lambda k: jnp.stack(outs[k])
    return (y_prompt, y_sample, st("k_p"), st("v_p"), st("conv_p"), st("gdn_p"), st("hg_p"),
            st("k_s"), st("v_s"), st("conv_s"), st("gdn_s"), st("hg_s"))
```

```python
import functools
import math

import numpy as np
import jax
import jax.numpy as jnp
from jax import lax
from jax.experimental import pallas as pl
from jax.experimental.pallas import tpu as pltpu

F32 = jnp.float32
BF16 = jnp.bfloat16

D_MODEL = 2048
DEPTH = 2
PAGE_SIZE = 128
NORM_EPS = 1e-6

DA_HEADS = 8
DA_QK_DIM = 64
DA_V_DIM = 2 * DA_QK_DIM
DA_SCALE = DA_QK_DIM ** -0.5
ROT_DIM = DA_QK_DIM // 4
ROPE_THETA = 500000.0

GDN_HEADS = 8
GDN_DK = 128
GDN_DV = 128
GDN_QKV = GDN_HEADS * (2 * GDN_DK + GDN_DV)
GDN_CONV = 4
GDN_CHUNK = 64

HG_EXPAND = 128
HG_HEADS = D_MODEL // HG_EXPAND
HG_DI = D_MODEL // HG_HEADS
HG_CHUNK = 64

D_FF = -(-(8 * D_MODEL) // (3 * 256)) * 256

DA_W = DA_HEADS * 2 * DA_QK_DIM
EVEN_MAIN = 3 * DA_W + GDN_QKV + GDN_HEADS * GDN_DV
EVEN_ROWS = EVEN_MAIN - DA_W

LANES = 128
SUBLANES = 8
VMEM_LIMIT = 56 * 1024 * 1024
NEG_BIG = -1e30


def _cparams(*sem):
    return pltpu.CompilerParams(dimension_semantics=sem, vmem_limit_bytes=VMEM_LIMIT)


def _split2(x):
    hi = x.astype(BF16)
    lo = (x - hi.astype(F32)).astype(BF16)
    return hi, lo


def _cumsum_rows(tri, x):
    hi = x.astype(BF16)
    r = x - hi.astype(F32)
    mid = r.astype(BF16)
    lo = (r - mid.astype(F32)).astype(BF16)
    d = functools.partial(jnp.dot, preferred_element_type=F32)
    return d(tri, hi) + (d(tri, mid) + d(tri, lo))


def _sigmoid(x):
    return 0.5 * jnp.tanh(0.5 * x) + 0.5


def _silu(x):
    return x * _sigmoid(x)


def _rms_rows(x, w):
    return x * lax.rsqrt(jnp.mean(x * x, axis=-1, keepdims=True) + NORM_EPS) * w


W_SLABS = 4


def _slab_specs(k_rows, tn, index, transposed=False):
    def spec(s):
        def index_map(*g):
            layer, kt, col = index(*g)
            return (layer, col * W_SLABS + s, kt) if transposed else (layer, kt * W_SLABS + s, col)
        block = (None, tn // W_SLABS, k_rows) if transposed else (None, k_rows // W_SLABS, tn)
        return pl.BlockSpec(block, index_map)
    return [spec(s) for s in range(W_SLABS)]


def _dot_nt(a, b):
    return lax.dot_general(a, b, (((1,), (1,)), ((), ())), preferred_element_type=F32)


def _load_weight(w_refs, wbuf_ref):
    rows = wbuf_ref.shape[0] // len(w_refs)
    for s, r in enumerate(w_refs):
        wbuf_ref[s * rows:(s + 1) * rows, :] = r[...].astype(BF16)
    return wbuf_ref[...]


def _small_col(i, j, n_i):
    return jnp.where(i == n_i - 1, j, 0)


def _inproj_even_kernel(xp_ref, xs_ref, nw_ref, *rest, n_i, n_q_tiles, n_k_tiles, tn):
    w_refs = rest[:W_SLABS]
    (wab_ref, lane_p0, lane_p1, sub_p0, sub_p1, lane_s0, lane_s1, sub_s0, sub_s1,
     op_ref, ktp_ref, abp_ref, os_ref, kts_ref, abs_ref,
     xnp_ref, xns_ref, wbuf_ref) = rest[W_SLABS:]
    i = pl.program_id(0)
    j = pl.program_id(1)
    last = i == n_i - 1
    half = ROT_DIM // 2

    def prologue(x_ref, xn_ref, ab_ref):
        xn = _rms_rows(x_ref[...], nw_ref[...]).astype(BF16)
        xn_ref[...] = xn
        ab_ref[...] = _dot_nt(xn, wab_ref[...].astype(BF16))

    @pl.when(j == 0)
    def _():
        prologue(xp_ref, xnp_ref, abp_ref)

    @pl.when(jnp.logical_and(last, j == 0))
    def _():
        prologue(xs_ref, xns_ref, abs_ref)

    w = _load_weight(w_refs, wbuf_ref)

    def rotate(a, cosf, sinf, axis):
        ch = lax.broadcasted_iota(jnp.int32, a.shape, axis) % DA_QK_DIM
        partner = jnp.where(ch < half, -pltpu.roll(a, LANES - half, axis), pltpu.roll(a, half, axis))
        return a * cosf + partner * sinf

    def q_tile(xn_ref, o_ref, tabs):
        acc = _dot_nt(xn_ref[...], w)
        cosf, sinf = (t[...] for t in tabs)
        for c in range(tn // LANES):
            o_ref[:, c * LANES:(c + 1) * LANES] = rotate(acc[:, c * LANES:(c + 1) * LANES], cosf, sinf, 1)

    def k_tile(xn_ref, kt_ref, tabs):
        acc = _dot_nt(w, xn_ref[...])
        cosf, sinf = (t[...] for t in tabs)
        for c in range(tn // LANES):
            kt_ref[c * LANES:(c + 1) * LANES, :] = rotate(acc[c * LANES:(c + 1) * LANES, :], cosf, sinf, 0)

    def plain_tile(xn_ref, o_ref):
        o_ref[...] = _dot_nt(xn_ref[...], w)

    is_q = j < n_q_tiles
    is_k = jnp.logical_and(j >= n_q_tiles, j < n_q_tiles + n_k_tiles)
    is_plain = j >= n_q_tiles + n_k_tiles

    @pl.when(is_q)
    def _():
        q_tile(xnp_ref, op_ref, (lane_p0, lane_p1))

    @pl.when(is_k)
    def _():
        k_tile(xnp_ref, ktp_ref, (sub_p0, sub_p1))

    @pl.when(is_plain)
    def _():
        plain_tile(xnp_ref, op_ref)

    @pl.when(jnp.logical_and(last, is_q))
    def _():
        q_tile(xns_ref, os_ref, (lane_s0, lane_s1))

    @pl.when(jnp.logical_and(last, is_k))
    def _():
        k_tile(xns_ref, kts_ref, (sub_s0, sub_s1))

    @pl.when(jnp.logical_and(last, is_plain))
    def _():
        plain_tile(xns_ref, os_ref)


def _rope_tables(pos):
    half = ROT_DIM // 2
    inv = np.power(np.float32(ROPE_THETA), -np.arange(half, dtype=np.float32) * np.float32(2.0) / np.float32(ROT_DIM))
    ang = (pos.astype(np.float32)[:, None] * inv[None, :]).astype(np.float32)
    cos = np.cos(ang.astype(np.float64)).astype(np.float32)
    sin = np.sin(ang.astype(np.float64)).astype(np.float32)
    n = pos.shape[0]
    cosf = np.ones((n, LANES), np.float32)
    sinf = np.zeros((n, LANES), np.float32)
    for base in range(0, LANES, DA_QK_DIM):
        for off in (0, half):
            cosf[:, base + off:base + off + half] = cos
            sinf[:, base + off:base + off + half] = sin
    return cosf, sinf


def _inproj_even(xp, xs, nw, w_in_t, e, wab, pos_p, pos_s, B, tm, tn=512):
    M = xp.shape[0]
    Ms = xs.shape[0]
    L = M // B
    lt = L // tm
    n_i = M // tm
    tabs_p = _rope_tables(pos_p)
    tabs_s = _rope_tables(pos_s)
    lane = lambda tabs: [jnp.asarray(t) for t in tabs]
    sub = lambda tabs: [jnp.asarray(np.ascontiguousarray(t.T)) for t in tabs]
    n_q, n_k = DA_W // tn, DA_W // tn
    n_tiles = EVEN_MAIN // tn
    kern = functools.partial(_inproj_even_kernel, n_i=n_i, n_q_tiles=n_q, n_k_tiles=n_k, tn=tn)
    row = lambda i, j: (i, 0)
    col = lambda i, j: (0, i)
    fixed = lambda i, j: (0, 0)
    once = pl.Buffered(1)
    row_col = lambda j: jnp.where(j < n_q, j, jnp.maximum(j - n_k, n_q - 1))
    kt_col = lambda j: jnp.clip(j - n_q, 0, n_k - 1)
    return pl.pallas_call(
        kern,
        grid=(n_i, n_tiles),
        in_specs=[
            pl.BlockSpec((tm, D_MODEL), row),
            pl.BlockSpec((Ms, D_MODEL), fixed, pipeline_mode=once),
            pl.BlockSpec((1, D_MODEL), fixed, pipeline_mode=once),
        ] + _slab_specs(D_MODEL, tn, lambda i, j: (e, 0, j), transposed=True) + [
            pl.BlockSpec((LANES, D_MODEL), fixed, pipeline_mode=once),
        ] + [pl.BlockSpec((tm, LANES), row, pipeline_mode=once)] * 2
          + [pl.BlockSpec((LANES, tm), col, pipeline_mode=once)] * 2
          + [pl.BlockSpec((Ms, LANES), fixed, pipeline_mode=once)] * 2
          + [pl.BlockSpec((LANES, Ms), fixed, pipeline_mode=once)] * 2,
        out_specs=[
            pl.BlockSpec((tm, tn), lambda i, j: (i, row_col(j))),
            pl.BlockSpec((None, tn, tm), lambda i, j: (i // lt, kt_col(j), i % lt)),
            pl.BlockSpec((tm, LANES), row),
            pl.BlockSpec((Ms, tn), lambda i, j: (0, _small_col(i, row_col(j), n_i))),
            pl.BlockSpec((None, tn, Ms), lambda i, j: (0, _small_col(i, kt_col(j), n_i), 0)),
            pl.BlockSpec((Ms, LANES), fixed),
        ],
        out_shape=[jax.ShapeDtypeStruct((M, EVEN_ROWS), F32),
                   jax.ShapeDtypeStruct((B, DA_W, L), F32),
                   jax.ShapeDtypeStruct((M, LANES), F32),
                   jax.ShapeDtypeStruct((Ms, EVEN_ROWS), F32),
                   jax.ShapeDtypeStruct((1, DA_W, Ms), F32),
                   jax.ShapeDtypeStruct((Ms, LANES), F32)],
        scratch_shapes=[pltpu.VMEM((tm, D_MODEL), BF16), pltpu.VMEM((Ms, D_MODEL), BF16),
                        pltpu.VMEM((tn, D_MODEL), BF16)],
        compiler_params=_cparams("arbitrary", "arbitrary"),
        name="inproj_even",
    )(xp, xs, nw, *([w_in_t] * W_SLABS), wab, *lane(tabs_p), *sub(tabs_p), *lane(tabs_s), *sub(tabs_s))


def _norm_matmul_kernel(xp_ref, xs_ref, nw_ref, *rest, n_i):
    w_refs = rest[:W_SLABS]
    op_ref, os_ref, xnp_ref, xns_ref, wbuf_ref = rest[W_SLABS:]
    i = pl.program_id(0)
    j = pl.program_id(1)
    last = i == n_i - 1

    @pl.when(j == 0)
    def _():
        xnp_ref[...] = _rms_rows(xp_ref[...], nw_ref[...]).astype(BF16)

    @pl.when(jnp.logical_and(last, j == 0))
    def _():
        xns_ref[...] = _rms_rows(xs_ref[...], nw_ref[...]).astype(BF16)

    w = _load_weight(w_refs, wbuf_ref)
    op_ref[...] = jnp.dot(xnp_ref[...], w, preferred_element_type=F32)

    @pl.when(last)
    def _():
        os_ref[...] = jnp.dot(xns_ref[...], w, preferred_element_type=F32)


def _norm_matmul(xp, xs, nw, w_all, l, tm, tn=512):
    M = xp.shape[0]
    Ms = xs.shape[0]
    N = w_all.shape[-1]
    n_i = M // tm
    return pl.pallas_call(
        functools.partial(_norm_matmul_kernel, n_i=n_i),
        grid=(n_i, N // tn),
        in_specs=[
            pl.BlockSpec((tm, D_MODEL), lambda i, j: (i, 0)),
            pl.BlockSpec((Ms, D_MODEL), lambda i, j: (0, 0)),
            pl.BlockSpec((1, D_MODEL), lambda i, j: (0, 0)),
        ] + _slab_specs(D_MODEL, tn, lambda i, j: (l, 0, j)),
        out_specs=[pl.BlockSpec((tm, tn), lambda i, j: (i, j)),
                   pl.BlockSpec((Ms, tn), lambda i, j: (0, _small_col(i, j, n_i)))],
        out_shape=[jax.ShapeDtypeStruct((M, N), F32), jax.ShapeDtypeStruct((Ms, N), F32)],
        scratch_shapes=[pltpu.VMEM((tm, D_MODEL), BF16), pltpu.VMEM((Ms, D_MODEL), BF16),
                        pltpu.VMEM((D_MODEL, tn), BF16)],
        compiler_params=_cparams("arbitrary", "arbitrary"),
        name="norm_matmul",
    )(xp, xs, nw, *([w_all] * W_SLABS))


def _ffn_up_kernel(xp_ref, xs_ref, nw_ref, *rest, n_i):
    wg_refs = rest[:W_SLABS]
    wu_refs = rest[W_SLABS:2 * W_SLABS]
    op_ref, os_ref, xnp_ref, xns_ref, wgbuf_ref, wubuf_ref = rest[2 * W_SLABS:]
    i = pl.program_id(0)
    j = pl.program_id(1)
    last = i == n_i - 1

    @pl.when(j == 0)
    def _():
        xnp_ref[...] = _rms_rows(xp_ref[...], nw_ref[...]).astype(BF16)

    @pl.when(jnp.logical_and(last, j == 0))
    def _():
        xns_ref[...] = _rms_rows(xs_ref[...], nw_ref[...]).astype(BF16)

    wg = _load_weight(wg_refs, wgbuf_ref)
    wu = _load_weight(wu_refs, wubuf_ref)

    def tile(xn_ref, o_ref):
        xn = xn_ref[...]
        g = jnp.dot(xn, wg, preferred_element_type=F32)
        u = jnp.dot(xn, wu, preferred_element_type=F32)
        o_ref[...] = (_silu(g) * u).astype(BF16)

    tile(xnp_ref, op_ref)

    @pl.when(last)
    def _():
        tile(xns_ref, os_ref)


def _ffn_up(xp, xs, nw, w_gate_up, l, tm, tn=512):
    M = xp.shape[0]
    Ms = xs.shape[0]
    nj = D_FF // tn
    n_i = M // tm
    return pl.pallas_call(
        functools.partial(_ffn_up_kernel, n_i=n_i),
        grid=(n_i, nj),
        in_specs=[
            pl.BlockSpec((tm, D_MODEL), lambda i, j: (i, 0)),
            pl.BlockSpec((Ms, D_MODEL), lambda i, j: (0, 0), pipeline_mode=pl.Buffered(1)),
            pl.BlockSpec((1, D_MODEL), lambda i, j: (0, 0), pipeline_mode=pl.Buffered(1)),
        ] + _slab_specs(D_MODEL, tn, lambda i, j: (l, 0, j))
          + _slab_specs(D_MODEL, tn, lambda i, j: (l, 0, j + nj)),
        out_specs=[pl.BlockSpec((tm, tn), lambda i, j: (i, j)),
                   pl.BlockSpec((Ms, tn), lambda i, j: (0, _small_col(i, j, n_i)))],
        out_shape=[jax.ShapeDtypeStruct((M, D_FF), BF16), jax.ShapeDtypeStruct((Ms, D_FF), BF16)],
        scratch_shapes=[pltpu.VMEM((tm, D_MODEL), BF16), pltpu.VMEM((Ms, D_MODEL), BF16),
                        pltpu.VMEM((D_MODEL, tn), BF16), pltpu.VMEM((D_MODEL, tn), BF16)],
        compiler_params=_cparams("arbitrary", "arbitrary"),
        name="ffn_up",
    )(xp, xs, nw, *([w_gate_up] * (2 * W_SLABS)))


def _mm_res_kernel(*refs, nk, na, n_i):
    ap_refs = refs[:na]
    as_refs = refs[na:2 * na]
    w_refs = refs[2 * na:2 * na + W_SLABS]
    (rp_ref, rs_ref, op_ref, os_ref,
     accp_ref, accs_ref, wbuf_ref, abufp_ref, abufs_ref) = refs[2 * na + W_SLABS:]
    i = pl.program_id(0)
    k = pl.program_id(2)
    last = i == n_i - 1
    w = _load_weight(w_refs, wbuf_ref)

    def tile(a_refs, abuf_ref, r_ref, o_ref, acc_ref):
        if na == 1:
            a = a_refs[0][...].astype(BF16)
        else:
            off = 0
            for r in a_refs:
                abuf_ref[:, off:off + r.shape[1]] = r[...].astype(BF16)
                off += r.shape[1]
            a = abuf_ref[...]
        p = jnp.dot(a, w, preferred_element_type=F32)
        if nk == 1:
            o_ref[...] = r_ref[...] + p
        else:
            @pl.when(k == 0)
            def _():
                acc_ref[...] = p

            @pl.when(jnp.logical_and(k > 0, k < nk - 1))
            def _():
                acc_ref[...] += p

            @pl.when(k == nk - 1)
            def _():
                o_ref[...] = r_ref[...] + (acc_ref[...] + p)

    tile(ap_refs, abufp_ref, rp_ref, op_ref, accp_ref)

    @pl.when(last)
    def _():
        tile(as_refs, abufs_ref, rs_ref, os_ref, accs_ref)


def _mm_res(ap_list, as_list, w_all, l, res_p, res_s, tm, tn=512, tk=None):
    M = res_p.shape[0]
    Ms = res_s.shape[0]
    K = sum(a.shape[1] for a in ap_list)
    N = w_all.shape[-1]
    tk = K if tk is None else tk
    nk = K // tk
    na = len(ap_list)
    n_i = M // tm
    assert na == 1 or nk == 1
    small = lambda i, j, k: (0, _small_col(i, j, n_i))
    if na == 1:
        a_specs = [pl.BlockSpec((tm, tk), lambda i, j, k: (i, k)),
                   pl.BlockSpec((Ms, tk), lambda i, j, k: (0, jnp.where(i == n_i - 1, k, 0)))]
    else:
        a_specs = ([pl.BlockSpec((tm, a.shape[1]), lambda i, j, k: (i, 0)) for a in ap_list]
                   + [pl.BlockSpec((Ms, a.shape[1]), lambda i, j, k: (0, 0)) for a in as_list])
    abuf = lambda rows: pltpu.VMEM((rows, tk) if na > 1 else (SUBLANES * 2, LANES), BF16)
    return pl.pallas_call(
        functools.partial(_mm_res_kernel, nk=nk, na=na, n_i=n_i),
        grid=(n_i, N // tn, nk),
        in_specs=a_specs + _slab_specs(tk, tn, lambda i, j, k: (l, k, j))
                 + [pl.BlockSpec((tm, tn), lambda i, j, k: (i, j)), pl.BlockSpec((Ms, tn), small)],
        out_specs=[pl.BlockSpec((tm, tn), lambda i, j, k: (i, j)), pl.BlockSpec((Ms, tn), small)],
        out_shape=[jax.ShapeDtypeStruct((M, N), F32), jax.ShapeDtypeStruct((Ms, N), F32)],
        scratch_shapes=[pltpu.VMEM((tm, tn), F32), pltpu.VMEM((Ms, tn), F32), pltpu.VMEM((tk, tn), BF16),
                        abuf(tm), abuf(Ms)],
        compiler_params=_cparams("arbitrary", "arbitrary", "arbitrary"),
        name="matmul_residual",
    )(*ap_list, *as_list, *([w_all] * W_SLABS), res_p, res_s)


def _final_norm_kernel(x_ref, w_ref, o_ref):
    o_ref[...] = _rms_rows(x_ref[...], w_ref[...])


def _final_norm(x, w, tm):
    M = x.shape[0]
    return pl.pallas_call(
        _final_norm_kernel,
        grid=(M // tm,),
        in_specs=[pl.BlockSpec((tm, D_MODEL), lambda i: (i, 0)),
                  pl.BlockSpec((1, D_MODEL), lambda i: (0, 0))],
        out_specs=pl.BlockSpec((tm, D_MODEL), lambda i: (i, 0)),
        out_shape=jax.ShapeDtypeStruct((M, D_MODEL), F32),
        compiler_params=_cparams("parallel"),
        name="final_norm",
    )(x, w)


def _lambda_of(lam_ref, lam_init):
    lf = lam_ref[...]
    s1 = jnp.sum(lf[0:1] * lf[1:2], axis=-1, keepdims=True)
    s2 = jnp.sum(lf[2:3] * lf[3:4], axis=-1, keepdims=True)
    return jnp.exp(s1) - jnp.exp(s2) + lam_init


def _attn_prompt_kernel(qi_ref, ki_ref, lam_ref, q_ref, k_ref, v_ref, sub_ref, o_ref,
                        m_ref, l_ref, acc_ref, *, t, lam_init):
    step = pl.program_id(2)
    qi = qi_ref[step]
    ki = ki_ref[step]

    @pl.when(ki == 0)
    def _():
        m_ref[...] = jnp.full(m_ref.shape, NEG_BIG, F32)
        l_ref[...] = jnp.zeros(l_ref.shape, F32)
        acc_ref[...] = jnp.zeros(acc_ref.shape, F32)

    def update(diagonal):
        q = q_ref[...] * DA_SCALE
        kt = k_ref[...].astype(BF16)
        v = v_ref[...].astype(BF16)
        lane_map = lax.broadcasted_iota(jnp.int32, (1, LANES), 1) // DA_QK_DIM
        if diagonal:
            causal = (lax.broadcasted_iota(jnp.int32, (1, t), 1)
                      <= lax.broadcasted_iota(jnp.int32, (t, 1), 0))
        for m in range(2):
            qm = jnp.where(lane_map == m, q, 0.0).astype(BF16)
            s = jnp.dot(qm, kt, preferred_element_type=F32)
            if diagonal:
                s = jnp.where(causal, s, NEG_BIG)
            m_prev = m_ref[m]
            m_next = jnp.maximum(m_prev, jnp.max(s, axis=1, keepdims=True))
            alpha = jnp.exp(m_prev - m_next)
            p = jnp.exp(s - jnp.concatenate([m_next] * (t // LANES), axis=1))
            l_ref[m] = alpha * l_ref[m] + jnp.sum(p, axis=1, keepdims=True)
            acc_ref[m] = alpha * acc_ref[m] + jnp.dot(p.astype(BF16), v, preferred_element_type=F32)
            m_ref[m] = m_next

    @pl.when(ki < qi)
    def _():
        update(False)

    @pl.when(ki == qi)
    def _():
        update(True)
        lam = _lambda_of(lam_ref, lam_init)
        o = acc_ref[0] / l_ref[0] - lam * (acc_ref[1] / l_ref[1])
        o_ref[...] = (_rms_rows(o, sub_ref[...]) * (1.0 - lam_init)).astype(o_ref.dtype)


def _attn_prompt(rows, kt, da_lambda_e, subln_e, B, S, lam_init, t=512):
    nq = S // t
    pairs = [(qi, ki) for qi in range(nq) for ki in range(qi + 1)]
    qi_tab = jnp.asarray([p[0] for p in pairs], jnp.int32)
    ki_tab = jnp.asarray([p[1] for p in pairs], jnp.int32)
    kern = functools.partial(_attn_prompt_kernel, t=t, lam_init=lam_init)
    grid_spec = pltpu.PrefetchScalarGridSpec(
        num_scalar_prefetch=2,
        grid=(B, DA_HEADS, len(pairs)),
        in_specs=[
            pl.BlockSpec((4, DA_QK_DIM), lambda b, h, s, qt, kt: (0, 0)),
            pl.BlockSpec((t, LANES), lambda b, h, s, qt, kt: (b * nq + qt[s], h)),
            pl.BlockSpec((None, LANES, t), lambda b, h, s, qt, kt: (b, h, kt[s])),
            pl.BlockSpec((t, LANES), lambda b, h, s, qt, kt: (b * nq + kt[s], DA_HEADS + h)),
            pl.BlockSpec((1, DA_V_DIM), lambda b, h, s, qt, kt: (0, 0)),
        ],
        out_specs=pl.BlockSpec((t, LANES), lambda b, h, s, qt, kt: (b * nq + qt[s], h)),
        scratch_shapes=[pltpu.VMEM((2, t, LANES), F32), pltpu.VMEM((2, t, LANES), F32),
                        pltpu.VMEM((2, t, DA_V_DIM), F32)],
    )
    return pl.pallas_call(
        kern,
        grid_spec=grid_spec,
        out_shape=jax.ShapeDtypeStruct((B * S, DA_W), BF16),
        compiler_params=_cparams("parallel", "parallel", "arbitrary"),
        name="diff_attn_prompt",
    )(qi_tab, ki_tab, da_lambda_e, rows, kt, rows, subln_e)


PAGE_RING = 32


def _attn_sample_kernel(pt_ref, lam_ref, qbd_ref, knt_ref, vn_ref, sub_ref, ck_hbm, cv_hbm, o_ref,
                        ring_ref, sem_ref, st_ref, mx_ref, psum_ref, oacc_ref, vpad_ref,
                        *, layer, G, n_steps, dec_seq, lam_init):
    b = pl.program_id(0)
    t = pl.program_id(1)
    n_pages = n_steps * G
    steps_per_seq = 2 * n_steps
    step = b * steps_per_seq + t
    total_steps = pl.num_programs(0) * steps_per_seq
    ahead = PAGE_RING // G - 1

    def page_copy(src_hbm, page, slot):
        return pltpu.make_async_copy(src_hbm.at[layer, page], ring_ref.at[slot], sem_ref.at[slot])

    def start_step(s):
        sb = s // steps_per_seq
        stp = s % steps_per_seq
        is_k = stp < n_steps
        first = sb * n_pages + jnp.where(is_k, stp, stp - n_steps) * G
        for g in range(G):
            slot = (s * G + g) % PAGE_RING
            page = pt_ref[first + g]

            @pl.when(is_k)
            def _():
                page_copy(ck_hbm, page, slot).start()

            @pl.when(jnp.logical_not(is_k))
            def _():
                page_copy(cv_hbm, page, slot).start()

    @pl.when(step == 0)
    def _():
        for s in range(ahead):
            start_step(jnp.int32(s))

    @pl.when(step + ahead < total_steps)
    def _():
        start_step(step + ahead)

    def wait_pages():
        slots = [(step * G + g) % PAGE_RING for g in range(G)]
        for slot in slots:
            page_copy(ck_hbm, 0, slot).wait()
        return slots

    @pl.when(t == 0)
    def _():
        mx_ref[...] = jnp.full(mx_ref.shape, NEG_BIG, F32)
        vpad_ref[...] = jnp.zeros(vpad_ref.shape, F32)
        vpad_ref[0:dec_seq, :] = vn_ref[...]

    def scores(kt_page, idx, mask=None):
        s = jnp.dot(qbd_ref[...], kt_page.astype(BF16), preferred_element_type=F32)
        if mask is not None:
            s = jnp.where(mask, s, NEG_BIG)
        st_ref[idx] = s
        mx_ref[...] = jnp.maximum(mx_ref[...], s)

    @pl.when(t < n_steps)
    def _():
        for g, slot in enumerate(wait_pages()):
            scores(ring_ref[slot], t * G + g)

    @pl.when(t == n_steps - 1)
    def _():
        qry = lax.broadcasted_iota(jnp.int32, (LANES, PAGE_SIZE), 0) % dec_seq
        key = lax.broadcasted_iota(jnp.int32, (LANES, PAGE_SIZE), 1)
        scores(knt_ref[...], n_pages, mask=key <= qry)
        row_max = jnp.max(mx_ref[...], axis=1, keepdims=True)
        mx_ref[...] = jnp.broadcast_to(row_max, mx_ref.shape)
        psum_ref[...] = jnp.zeros(psum_ref.shape, F32)
        oacc_ref[...] = jnp.zeros(oacc_ref.shape, F32)

    def accumulate(v_flat, idx):
        p = jnp.exp(st_ref[idx] - mx_ref[...])
        psum_ref[...] += p
        oacc_ref[...] += jnp.dot(p.astype(BF16), v_flat.astype(BF16), preferred_element_type=F32)

    def flat_values(page_ref):
        return jnp.concatenate(
            [page_ref[pl.ds(h, PAGE_SIZE, stride=DA_HEADS), :] for h in range(DA_HEADS)], axis=1)

    @pl.when(t >= n_steps)
    def _():
        for g, slot in enumerate(wait_pages()):
            accumulate(flat_values(ring_ref.at[slot]), (t - n_steps) * G + g)

    @pl.when(t == 2 * n_steps - 1)
    def _():
        accumulate(vpad_ref[...], n_pages)
        lam = _lambda_of(lam_ref, lam_init)
        denom = jnp.sum(psum_ref[...], axis=1, keepdims=True)
        rows_per_head = 2 * dec_seq
        for h in range(DA_HEADS):
            r0 = h * rows_per_head
            blk = oacc_ref[r0:r0 + rows_per_head, h * DA_V_DIM:(h + 1) * DA_V_DIM]
            n = blk / denom[r0:r0 + rows_per_head]
            o = n[0:dec_seq] - lam * n[dec_seq:rows_per_head]
            o_ref[:, h * DA_V_DIM:(h + 1) * DA_V_DIM] = _rms_rows(o, sub_ref[...]) * (1.0 - lam_init)


def _attn_sample(proj_s, kt_s, cache_k, cache_v, e, page_table, da_lambda_e, subln_e, DB, dec_seq, lam_init, G=8):
    n_pages = page_table.shape[1]
    n_steps = n_pages // G
    n_rows = DA_HEADS * 2 * dec_seq
    assert n_rows == LANES and n_pages % G == 0 and PAGE_RING % G == 0 and PAGE_RING // G >= 2
    assert DA_W == PAGE_SIZE * DA_HEADS
    n_layers, n_phys = cache_k.shape[:2]
    q = proj_s[:, :DA_W].reshape(DB, dec_seq, DA_HEADS * 2, DA_QK_DIM).transpose(0, 2, 1, 3)
    eye = jnp.eye(DA_HEADS * 2, dtype=F32)
    qbd = (q[:, :, :, None, :] * eye[None, :, None, :, None]).reshape(DB, n_rows, DA_W)
    qbd = (qbd * DA_SCALE).astype(BF16)
    knt = kt_s.reshape(DA_W, DB, dec_seq).transpose(1, 0, 2)
    knt = jnp.pad(knt, ((0, 0), (0, 0), (0, PAGE_SIZE - dec_seq))).astype(BF16)
    ckt = jnp.transpose(cache_k, (0, 1, 3, 4, 5, 2)).reshape(n_layers, n_phys, DA_W, PAGE_SIZE)
    cv = cache_v.reshape(n_layers, n_phys, PAGE_SIZE * DA_HEADS, DA_V_DIM)
    pt = page_table.reshape(-1)

    kern = functools.partial(_attn_sample_kernel, layer=e, G=G, n_steps=n_steps, dec_seq=dec_seq,
                             lam_init=lam_init)
    grid_spec = pltpu.PrefetchScalarGridSpec(
        num_scalar_prefetch=1,
        grid=(DB, 2 * n_steps),
        in_specs=[
            pl.BlockSpec((4, DA_QK_DIM), lambda b, t, pt: (0, 0)),
            pl.BlockSpec((None, n_rows, DA_W), lambda b, t, pt: (b, 0, 0)),
            pl.BlockSpec((None, DA_W, PAGE_SIZE), lambda b, t, pt: (b, 0, 0)),
            pl.BlockSpec((dec_seq, DA_W), lambda b, t, pt: (b, 1)),
            pl.BlockSpec((1, DA_V_DIM), lambda b, t, pt: (0, 0)),
            pl.BlockSpec(memory_space=pl.ANY),
            pl.BlockSpec(memory_space=pl.ANY),
        ],
        out_specs=pl.BlockSpec((dec_seq, DA_W), lambda b, t, pt: (b, 0)),
        scratch_shapes=[
            pltpu.VMEM((PAGE_RING, DA_W, PAGE_SIZE), F32),
            pltpu.SemaphoreType.DMA((PAGE_RING,)),
            pltpu.VMEM((n_pages + 1, n_rows, PAGE_SIZE), F32),
            pltpu.VMEM((n_rows, PAGE_SIZE), F32),
            pltpu.VMEM((n_rows, PAGE_SIZE), F32),
            pltpu.VMEM((n_rows, DA_W), F32),
            pltpu.VMEM((PAGE_SIZE, DA_W), F32),
        ],
    )
    return pl.pallas_call(
        kern,
        grid_spec=grid_spec,
        out_shape=jax.ShapeDtypeStruct((DB * dec_seq, DA_W), F32),
        compiler_params=_cparams("arbitrary", "arbitrary"),
        name="diff_attn_sample",
    )(pt, da_lambda_e, qbd, knt, proj_s, subln_e, ckt, cv)


def _bmm(a, b):
    return jnp.einsum("gmk,gkn->gmn", a.astype(BF16), b.astype(BF16), preferred_element_type=F32)


def _bmm_nt(a, b):
    return jnp.einsum("gmk,gnk->gmn", a.astype(BF16), b.astype(BF16), preferred_element_type=F32)


def _bmm3(a, b):
    ah, al = _split2(a)
    bh, bl = _split2(b)
    d = functools.partial(jnp.einsum, "gmk,gkn->gmn", preferred_element_type=F32)
    return d(ah, bh) + (d(ah, bl) + d(al, bh))


def _tri_inverse(m_strict, C):
    ti = lax.broadcasted_iota(jnp.int32, (1, C, C), 1)
    sj = lax.broadcasted_iota(jnp.int32, (1, C, C), 2)
    eye = (ti == sj).astype(F32)
    a = jnp.where(ti // SUBLANES == sj // SUBLANES, -m_strict, 0.0)
    x = eye + a
    p = _bmm3(a, a)
    x = x + _bmm3(x, p)
    p = _bmm3(p, p)
    x = x + _bmm3(x, p)
    b = SUBLANES
    while b < C:
        join = jnp.logical_and(ti // (2 * b) == sj // (2 * b), ti // b != sj // b)
        c = jnp.where(join, m_strict, 0.0)
        x = x - _bmm3(x, _bmm3(c, x))
        b *= 2
    return x


def _gdn_kernel(*refs, C, NCH, has_state):
    if has_state:
        (q_ref, k_ref, v_ref, z_ref, ab_ref, cw_ref, alog_ref, dtb_ref, nw_ref,
         prev_ref, s0_ref, o_ref, sout_ref, buf_ref, s_ref) = refs
    else:
        (q_ref, k_ref, v_ref, z_ref, ab_ref, cw_ref, alog_ref, dtb_ref, nw_ref,
         o_ref, sout_ref, buf_ref, s_ref) = refs
    ci = pl.program_id(1)
    W = GDN_HEADS * GDN_DK
    H = GDN_HEADS
    R = C * NCH
    G = H * NCH

    @pl.when(ci == 0)
    def _():
        if has_state:
            buf_ref[0:SUBLANES, :] = prev_ref[...]
            s_ref[...] = s0_ref[...]
        else:
            buf_ref[0:SUBLANES, :] = jnp.zeros((SUBLANES, GDN_QKV), F32)
            s_ref[...] = jnp.zeros(s_ref.shape, F32)

    buf_ref[SUBLANES:SUBLANES + R, 0:W] = q_ref[...]
    buf_ref[SUBLANES:SUBLANES + R, W:2 * W] = k_ref[...]
    buf_ref[SUBLANES:SUBLANES + R, 2 * W:3 * W] = v_ref[...]
    y = buf_ref[SUBLANES:SUBLANES + R, :] * cw_ref[GDN_CONV - 1:GDN_CONV, :]
    for back in range(1, GDN_CONV):
        y = y + buf_ref[pl.ds(SUBLANES - back, R), :] * cw_ref[GDN_CONV - 1 - back:GDN_CONV - back, :]
    tail = buf_ref[R:R + SUBLANES, :]
    buf_ref[0:SUBLANES, :] = tail
    y = _silu(y)

    def heads(x, off, width):
        return jnp.stack([x[:, off + h * width:off + (h + 1) * width] for h in range(H)]).reshape(G, C, width)

    def chunk(x, c):
        return x.reshape((H, NCH) + x.shape[1:])[:, c]

    ab = ab_ref[...]
    sp = ab + dtb_ref[...]
    softplus = jnp.maximum(sp, 0.0) + jnp.log(1.0 + jnp.exp(-jnp.abs(sp)))
    g_all = -jnp.exp(alog_ref[...]) * softplus
    beta_all = _sigmoid(ab)

    ti = lax.broadcasted_iota(jnp.int32, (1, C, C), 1)
    sj = lax.broadcasted_iota(jnp.int32, (1, C, C), 2)
    causal = sj <= ti
    strict = sj < ti
    eye = ti == sj
    tri = causal[0].astype(BF16)
    bc_all = jnp.concatenate([_cumsum_rows(tri, g_all[c * C:(c + 1) * C]) for c in range(NCH)], axis=0)

    q3 = heads(y, 0, GDN_DK)
    k3 = heads(y, W, GDN_DK)
    v3 = heads(y, 2 * W, GDN_DV)
    q3 = q3 * lax.rsqrt(jnp.sum(q3 * q3, axis=-1, keepdims=True) + NORM_EPS) * (GDN_DK ** -0.5)
    k3 = k3 * lax.rsqrt(jnp.sum(k3 * k3, axis=-1, keepdims=True) + NORM_EPS)
    bcol = heads(bc_all, 0, 1)
    beta = heads(beta_all, H, 1)
    brow = jnp.sum(jnp.where(eye, bcol, 0.0), axis=1, keepdims=True)
    dec = jnp.exp(jnp.where(causal, bcol - brow, NEG_BIG))
    kb = k3 * beta
    eb = jnp.exp(bcol)
    m_strict = jnp.where(strict, _bmm_nt(kb, k3) * dec, 0.0)
    tinv = _tri_inverse(m_strict, C)
    uw = _bmm(tinv, jnp.concatenate([v3 * beta, kb * eb], axis=2))
    attn = _bmm_nt(q3, k3) * dec
    blast = bcol[:, C - 1:C, :]
    kd = k3 * jnp.exp(blast - bcol)
    gl = jnp.exp(blast)
    wq = jnp.concatenate([uw[:, :, GDN_DV:], q3 * eb], axis=1)
    z3 = heads(z_ref[...], 0, GDN_DV)

    s = s_ref[...]
    for c in range(NCH):
        ws = _bmm(chunk(wq, c), s)
        v_new = chunk(uw, c)[:, :, :GDN_DV] - ws[:, :C]
        o = ws[:, C:] + _bmm(chunk(attn, c), v_new)
        s = s * chunk(gl, c) + _bmm(jnp.swapaxes(chunk(kd, c), 1, 2), v_new)
        o = _rms_rows(o, nw_ref[...]) * _silu(chunk(z3, c))
        for h in range(H):
            o_ref[c * C:(c + 1) * C, h * GDN_DV:(h + 1) * GDN_DV] = o[h].astype(o_ref.dtype)
    s_ref[...] = s

    @pl.when(ci == pl.num_programs(1) - 1)
    def _():
        sout_ref[...] = s


def _gdn(proj, ab, conv_w_e, alog_pad, dtb_pad, norm_e, B, L, C, NCH=1, prev8=None, s0=None, out_dtype=F32):
    has_state = prev8 is not None
    R = C * NCH
    nc = L // R
    W = GDN_HEADS * GDN_DK
    base = (2 * DA_W) // W
    row = lambda b, c: (b * nc + c, 0)
    in_specs = [
        pl.BlockSpec((R, W), lambda b, c: (b * nc + c, base)),
        pl.BlockSpec((R, W), lambda b, c: (b * nc + c, base + 1)),
        pl.BlockSpec((R, W), lambda b, c: (b * nc + c, base + 2)),
        pl.BlockSpec((R, W), lambda b, c: (b * nc + c, base + 3)),
        pl.BlockSpec((R, LANES), row),
        pl.BlockSpec((GDN_CONV, GDN_QKV), lambda b, c: (0, 0)),
        pl.BlockSpec((1, LANES), lambda b, c: (0, 0)),
        pl.BlockSpec((1, LANES), lambda b, c: (0, 0)),
        pl.BlockSpec((1, GDN_DV), lambda b, c: (0, 0)),
    ]
    args = [proj, proj, proj, proj, ab, conv_w_e, alog_pad, dtb_pad, norm_e]
    state_block = (None, GDN_HEADS, GDN_DK, GDN_DV)
    if has_state:
        in_specs += [pl.BlockSpec((None, SUBLANES, GDN_QKV), lambda b, c: (b, 0, 0)),
                     pl.BlockSpec(state_block, lambda b, c: (b, 0, 0, 0))]
        args += [prev8, s0]
    return pl.pallas_call(
        functools.partial(_gdn_kernel, C=C, NCH=NCH, has_state=has_state),
        grid=(B, nc),
        in_specs=in_specs,
        out_specs=[pl.BlockSpec((R, W), row),
                   pl.BlockSpec(state_block, lambda b, c: (b, 0, 0, 0))],
        out_shape=[jax.ShapeDtypeStruct((B * L, W), out_dtype),
                   jax.ShapeDtypeStruct((B, GDN_HEADS, GDN_DK, GDN_DV), F32)],
        scratch_shapes=[pltpu.VMEM((SUBLANES + R, GDN_QKV), F32),
                        pltpu.VMEM((GDN_HEADS, GDN_DK, GDN_DV), F32)],
        compiler_params=_cparams("parallel", "arbitrary"),
        name="gdn_state" if has_state else "gdn_fresh",
    )(*args)


def _sub_block_rows(x4, b):
    sub = lax.broadcasted_iota(jnp.int32, (1, 1, SUBLANES, 1), 2)
    out = None
    for start in range(0, SUBLANES, b):
        mid = start + b // 2
        piece = jnp.broadcast_to(x4[:, :, mid:mid + 1, :], x4.shape)
        out = piece if out is None else jnp.where(sub >= start, piece, out)
    return out


def _hgrn_kernel(*refs, C, HB, SB, layer, has_state):
    if has_state:
        (q_ref, f_ref, i_ref, g_ref, lower_ref, nw_ref, s0_ref, o_ref, sout_ref, st_ref) = refs
    else:
        (q_ref, f_ref, i_ref, g_ref, lower_ref, nw_ref, o_ref, sout_ref, st_ref) = refs
    ci = pl.program_id(2)
    G = HB * SB
    K = HG_EXPAND

    @pl.when(ci == 0)
    def _():
        for h in range(HB):
            for sb in range(SB):
                st_ref[h * SB + sb] = s0_ref[sb, h].T if has_state else jnp.zeros((HG_DI, K), F32)

    def heads(x):
        return jnp.stack([x[:, h * K:(h + 1) * K] for h in range(HB)]).reshape(G, C, K)

    low = lower_ref[...]
    ex = jnp.exp(low - jnp.max(low, axis=0, keepdims=True))
    lb = jnp.sum(ex[1:layer + 1], axis=0, keepdims=True) / jnp.sum(ex, axis=0, keepdims=True)

    fg2 = lb + (1.0 - lb) * _sigmoid(f_ref[...])
    logf2 = jnp.log(fg2)
    ti = lax.broadcasted_iota(jnp.int32, (1, C, C), 1)
    sj = lax.broadcasted_iota(jnp.int32, (1, C, C), 2)
    tri = (sj[0] <= ti[0]).astype(BF16)
    bc2 = jnp.concatenate([_cumsum_rows(tri, logf2[sb * C:(sb + 1) * C]) for sb in range(SB)], axis=0)

    qt = heads(_silu(q_ref[...]) * (K ** -0.5))
    kk = heads(1.0 - fg2)
    v = heads(i_ref[...])
    bc = heads(bc2)
    st = st_ref[...]
    o = _bmm_nt(qt * jnp.exp(bc), st)

    ri = lax.broadcasted_iota(jnp.int32, (1, C, 1), 1)
    a = jnp.where(ti == sj, jnp.sum(qt * kk, axis=-1, keepdims=True), 0.0)
    bc4 = bc.reshape(G, C // SUBLANES, SUBLANES, K)
    b = 2
    while b <= C:
        if b <= SUBLANES:
            ref = _sub_block_rows(bc4, b).reshape(G, C, K)
        else:
            ref = jnp.concatenate(
                [jnp.broadcast_to(bc[:, m:m + 1, :], (G, b, K)) for m in range(b // 2, C, b)], axis=1)
        upper = (ri % b) >= (b // 2)
        d = bc - ref
        e = jnp.exp(jnp.where(upper, d, -d))
        qs = jnp.where(upper, qt * e, 0.0)
        ks = jnp.where(upper, 0.0, kk * e)
        a = a + jnp.where(ti // b == sj // b, _bmm_nt(qs, ks), 0.0)
        b *= 2
    o = o + _bmm(a, v)

    blast = bc[:, C - 1:C, :]
    st = st * jnp.exp(blast) + _bmm(jnp.swapaxes(v, 1, 2), kk * jnp.exp(blast - bc))
    st_ref[...] = st
    nw = jnp.stack([nw_ref[:, h * K:(h + 1) * K] for h in range(HB) for _ in range(SB)])
    o = _rms_rows(o, nw) * heads(_silu(g_ref[...]))
    for h in range(HB):
        for sb in range(SB):
            o_ref[sb * C:(sb + 1) * C, h * K:(h + 1) * K] = o[h * SB + sb].astype(o_ref.dtype)

    @pl.when(ci == pl.num_programs(2) - 1)
    def _():
        for h in range(HB):
            for sb in range(SB):
                sout_ref[sb, h] = st[h * SB + sb].T


def _hgrn(proj, hg_lower, hg_norm_o, layer, B, L, C, s0=None, HB=8, SB=1):
    has_state = s0 is not None
    nc = L // C
    assert SB == 1 or nc == 1
    nhg = HG_HEADS // HB
    Wb = HB * HG_EXPAND
    R = SB * C

    def col(section):
        return lambda b, hg, c: (b * nc + c, section * nhg + hg)

    in_specs = [pl.BlockSpec((R, Wb), col(s)) for s in range(4)] + [
        pl.BlockSpec((DEPTH, Wb), lambda b, hg, c: (0, hg)),
        pl.BlockSpec((1, Wb), lambda b, hg, c: (0, hg)),
    ]
    args = [proj, proj, proj, proj, hg_lower, hg_norm_o]
    state_block = (SB, HB, HG_EXPAND, HG_DI)
    if has_state:
        in_specs.append(pl.BlockSpec(state_block, lambda b, hg, c: (b, hg, 0, 0)))
        args.append(s0)
    return pl.pallas_call(
        functools.partial(_hgrn_kernel, C=C, HB=HB, SB=SB, layer=layer, has_state=has_state),
        grid=(B // SB, nhg, nc),
        in_specs=in_specs,
        out_specs=[pl.BlockSpec((R, Wb), col(0)),
                   pl.BlockSpec(state_block, lambda b, hg, c: (b, hg, 0, 0))],
        out_shape=[jax.ShapeDtypeStruct((B * L, D_MODEL), BF16),
                   jax.ShapeDtypeStruct((B, HG_HEADS, HG_EXPAND, HG_DI), F32)],
        scratch_shapes=[pltpu.VMEM((HB * SB, HG_DI, HG_EXPAND), F32)],
        compiler_params=_cparams("parallel", "parallel", "arbitrary"),
        name="hgrn_state" if has_state else "hgrn_fresh",
    )(*args)


def _pad_lanes(v, offset=0):
    return jnp.zeros((1, LANES), F32).at[0, offset:offset + v.shape[0]].set(v.astype(F32))


def kernel(x_prompt, x_sample, cache_k, cache_v, state_gdn_conv, state_gdn, state_hgrn, page_table, norm_mix, norm_ffn, norm_final, w_in_even, w_out_even, da_lambda, da_subln, gdn_conv_w, gdn_a_log, gdn_dt_bias, gdn_norm, w_in_odd, w_out_odd, hg_lower, hg_norm, w_gate_up, w_down):
    Bp, Lp = x_prompt.shape[:2]
    DB, Ls = x_sample.shape[:2]
    past_len = page_table.shape[1] * PAGE_SIZE
    pos_p = np.tile(np.arange(Lp), Bp)
    pos_s = np.tile(past_len + np.arange(Ls), DB)
    Mp, Ms = Bp * Lp, DB * Ls
    tm_p, tm_s = 1024, Ms

    hp = x_prompt.reshape(Mp, D_MODEL)
    hs = x_sample.reshape(Ms, D_MODEL)
    outs = {k: [] for k in ("k_p", "v_p", "conv_p", "gdn_p", "hg_p", "k_s", "v_s", "conv_s", "gdn_s", "hg_s")}

    for l in range(DEPTH):
        nw_mix = norm_mix[l].reshape(1, D_MODEL)
        nw_ffn = norm_ffn[l].reshape(1, D_MODEL)
        if l % 2 == 0:
            e = l // 2
            lam_init = 0.8 - 0.6 * math.exp(-0.3 * l)
            w_in_t = jnp.transpose(w_in_even, (0, 2, 1))
            wab = jnp.pad(w_in_t[e, EVEN_MAIN:], ((0, LANES - 2 * GDN_HEADS), (0, 0)))
            alog_pad = _pad_lanes(gdn_a_log[e])
            dtb_pad = _pad_lanes(gdn_dt_bias[e])
            sub = da_subln[e].reshape(1, DA_V_DIM)
            gnorm = gdn_norm[e].reshape(1, GDN_DV)
            conv_cols = slice(2 * DA_W, 2 * DA_W + GDN_QKV)
            proj_p, kt_p, ab_p, proj_s, kt_s, ab_s = _inproj_even(
                hp, hs, nw_mix, w_in_t, e, wab, pos_p, pos_s, Bp, tm_p)
            oa_p = _attn_prompt(proj_p, kt_p, da_lambda[e], sub, Bp, Lp, lam_init)
            gdn_c = min(GDN_CHUNK, Lp)
            ob_p, st = _gdn(proj_p, ab_p, gdn_conv_w[e], alog_pad, dtb_pad, gnorm, Bp, Lp, gdn_c,
                            NCH=2 if Lp % (2 * gdn_c) == 0 else 1, out_dtype=BF16)
            outs["k_p"].append(kt_p.reshape(Bp, DA_HEADS, 2, DA_QK_DIM, Lp).transpose(0, 4, 1, 2, 3))
            outs["v_p"].append(proj_p[:, DA_W:2 * DA_W].reshape(Bp, Lp, DA_HEADS, DA_V_DIM))
            outs["conv_p"].append(proj_p.reshape(Bp, Lp, EVEN_ROWS)[:, Lp - (GDN_CONV - 1):, conv_cols])
            outs["gdn_p"].append(st)
            oa_s = _attn_sample(proj_s, kt_s[0], cache_k, cache_v, e, page_table, da_lambda[e], sub,
                                DB, Ls, lam_init)
            prev8 = jnp.pad(state_gdn_conv[e], ((0, 0), (SUBLANES - (GDN_CONV - 1), 0), (0, 0)))
            ob_s, st = _gdn(proj_s, ab_s, gdn_conv_w[e], alog_pad, dtb_pad, gnorm, DB, Ls, min(GDN_CHUNK, Ls),
                            prev8=prev8, s0=state_gdn[e])
            gq = proj_s[:, conv_cols].reshape(DB, Ls, GDN_QKV)
            conv_s = jnp.concatenate([state_gdn_conv[e], gq], axis=1)[:, Ls:]
            outs["k_s"].append(kt_s[0].reshape(DA_HEADS, 2, DA_QK_DIM, DB, Ls).transpose(3, 4, 0, 1, 2))
            outs["v_s"].append(proj_s[:, DA_W:2 * DA_W].reshape(DB, Ls, DA_HEADS, DA_V_DIM))
            outs["conv_s"].append(conv_s)
            outs["gdn_s"].append(st)
            hp, hs = _mm_res([oa_p, ob_p], [oa_s, ob_s], w_out_even, e, hp, hs, tm_p)
        else:
            o = l // 2
            hnorm = hg_norm[o].reshape(1, D_MODEL)
            proj_p, proj_s = _norm_matmul(hp, hs, nw_mix, w_in_odd, o, tm_p)
            y_p, st = _hgrn(proj_p, hg_lower, hnorm, l, Bp, Lp, min(HG_CHUNK, Lp))
            outs["hg_p"].append(st)
            y_s, st = _hgrn(proj_s, hg_lower, hnorm, l, DB, Ls, min(HG_CHUNK, Ls), s0=state_hgrn[o],
                            SB=4 if Ls <= HG_CHUNK and DB % 4 == 0 else 1)
            outs["hg_s"].append(st)
            hp, hs = _mm_res([y_p], [y_s], w_out_odd, o, hp, hs, tm_p)
        act_p, act_s = _ffn_up(hp, hs, nw_ffn, w_gate_up, l, tm_p)
        hp, hs = _mm_res([act_p], [act_s], w_down, l, hp, hs, tm_p, tk=D_FF // 2)

    nwf = norm_final.reshape(1, D_MODEL)
    y_prompt = _final_norm(hp, nwf, tm_p).reshape(Bp, Lp, D_MODEL)
    y_sample = _final_norm(hs, nwf, tm_s).reshape(DB, Ls, D_MODEL)
    st = lambda k: jnp.stack(outs[k])
    return (y_prompt, y_sample, st("k_p"), st("v_p"), st("conv_p"), st("gdn_p"), st("hg_p"),
            st("k_s"), st("v_s"), st("conv_s"), st("gdn_s"), st("hg_s"))
```

```python
import functools
import math

import numpy as np
import jax
import jax.numpy as jnp
from jax import lax
from jax.experimental import pallas as pl
from jax.experimental.pallas import tpu as pltpu

F32 = jnp.float32
BF16 = jnp.bfloat16

D_MODEL = 2048
DEPTH = 2
PAGE_SIZE = 128
NORM_EPS = 1e-6

DA_HEADS = 8
DA_QK_DIM = 64
DA_V_DIM = 2 * DA_QK_DIM
DA_SCALE = DA_QK_DIM ** -0.5
LOG2_E = 1.4426950408889634
ROT_DIM = DA_QK_DIM // 4
ROPE_THETA = 500000.0

GDN_HEADS = 8
GDN_DK = 128
GDN_DV = 128
GDN_QKV = GDN_HEADS * (2 * GDN_DK + GDN_DV)
GDN_CONV = 4
GDN_CHUNK = 64

HG_EXPAND = 128
HG_HEADS = D_MODEL // HG_EXPAND
HG_DI = D_MODEL // HG_HEADS
HG_CHUNK = 64

D_FF = -(-(8 * D_MODEL) // (3 * 256)) * 256

DA_W = DA_HEADS * 2 * DA_QK_DIM
EVEN_MAIN = 3 * DA_W + GDN_QKV + GDN_HEADS * GDN_DV
EVEN_ROWS = EVEN_MAIN - DA_W

LANES = 128
SUBLANES = 8
VMEM_LIMIT = 56 * 1024 * 1024
NEG_BIG = -1e30


def _cparams(*sem):
    return pltpu.CompilerParams(dimension_semantics=sem, vmem_limit_bytes=VMEM_LIMIT)


def _split2(x):
    hi = x.astype(BF16)
    lo = (x - hi.astype(F32)).astype(BF16)
    return hi, lo


def _cumsum_rows(tri, x):
    hi = x.astype(BF16)
    r = x - hi.astype(F32)
    mid = r.astype(BF16)
    lo = (r - mid.astype(F32)).astype(BF16)
    d = functools.partial(jnp.dot, preferred_element_type=F32)
    return d(tri, hi) + (d(tri, mid) + d(tri, lo))


def _sigmoid(x):
    return 0.5 * jnp.tanh(0.5 * x) + 0.5


def _silu(x):
    return x * _sigmoid(x)


def _rms_rows(x, w):
    return x * lax.rsqrt(jnp.mean(x * x, axis=-1, keepdims=True) + NORM_EPS) * w


W_SLABS = 4


def _slab_specs(k_rows, tn, index, transposed=False):
    def spec(s):
        def index_map(*g):
            layer, kt, col = index(*g)
            return (layer, col * W_SLABS + s, kt) if transposed else (layer, kt * W_SLABS + s, col)
        block = (None, tn // W_SLABS, k_rows) if transposed else (None, k_rows // W_SLABS, tn)
        return pl.BlockSpec(block, index_map)
    return [spec(s) for s in range(W_SLABS)]


def _dot_nt(a, b):
    return lax.dot_general(a, b, (((1,), (1,)), ((), ())), preferred_element_type=F32)


def _load_weight(w_refs, wbuf_ref):
    rows = wbuf_ref.shape[0] // len(w_refs)
    for s, r in enumerate(w_refs):
        wbuf_ref[s * rows:(s + 1) * rows, :] = r[...].astype(BF16)
    return wbuf_ref[...]


def _small_col(i, j, n_i):
    return jnp.where(i == n_i - 1, j, 0)


def _inproj_even_kernel(xp_ref, xs_ref, nw_ref, *rest, n_i, n_q_tiles, n_k_tiles, tn):
    w_refs = rest[:W_SLABS]
    (wab_ref, lane_p0, lane_p1, sub_p0, sub_p1, lane_s0, lane_s1, sub_s0, sub_s1,
     op_ref, ktp_ref, abp_ref, os_ref, kts_ref, abs_ref,
     xnp_ref, xns_ref, wbuf_ref) = rest[W_SLABS:]
    i = pl.program_id(0)
    j = pl.program_id(1)
    last = i == n_i - 1
    half = ROT_DIM // 2

    def prologue(x_ref, xn_ref, ab_ref):
        xn = _rms_rows(x_ref[...], nw_ref[...]).astype(BF16)
        xn_ref[...] = xn
        ab_ref[...] = _dot_nt(xn, wab_ref[...].astype(BF16))

    @pl.when(j == 0)
    def _():
        prologue(xp_ref, xnp_ref, abp_ref)

    @pl.when(jnp.logical_and(last, j == 0))
    def _():
        prologue(xs_ref, xns_ref, abs_ref)

    w = _load_weight(w_refs, wbuf_ref)

    def rotate(a, cosf, sinf, axis):
        ch = lax.broadcasted_iota(jnp.int32, a.shape, axis) % DA_QK_DIM
        partner = jnp.where(ch < half, -pltpu.roll(a, LANES - half, axis), pltpu.roll(a, half, axis))
        return a * cosf + partner * sinf

    def q_tile(xn_ref, o_ref, tabs):
        acc = _dot_nt(xn_ref[...], w)
        cosf, sinf = (t[...] for t in tabs)
        for c in range(tn // LANES):
            o_ref[:, c * LANES:(c + 1) * LANES] = rotate(acc[:, c * LANES:(c + 1) * LANES], cosf, sinf, 1)

    def k_tile(xn_ref, kt_ref, tabs):
        acc = _dot_nt(w, xn_ref[...])
        cosf, sinf = (t[...] for t in tabs)
        for c in range(tn // LANES):
            kt_ref[c * LANES:(c + 1) * LANES, :] = rotate(acc[c * LANES:(c + 1) * LANES, :], cosf, sinf, 0)

    def plain_tile(xn_ref, o_ref):
        o_ref[...] = _dot_nt(xn_ref[...], w)

    is_q = j < n_q_tiles
    is_k = jnp.logical_and(j >= n_q_tiles, j < n_q_tiles + n_k_tiles)
    is_plain = j >= n_q_tiles + n_k_tiles

    @pl.when(is_q)
    def _():
        q_tile(xnp_ref, op_ref, (lane_p0, lane_p1))

    @pl.when(is_k)
    def _():
        k_tile(xnp_ref, ktp_ref, (sub_p0, sub_p1))

    @pl.when(is_plain)
    def _():
        plain_tile(xnp_ref, op_ref)

    @pl.when(jnp.logical_and(last, is_q))
    def _():
        q_tile(xns_ref, os_ref, (lane_s0, lane_s1))

    @pl.when(jnp.logical_and(last, is_k))
    def _():
        k_tile(xns_ref, kts_ref, (sub_s0, sub_s1))

    @pl.when(jnp.logical_and(last, is_plain))
    def _():
        plain_tile(xns_ref, os_ref)


def _rope_tables(pos):
    half = ROT_DIM // 2
    inv = np.power(np.float32(ROPE_THETA), -np.arange(half, dtype=np.float32) * np.float32(2.0) / np.float32(ROT_DIM))
    ang = (pos.astype(np.float32)[:, None] * inv[None, :]).astype(np.float32)
    cos = np.cos(ang.astype(np.float64)).astype(np.float32)
    sin = np.sin(ang.astype(np.float64)).astype(np.float32)
    n = pos.shape[0]
    cosf = np.ones((n, LANES), np.float32)
    sinf = np.zeros((n, LANES), np.float32)
    for base in range(0, LANES, DA_QK_DIM):
        for off in (0, half):
            cosf[:, base + off:base + off + half] = cos
            sinf[:, base + off:base + off + half] = sin
    return cosf, sinf


def _inproj_even(xp, xs, nw, w_in_t, e, wab, pos_p, pos_s, B, tm, tn=512):
    M = xp.shape[0]
    Ms = xs.shape[0]
    L = M // B
    lt = L // tm
    n_i = M // tm
    tabs_p = _rope_tables(pos_p)
    tabs_s = _rope_tables(pos_s)
    lane = lambda tabs: [jnp.asarray(t) for t in tabs]
    sub = lambda tabs: [jnp.asarray(np.ascontiguousarray(t.T)) for t in tabs]
    n_q, n_k = DA_W // tn, DA_W // tn
    n_tiles = EVEN_MAIN // tn
    kern = functools.partial(_inproj_even_kernel, n_i=n_i, n_q_tiles=n_q, n_k_tiles=n_k, tn=tn)
    row = lambda i, j: (i, 0)
    col = lambda i, j: (0, i)
    fixed = lambda i, j: (0, 0)
    once = pl.Buffered(1)
    row_col = lambda j: jnp.where(j < n_q, j, jnp.maximum(j - n_k, n_q - 1))
    kt_col = lambda j: jnp.clip(j - n_q, 0, n_k - 1)
    return pl.pallas_call(
        kern,
        grid=(n_i, n_tiles),
        in_specs=[
            pl.BlockSpec((tm, D_MODEL), row),
            pl.BlockSpec((Ms, D_MODEL), fixed, pipeline_mode=once),
            pl.BlockSpec((1, D_MODEL), fixed, pipeline_mode=once),
        ] + _slab_specs(D_MODEL, tn, lambda i, j: (e, 0, j), transposed=True) + [
            pl.BlockSpec((LANES, D_MODEL), fixed, pipeline_mode=once),
        ] + [pl.BlockSpec((tm, LANES), row, pipeline_mode=once)] * 2
          + [pl.BlockSpec((LANES, tm), col, pipeline_mode=once)] * 2
          + [pl.BlockSpec((Ms, LANES), fixed, pipeline_mode=once)] * 2
          + [pl.BlockSpec((LANES, Ms), fixed, pipeline_mode=once)] * 2,
        out_specs=[
            pl.BlockSpec((tm, tn), lambda i, j: (i, row_col(j))),
            pl.BlockSpec((None, tn, tm), lambda i, j: (i // lt, kt_col(j), i % lt)),
            pl.BlockSpec((tm, LANES), row),
            pl.BlockSpec((Ms, tn), lambda i, j: (0, _small_col(i, row_col(j), n_i))),
            pl.BlockSpec((None, tn, Ms), lambda i, j: (0, _small_col(i, kt_col(j), n_i), 0)),
            pl.BlockSpec((Ms, LANES), fixed),
        ],
        out_shape=[jax.ShapeDtypeStruct((M, EVEN_ROWS), F32),
                   jax.ShapeDtypeStruct((B, DA_W, L), F32),
                   jax.ShapeDtypeStruct((M, LANES), F32),
                   jax.ShapeDtypeStruct((Ms, EVEN_ROWS), F32),
                   jax.ShapeDtypeStruct((1, DA_W, Ms), F32),
                   jax.ShapeDtypeStruct((Ms, LANES), F32)],
        scratch_shapes=[pltpu.VMEM((tm, D_MODEL), BF16), pltpu.VMEM((Ms, D_MODEL), BF16),
                        pltpu.VMEM((tn, D_MODEL), BF16)],
        compiler_params=_cparams("arbitrary", "arbitrary"),
        name="inproj_even",
    )(xp, xs, nw, *([w_in_t] * W_SLABS), wab, *lane(tabs_p), *sub(tabs_p), *lane(tabs_s), *sub(tabs_s))


def _norm_matmul_kernel(xp_ref, xs_ref, nw_ref, *rest, n_i):
    w_refs = rest[:W_SLABS]
    op_ref, os_ref, xnp_ref, xns_ref, wbuf_ref = rest[W_SLABS:]
    i = pl.program_id(0)
    j = pl.program_id(1)
    last = i == n_i - 1

    @pl.when(j == 0)
    def _():
        xnp_ref[...] = _rms_rows(xp_ref[...], nw_ref[...]).astype(BF16)

    @pl.when(jnp.logical_and(last, j == 0))
    def _():
        xns_ref[...] = _rms_rows(xs_ref[...], nw_ref[...]).astype(BF16)

    w = _load_weight(w_refs, wbuf_ref)
    op_ref[...] = jnp.dot(xnp_ref[...], w, preferred_element_type=F32)

    @pl.when(last)
    def _():
        os_ref[...] = jnp.dot(xns_ref[...], w, preferred_element_type=F32)


def _norm_matmul(xp, xs, nw, w_all, l, tm, tn=512):
    M = xp.shape[0]
    Ms = xs.shape[0]
    N = w_all.shape[-1]
    n_i = M // tm
    return pl.pallas_call(
        functools.partial(_norm_matmul_kernel, n_i=n_i),
        grid=(n_i, N // tn),
        in_specs=[
            pl.BlockSpec((tm, D_MODEL), lambda i, j: (i, 0)),
            pl.BlockSpec((Ms, D_MODEL), lambda i, j: (0, 0)),
            pl.BlockSpec((1, D_MODEL), lambda i, j: (0, 0)),
        ] + _slab_specs(D_MODEL, tn, lambda i, j: (l, 0, j)),
        out_specs=[pl.BlockSpec((tm, tn), lambda i, j: (i, j)),
                   pl.BlockSpec((Ms, tn), lambda i, j: (0, _small_col(i, j, n_i)))],
        out_shape=[jax.ShapeDtypeStruct((M, N), F32), jax.ShapeDtypeStruct((Ms, N), F32)],
        scratch_shapes=[pltpu.VMEM((tm, D_MODEL), BF16), pltpu.VMEM((Ms, D_MODEL), BF16),
                        pltpu.VMEM((D_MODEL, tn), BF16)],
        compiler_params=_cparams("arbitrary", "arbitrary"),
        name="norm_matmul",
    )(xp, xs, nw, *([w_all] * W_SLABS))


def _ffn_up_kernel(xp_ref, xs_ref, nw_ref, *rest, n_i):
    wg_refs = rest[:W_SLABS]
    wu_refs = rest[W_SLABS:2 * W_SLABS]
    op_ref, os_ref, xnp_ref, xns_ref, wgbuf_ref, wubuf_ref = rest[2 * W_SLABS:]
    i = pl.program_id(0)
    j = pl.program_id(1)
    last = i == n_i - 1

    @pl.when(j == 0)
    def _():
        xnp_ref[...] = _rms_rows(xp_ref[...], nw_ref[...]).astype(BF16)

    @pl.when(jnp.logical_and(last, j == 0))
    def _():
        xns_ref[...] = _rms_rows(xs_ref[...], nw_ref[...]).astype(BF16)

    wg = _load_weight(wg_refs, wgbuf_ref)
    wu = _load_weight(wu_refs, wubuf_ref)

    def tile(xn_ref, o_ref):
        xn = xn_ref[...]
        g = jnp.dot(xn, wg, preferred_element_type=F32)
        u = jnp.dot(xn, wu, preferred_element_type=F32)
        o_ref[...] = (_silu(g) * u).astype(BF16)

    tile(xnp_ref, op_ref)

    @pl.when(last)
    def _():
        tile(xns_ref, os_ref)


def _ffn_up(xp, xs, nw, w_gate_up, l, tm, tn=512):
    M = xp.shape[0]
    Ms = xs.shape[0]
    nj = D_FF // tn
    n_i = M // tm
    return pl.pallas_call(
        functools.partial(_ffn_up_kernel, n_i=n_i),
        grid=(n_i, nj),
        in_specs=[
            pl.BlockSpec((tm, D_MODEL), lambda i, j: (i, 0)),
            pl.BlockSpec((Ms, D_MODEL), lambda i, j: (0, 0), pipeline_mode=pl.Buffered(1)),
            pl.BlockSpec((1, D_MODEL), lambda i, j: (0, 0), pipeline_mode=pl.Buffered(1)),
        ] + _slab_specs(D_MODEL, tn, lambda i, j: (l, 0, j))
          + _slab_specs(D_MODEL, tn, lambda i, j: (l, 0, j + nj)),
        out_specs=[pl.BlockSpec((tm, tn), lambda i, j: (i, j)),
                   pl.BlockSpec((Ms, tn), lambda i, j: (0, _small_col(i, j, n_i)))],
        out_shape=[jax.ShapeDtypeStruct((M, D_FF), BF16), jax.ShapeDtypeStruct((Ms, D_FF), BF16)],
        scratch_shapes=[pltpu.VMEM((tm, D_MODEL), BF16), pltpu.VMEM((Ms, D_MODEL), BF16),
                        pltpu.VMEM((D_MODEL, tn), BF16), pltpu.VMEM((D_MODEL, tn), BF16)],
        compiler_params=_cparams("arbitrary", "arbitrary"),
        name="ffn_up",
    )(xp, xs, nw, *([w_gate_up] * (2 * W_SLABS)))


def _mm_res_kernel(*refs, nk, na, n_i):
    ap_refs = refs[:na]
    as_refs = refs[na:2 * na]
    w_refs = refs[2 * na:2 * na + W_SLABS]
    (rp_ref, rs_ref, op_ref, os_ref,
     accp_ref, accs_ref, wbuf_ref, abufp_ref, abufs_ref) = refs[2 * na + W_SLABS:]
    i = pl.program_id(0)
    k = pl.program_id(2)
    last = i == n_i - 1
    w = _load_weight(w_refs, wbuf_ref)

    def tile(a_refs, abuf_ref, r_ref, o_ref, acc_ref):
        if na == 1:
            a = a_refs[0][...].astype(BF16)
        else:
            off = 0
            for r in a_refs:
                abuf_ref[:, off:off + r.shape[1]] = r[...].astype(BF16)
                off += r.shape[1]
            a = abuf_ref[...]
        p = jnp.dot(a, w, preferred_element_type=F32)
        if nk == 1:
            o_ref[...] = r_ref[...] + p
        else:
            @pl.when(k == 0)
            def _():
                acc_ref[...] = p

            @pl.when(jnp.logical_and(k > 0, k < nk - 1))
            def _():
                acc_ref[...] += p

            @pl.when(k == nk - 1)
            def _():
                o_ref[...] = r_ref[...] + (acc_ref[...] + p)

    tile(ap_refs, abufp_ref, rp_ref, op_ref, accp_ref)

    @pl.when(last)
    def _():
        tile(as_refs, abufs_ref, rs_ref, os_ref, accs_ref)


def _mm_res(ap_list, as_list, w_all, l, res_p, res_s, tm, tn=512, tk=None):
    M = res_p.shape[0]
    Ms = res_s.shape[0]
    K = sum(a.shape[1] for a in ap_list)
    N = w_all.shape[-1]
    tk = K if tk is None else tk
    nk = K // tk
    na = len(ap_list)
    n_i = M // tm
    assert na == 1 or nk == 1
    small = lambda i, j, k: (0, _small_col(i, j, n_i))
    if na == 1:
        a_specs = [pl.BlockSpec((tm, tk), lambda i, j, k: (i, k)),
                   pl.BlockSpec((Ms, tk), lambda i, j, k: (0, jnp.where(i == n_i - 1, k, 0)))]
    else:
        a_specs = ([pl.BlockSpec((tm, a.shape[1]), lambda i, j, k: (i, 0)) for a in ap_list]
                   + [pl.BlockSpec((Ms, a.shape[1]), lambda i, j, k: (0, 0)) for a in as_list])
    abuf = lambda rows: pltpu.VMEM((rows, tk) if na > 1 else (SUBLANES * 2, LANES), BF16)
    return pl.pallas_call(
        functools.partial(_mm_res_kernel, nk=nk, na=na, n_i=n_i),
        grid=(n_i, N // tn, nk),
        in_specs=a_specs + _slab_specs(tk, tn, lambda i, j, k: (l, k, j))
                 + [pl.BlockSpec((tm, tn), lambda i, j, k: (i, j)), pl.BlockSpec((Ms, tn), small)],
        out_specs=[pl.BlockSpec((tm, tn), lambda i, j, k: (i, j)), pl.BlockSpec((Ms, tn), small)],
        out_shape=[jax.ShapeDtypeStruct((M, N), F32), jax.ShapeDtypeStruct((Ms, N), F32)],
        scratch_shapes=[pltpu.VMEM((tm, tn), F32), pltpu.VMEM((Ms, tn), F32), pltpu.VMEM((tk, tn), BF16),
                        abuf(tm), abuf(Ms)],
        compiler_params=_cparams("arbitrary", "arbitrary", "arbitrary"),
        name="matmul_residual",
    )(*ap_list, *as_list, *([w_all] * W_SLABS), res_p, res_s)


def _final_norm_kernel(x_ref, w_ref, o_ref):
    o_ref[...] = _rms_rows(x_ref[...], w_ref[...])


def _final_norm(x, w, tm):
    M = x.shape[0]
    return pl.pallas_call(
        _final_norm_kernel,
        grid=(M // tm,),
        in_specs=[pl.BlockSpec((tm, D_MODEL), lambda i: (i, 0)),
                  pl.BlockSpec((1, D_MODEL), lambda i: (0, 0))],
        out_specs=pl.BlockSpec((tm, D_MODEL), lambda i: (i, 0)),
        out_shape=jax.ShapeDtypeStruct((M, D_MODEL), F32),
        compiler_params=_cparams("parallel"),
        name="final_norm",
    )(x, w)


def _lambda_of(lam_ref, lam_init):
    lf = lam_ref[...]
    s1 = jnp.sum(lf[0:1] * lf[1:2], axis=-1, keepdims=True)
    s2 = jnp.sum(lf[2:3] * lf[3:4], axis=-1, keepdims=True)
    return jnp.exp(s1) - jnp.exp(s2) + lam_init


def _attn_prompt_kernel(qi_ref, ki_ref, lam_ref, q_ref, k_ref, v_ref, sub_ref, o_ref,
                        m_ref, l_ref, acc_ref, *, t, lam_init):
    step = pl.program_id(2)
    qi = qi_ref[step]
    ki = ki_ref[step]

    @pl.when(ki == 0)
    def _():
        m_ref[...] = jnp.full(m_ref.shape, NEG_BIG, F32)
        l_ref[...] = jnp.zeros(l_ref.shape, F32)
        acc_ref[...] = jnp.zeros(acc_ref.shape, F32)

    def update(diagonal):
        q = q_ref[...] * (DA_SCALE * LOG2_E)
        kt = k_ref[...].astype(BF16)
        v = v_ref[...].astype(BF16)
        lane_map = lax.broadcasted_iota(jnp.int32, (1, LANES), 1) // DA_QK_DIM
        if diagonal:
            causal = (lax.broadcasted_iota(jnp.int32, (1, t), 1)
                      <= lax.broadcasted_iota(jnp.int32, (t, 1), 0))
        for m in range(2):
            qm = jnp.where(lane_map == m, q, 0.0).astype(BF16)
            s = jnp.dot(qm, kt, preferred_element_type=F32)
            if diagonal:
                s = jnp.where(causal, s, NEG_BIG)
            m_prev = m_ref[m]
            m_next = jnp.maximum(m_prev, jnp.max(s, axis=1, keepdims=True))
            alpha = jnp.exp2(m_prev - m_next)
            p = jnp.exp2(s - jnp.concatenate([m_next] * (t // LANES), axis=1))
            l_ref[m] = alpha * l_ref[m] + jnp.sum(p, axis=1, keepdims=True)
            acc_ref[m] = alpha * acc_ref[m] + jnp.dot(p.astype(BF16), v, preferred_element_type=F32)
            m_ref[m] = m_next

    @pl.when(ki < qi)
    def _():
        update(False)

    @pl.when(ki == qi)
    def _():
        update(True)
        lam = _lambda_of(lam_ref, lam_init)
        o = acc_ref[0] / l_ref[0] - lam * (acc_ref[1] / l_ref[1])
        o_ref[...] = (_rms_rows(o, sub_ref[...]) * (1.0 - lam_init)).astype(o_ref.dtype)


def _attn_prompt(rows, kt, da_lambda_e, subln_e, B, S, lam_init, t=512):
    nq = S // t
    pairs = [(qi, ki) for qi in range(nq) for ki in range(qi + 1)]
    qi_tab = jnp.asarray([p[0] for p in pairs], jnp.int32)
    ki_tab = jnp.asarray([p[1] for p in pairs], jnp.int32)
    kern = functools.partial(_attn_prompt_kernel, t=t, lam_init=lam_init)
    grid_spec = pltpu.PrefetchScalarGridSpec(
        num_scalar_prefetch=2,
        grid=(B, DA_HEADS, len(pairs)),
        in_specs=[
            pl.BlockSpec((4, DA_QK_DIM), lambda b, h, s, qt, kt: (0, 0)),
            pl.BlockSpec((t, LANES), lambda b, h, s, qt, kt: (b * nq + qt[s], h)),
            pl.BlockSpec((None, LANES, t), lambda b, h, s, qt, kt: (b, h, kt[s])),
            pl.BlockSpec((t, LANES), lambda b, h, s, qt, kt: (b * nq + kt[s], DA_HEADS + h)),
            pl.BlockSpec((1, DA_V_DIM), lambda b, h, s, qt, kt: (0, 0)),
        ],
        out_specs=pl.BlockSpec((t, LANES), lambda b, h, s, qt, kt: (b * nq + qt[s], h)),
        scratch_shapes=[pltpu.VMEM((2, t, LANES), F32), pltpu.VMEM((2, t, LANES), F32),
                        pltpu.VMEM((2, t, DA_V_DIM), F32)],
    )
    return pl.pallas_call(
        kern,
        grid_spec=grid_spec,
        out_shape=jax.ShapeDtypeStruct((B * S, DA_W), BF16),
        compiler_params=_cparams("parallel", "parallel", "arbitrary"),
        name="diff_attn_prompt",
    )(qi_tab, ki_tab, da_lambda_e, rows, kt, rows, subln_e)


PAGE_RING = 32


def _attn_sample_kernel(pt_ref, lam_ref, qbd_ref, knt_ref, vn_ref, sub_ref, ck_hbm, cv_hbm, o_ref,
                        ring_ref, sem_ref, st_ref, mx_ref, psum_ref, oacc_ref, vpad_ref,
                        *, layer, G, n_steps, dec_seq, lam_init):
    b = pl.program_id(0)
    t = pl.program_id(1)
    n_pages = n_steps * G
    steps_per_seq = 2 * n_steps
    step = b * steps_per_seq + t
    total_steps = pl.num_programs(0) * steps_per_seq
    ahead = PAGE_RING // G - 1

    def page_copy(src_hbm, page, slot):
        return pltpu.make_async_copy(src_hbm.at[layer, page], ring_ref.at[slot], sem_ref.at[slot])

    def start_step(s):
        sb = s // steps_per_seq
        stp = s % steps_per_seq
        is_k = stp < n_steps
        first = sb * n_pages + jnp.where(is_k, stp, stp - n_steps) * G
        for g in range(G):
            slot = (s * G + g) % PAGE_RING
            page = pt_ref[first + g]

            @pl.when(is_k)
            def _():
                page_copy(ck_hbm, page, slot).start()

            @pl.when(jnp.logical_not(is_k))
            def _():
                page_copy(cv_hbm, page, slot).start()

    @pl.when(step == 0)
    def _():
        for s in range(ahead):
            start_step(jnp.int32(s))

    @pl.when(step + ahead < total_steps)
    def _():
        start_step(step + ahead)

    def wait_pages():
        slots = [(step * G + g) % PAGE_RING for g in range(G)]
        for slot in slots:
            page_copy(ck_hbm, 0, slot).wait()
        return slots

    @pl.when(t == 0)
    def _():
        mx_ref[...] = jnp.full(mx_ref.shape, NEG_BIG, F32)
        vpad_ref[...] = jnp.zeros(vpad_ref.shape, F32)
        vpad_ref[0:dec_seq, :] = vn_ref[...]

    def scores(kt_page, idx, mask=None):
        s = jnp.dot(qbd_ref[...], kt_page.astype(BF16), preferred_element_type=F32)
        if mask is not None:
            s = jnp.where(mask, s, NEG_BIG)
        st_ref[idx] = s
        mx_ref[...] = jnp.maximum(mx_ref[...], s)

    @pl.when(t < n_steps)
    def _():
        for g, slot in enumerate(wait_pages()):
            scores(ring_ref[slot], t * G + g)

    @pl.when(t == n_steps - 1)
    def _():
        qry = lax.broadcasted_iota(jnp.int32, (LANES, PAGE_SIZE), 0) % dec_seq
        key = lax.broadcasted_iota(jnp.int32, (LANES, PAGE_SIZE), 1)
        scores(knt_ref[...], n_pages, mask=key <= qry)
        row_max = jnp.max(mx_ref[...], axis=1, keepdims=True)
        mx_ref[...] = jnp.broadcast_to(row_max, mx_ref.shape)
        psum_ref[...] = jnp.zeros(psum_ref.shape, F32)
        oacc_ref[...] = jnp.zeros(oacc_ref.shape, F32)

    def accumulate(v_flat, idx):
        p = jnp.exp2(st_ref[idx] - mx_ref[...])
        psum_ref[...] += p
        oacc_ref[...] += jnp.dot(p.astype(BF16), v_flat.astype(BF16), preferred_element_type=F32)

    def flat_values(page_ref):
        return jnp.concatenate(
            [page_ref[pl.ds(h, PAGE_SIZE, stride=DA_HEADS), :] for h in range(DA_HEADS)], axis=1)

    @pl.when(t >= n_steps)
    def _():
        for g, slot in enumerate(wait_pages()):
            accumulate(flat_values(ring_ref.at[slot]), (t - n_steps) * G + g)

    @pl.when(t == 2 * n_steps - 1)
    def _():
        accumulate(vpad_ref[...], n_pages)
        lam = _lambda_of(lam_ref, lam_init)
        denom = jnp.sum(psum_ref[...], axis=1, keepdims=True)
        rows_per_head = 2 * dec_seq
        for h in range(DA_HEADS):
            r0 = h * rows_per_head
            blk = oacc_ref[r0:r0 + rows_per_head, h * DA_V_DIM:(h + 1) * DA_V_DIM]
            n = blk / denom[r0:r0 + rows_per_head]
            o = n[0:dec_seq] - lam * n[dec_seq:rows_per_head]
            o_ref[:, h * DA_V_DIM:(h + 1) * DA_V_DIM] = _rms_rows(o, sub_ref[...]) * (1.0 - lam_init)


def _attn_sample(proj_s, kt_s, cache_k, cache_v, e, page_table, da_lambda_e, subln_e, DB, dec_seq, lam_init, G=8):
    n_pages = page_table.shape[1]
    n_steps = n_pages // G
    n_rows = DA_HEADS * 2 * dec_seq
    assert n_rows == LANES and n_pages % G == 0 and PAGE_RING % G == 0 and PAGE_RING // G >= 2
    assert DA_W == PAGE_SIZE * DA_HEADS
    n_layers, n_phys = cache_k.shape[:2]
    q = proj_s[:, :DA_W].reshape(DB, dec_seq, DA_HEADS * 2, DA_QK_DIM).transpose(0, 2, 1, 3)
    eye = jnp.eye(DA_HEADS * 2, dtype=F32)
    qbd = (q[:, :, :, None, :] * eye[None, :, None, :, None]).reshape(DB, n_rows, DA_W)
    qbd = (qbd * (DA_SCALE * LOG2_E)).astype(BF16)
    knt = kt_s.reshape(DA_W, DB, dec_seq).transpose(1, 0, 2)
    knt = jnp.pad(knt, ((0, 0), (0, 0), (0, PAGE_SIZE - dec_seq))).astype(BF16)
    ckt = jnp.transpose(cache_k, (0, 1, 3, 4, 5, 2)).reshape(n_layers, n_phys, DA_W, PAGE_SIZE)
    cv = cache_v.reshape(n_layers, n_phys, PAGE_SIZE * DA_HEADS, DA_V_DIM)
    pt = page_table.reshape(-1)

    kern = functools.partial(_attn_sample_kernel, layer=e, G=G, n_steps=n_steps, dec_seq=dec_seq,
                             lam_init=lam_init)
    grid_spec = pltpu.PrefetchScalarGridSpec(
        num_scalar_prefetch=1,
        grid=(DB, 2 * n_steps),
        in_specs=[
            pl.BlockSpec((4, DA_QK_DIM), lambda b, t, pt: (0, 0)),
            pl.BlockSpec((None, n_rows, DA_W), lambda b, t, pt: (b, 0, 0)),
            pl.BlockSpec((None, DA_W, PAGE_SIZE), lambda b, t, pt: (b, 0, 0)),
            pl.BlockSpec((dec_seq, DA_W), lambda b, t, pt: (b, 1)),
            pl.BlockSpec((1, DA_V_DIM), lambda b, t, pt: (0, 0)),
            pl.BlockSpec(memory_space=pl.ANY),
            pl.BlockSpec(memory_space=pl.ANY),
        ],
        out_specs=pl.BlockSpec((dec_seq, DA_W), lambda b, t, pt: (b, 0)),
        scratch_shapes=[
            pltpu.VMEM((PAGE_RING, DA_W, PAGE_SIZE), F32),
            pltpu.SemaphoreType.DMA((PAGE_RING,)),
            pltpu.VMEM((n_pages + 1, n_rows, PAGE_SIZE), F32),
            pltpu.VMEM((n_rows, PAGE_SIZE), F32),
            pltpu.VMEM((n_rows, PAGE_SIZE), F32),
            pltpu.VMEM((n_rows, DA_W), F32),
            pltpu.VMEM((PAGE_SIZE, DA_W), F32),
        ],
    )
    return pl.pallas_call(
        kern,
        grid_spec=grid_spec,
        out_shape=jax.ShapeDtypeStruct((DB * dec_seq, DA_W), F32),
        compiler_params=_cparams("arbitrary", "arbitrary"),
        name="diff_attn_sample",
    )(pt, da_lambda_e, qbd, knt, proj_s, subln_e, ckt, cv)


def _bmm(a, b):
    return jnp.einsum("gmk,gkn->gmn", a.astype(BF16), b.astype(BF16), preferred_element_type=F32)


def _bmm_nt(a, b):
    return jnp.einsum("gmk,gnk->gmn", a.astype(BF16), b.astype(BF16), preferred_element_type=F32)


def _bmm3(a, b):
    ah, al = _split2(a)
    bh, bl = _split2(b)
    d = functools.partial(jnp.einsum, "gmk,gkn->gmn", preferred_element_type=F32)
    return d(ah, bh) + (d(ah, bl) + d(al, bh))


def _tri_inverse(m_strict, C):
    ti = lax.broadcasted_iota(jnp.int32, (1, C, C), 1)
    sj = lax.broadcasted_iota(jnp.int32, (1, C, C), 2)
    eye = (ti == sj).astype(F32)
    a = jnp.where(ti // SUBLANES == sj // SUBLANES, -m_strict, 0.0)
    x = eye + a
    p = _bmm3(a, a)
    x = x + _bmm3(x, p)
    p = _bmm3(p, p)
    x = x + _bmm3(x, p)
    b = SUBLANES
    while b < C:
        join = jnp.logical_and(ti // (2 * b) == sj // (2 * b), ti // b != sj // b)
        c = jnp.where(join, m_strict, 0.0)
        x = x - _bmm3(x, _bmm3(c, x))
        b *= 2
    return x


def _gdn_kernel(*refs, C, NCH, has_state):
    if has_state:
        (q_ref, k_ref, v_ref, z_ref, ab_ref, cw_ref, alog_ref, dtb_ref, nw_ref,
         prev_ref, s0_ref, o_ref, sout_ref, buf_ref, s_ref) = refs
    else:
        (q_ref, k_ref, v_ref, z_ref, ab_ref, cw_ref, alog_ref, dtb_ref, nw_ref,
         o_ref, sout_ref, buf_ref, s_ref) = refs
    ci = pl.program_id(1)
    W = GDN_HEADS * GDN_DK
    H = GDN_HEADS
    R = C * NCH
    G = H * NCH

    @pl.when(ci == 0)
    def _():
        if has_state:
            buf_ref[0:SUBLANES, :] = prev_ref[...]
            s_ref[...] = s0_ref[...]
        else:
            buf_ref[0:SUBLANES, :] = jnp.zeros((SUBLANES, GDN_QKV), F32)
            s_ref[...] = jnp.zeros(s_ref.shape, F32)

    buf_ref[SUBLANES:SUBLANES + R, 0:W] = q_ref[...]
    buf_ref[SUBLANES:SUBLANES + R, W:2 * W] = k_ref[...]
    buf_ref[SUBLANES:SUBLANES + R, 2 * W:3 * W] = v_ref[...]
    y = buf_ref[SUBLANES:SUBLANES + R, :] * cw_ref[GDN_CONV - 1:GDN_CONV, :]
    for back in range(1, GDN_CONV):
        y = y + buf_ref[pl.ds(SUBLANES - back, R), :] * cw_ref[GDN_CONV - 1 - back:GDN_CONV - back, :]
    tail = buf_ref[R:R + SUBLANES, :]
    buf_ref[0:SUBLANES, :] = tail
    y = _silu(y)

    def heads(x, off, width):
        return jnp.stack([x[:, off + h * width:off + (h + 1) * width] for h in range(H)]).reshape(G, C, width)

    def chunk(x, c):
        return x.reshape((H, NCH) + x.shape[1:])[:, c]

    ab = ab_ref[...]
    sp = ab + dtb_ref[...]
    softplus = jnp.maximum(sp, 0.0) + jnp.log(1.0 + jnp.exp(-jnp.abs(sp)))
    g_all = -jnp.exp(alog_ref[...]) * softplus
    beta_all = _sigmoid(ab)

    ti = lax.broadcasted_iota(jnp.int32, (1, C, C), 1)
    sj = lax.broadcasted_iota(jnp.int32, (1, C, C), 2)
    causal = sj <= ti
    strict = sj < ti
    eye = ti == sj
    tri = causal[0].astype(BF16)
    bc_all = jnp.concatenate([_cumsum_rows(tri, g_all[c * C:(c + 1) * C]) for c in range(NCH)], axis=0)

    q3 = heads(y, 0, GDN_DK)
    k3 = heads(y, W, GDN_DK)
    v3 = heads(y, 2 * W, GDN_DV)
    q3 = q3 * lax.rsqrt(jnp.sum(q3 * q3, axis=-1, keepdims=True) + NORM_EPS) * (GDN_DK ** -0.5)
    k3 = k3 * lax.rsqrt(jnp.sum(k3 * k3, axis=-1, keepdims=True) + NORM_EPS)
    bcol = heads(bc_all, 0, 1)
    beta = heads(beta_all, H, 1)
    brow = jnp.sum(jnp.where(eye, bcol, 0.0), axis=1, keepdims=True)
    dec = jnp.exp(jnp.where(causal, bcol - brow, NEG_BIG))
    kb = k3 * beta
    eb = jnp.exp(bcol)
    m_strict = jnp.where(strict, _bmm_nt(kb, k3) * dec, 0.0)
    tinv = _tri_inverse(m_strict, C)
    uw = _bmm(tinv, jnp.concatenate([v3 * beta, kb * eb], axis=2))
    attn = _bmm_nt(q3, k3) * dec
    blast = bcol[:, C - 1:C, :]
    kd = k3 * jnp.exp(blast - bcol)
    gl = jnp.exp(blast)
    wq = jnp.concatenate([uw[:, :, GDN_DV:], q3 * eb], axis=1)
    z3 = heads(z_ref[...], 0, GDN_DV)

    s = s_ref[...]
    for c in range(NCH):
        ws = _bmm(chunk(wq, c), s)
        v_new = chunk(uw, c)[:, :, :GDN_DV] - ws[:, :C]
        o = ws[:, C:] + _bmm(chunk(attn, c), v_new)
        s = s * chunk(gl, c) + _bmm(jnp.swapaxes(chunk(kd, c), 1, 2), v_new)
        o = _rms_rows(o, nw_ref[...]) * _silu(chunk(z3, c))
        for h in range(H):
            o_ref[c * C:(c + 1) * C, h * GDN_DV:(h + 1) * GDN_DV] = o[h].astype(o_ref.dtype)
    s_ref[...] = s

    @pl.when(ci == pl.num_programs(1) - 1)
    def _():
        sout_ref[...] = s


def _gdn(proj, ab, conv_w_e, alog_pad, dtb_pad, norm_e, B, L, C, NCH=1, prev8=None, s0=None, out_dtype=F32):
    has_state = prev8 is not None
    R = C * NCH
    nc = L // R
    W = GDN_HEADS * GDN_DK
    base = (2 * DA_W) // W
    row = lambda b, c: (b * nc + c, 0)
    in_specs = [
        pl.BlockSpec((R, W), lambda b, c: (b * nc + c, base)),
        pl.BlockSpec((R, W), lambda b, c: (b * nc + c, base + 1)),
        pl.BlockSpec((R, W), lambda b, c: (b * nc + c, base + 2)),
        pl.BlockSpec((R, W), lambda b, c: (b * nc + c, base + 3)),
        pl.BlockSpec((R, LANES), row),
        pl.BlockSpec((GDN_CONV, GDN_QKV), lambda b, c: (0, 0)),
        pl.BlockSpec((1, LANES), lambda b, c: (0, 0)),
        pl.BlockSpec((1, LANES), lambda b, c: (0, 0)),
        pl.BlockSpec((1, GDN_DV), lambda b, c: (0, 0)),
    ]
    args = [proj, proj, proj, proj, ab, conv_w_e, alog_pad, dtb_pad, norm_e]
    state_block = (None, GDN_HEADS, GDN_DK, GDN_DV)
    if has_state:
        in_specs += [pl.BlockSpec((None, SUBLANES, GDN_QKV), lambda b, c: (b, 0, 0)),
                     pl.BlockSpec(state_block, lambda b, c: (b, 0, 0, 0))]
        args += [prev8, s0]
    return pl.pallas_call(
        functools.partial(_gdn_kernel, C=C, NCH=NCH, has_state=has_state),
        grid=(B, nc),
        in_specs=in_specs,
        out_specs=[pl.BlockSpec((R, W), row),
                   pl.BlockSpec(state_block, lambda b, c: (b, 0, 0, 0))],
        out_shape=[jax.ShapeDtypeStruct((B * L, W), out_dtype),
                   jax.ShapeDtypeStruct((B, GDN_HEADS, GDN_DK, GDN_DV), F32)],
        scratch_shapes=[pltpu.VMEM((SUBLANES + R, GDN_QKV), F32),
                        pltpu.VMEM((GDN_HEADS, GDN_DK, GDN_DV), F32)],
        compiler_params=_cparams("parallel", "arbitrary"),
        name="gdn_state" if has_state else "gdn_fresh",
    )(*args)


def _sub_block_rows(x4, b):
    sub = lax.broadcasted_iota(jnp.int32, (1, 1, SUBLANES, 1), 2)
    out = None
    for start in range(0, SUBLANES, b):
        mid = start + b // 2
        piece = jnp.broadcast_to(x4[:, :, mid:mid + 1, :], x4.shape)
        out = piece if out is None else jnp.where(sub >= start, piece, out)
    return out


def _hgrn_kernel(*refs, C, HB, SB, layer, has_state):
    if has_state:
        (q_ref, f_ref, i_ref, g_ref, lower_ref, nw_ref, s0_ref, o_ref, sout_ref, st_ref) = refs
    else:
        (q_ref, f_ref, i_ref, g_ref, lower_ref, nw_ref, o_ref, sout_ref, st_ref) = refs
    ci = pl.program_id(2)
    G = HB * SB
    K = HG_EXPAND

    @pl.when(ci == 0)
    def _():
        for h in range(HB):
            for sb in range(SB):
                st_ref[h * SB + sb] = s0_ref[sb, h].T if has_state else jnp.zeros((HG_DI, K), F32)

    def heads(x):
        return jnp.stack([x[:, h * K:(h + 1) * K] for h in range(HB)]).reshape(G, C, K)

    low = lower_ref[...]
    ex = jnp.exp(low - jnp.max(low, axis=0, keepdims=True))
    lb = jnp.sum(ex[1:layer + 1], axis=0, keepdims=True) / jnp.sum(ex, axis=0, keepdims=True)

    fg2 = lb + (1.0 - lb) * _sigmoid(f_ref[...])
    logf2 = jnp.log(fg2)
    ti = lax.broadcasted_iota(jnp.int32, (1, C, C), 1)
    sj = lax.broadcasted_iota(jnp.int32, (1, C, C), 2)
    tri = (sj[0] <= ti[0]).astype(BF16)
    bc2 = jnp.concatenate([_cumsum_rows(tri, logf2[sb * C:(sb + 1) * C]) for sb in range(SB)], axis=0)

    qt = heads(_silu(q_ref[...]) * (K ** -0.5))
    kk = heads(1.0 - fg2)
    v = heads(i_ref[...])
    bc = heads(bc2)
    st = st_ref[...]
    o = _bmm_nt(qt * jnp.exp(bc), st)

    ri = lax.broadcasted_iota(jnp.int32, (1, C, 1), 1)
    a = jnp.where(ti == sj, jnp.sum(qt * kk, axis=-1, keepdims=True), 0.0)
    bc4 = bc.reshape(G, C // SUBLANES, SUBLANES, K)
    b = 2
    while b <= C:
        if b <= SUBLANES:
            ref = _sub_block_rows(bc4, b).reshape(G, C, K)
        else:
            ref = jnp.concatenate(
                [jnp.broadcast_to(bc[:, m:m + 1, :], (G, b, K)) for m in range(b // 2, C, b)], axis=1)
        upper = (ri % b) >= (b // 2)
        d = bc - ref
        e = jnp.exp(jnp.where(upper, d, -d))
        qs = jnp.where(upper, qt * e, 0.0)
        ks = jnp.where(upper, 0.0, kk * e)
        a = a + jnp.where(ti // b == sj // b, _bmm_nt(qs, ks), 0.0)
        b *= 2
    o = o + _bmm(a, v)

    blast = bc[:, C - 1:C, :]
    st = st * jnp.exp(blast) + _bmm(jnp.swapaxes(v, 1, 2), kk * jnp.exp(blast - bc))
    st_ref[...] = st
    nw = jnp.stack([nw_ref[:, h * K:(h + 1) * K] for h in range(HB) for _ in range(SB)])
    o = _rms_rows(o, nw) * heads(_silu(g_ref[...]))
    for h in range(HB):
        for sb in range(SB):
            o_ref[sb * C:(sb + 1) * C, h * K:(h + 1) * K] = o[h * SB + sb].astype(o_ref.dtype)

    @pl.when(ci == pl.num_programs(2) - 1)
    def _():
        for h in range(HB):
            for sb in range(SB):
                sout_ref[sb, h] = st[h * SB + sb].T


def _hgrn(proj, hg_lower, hg_norm_o, layer, B, L, C, s0=None, HB=16, SB=1):
    has_state = s0 is not None
    nc = L // C
    assert SB == 1 or nc == 1
    nhg = HG_HEADS // HB
    Wb = HB * HG_EXPAND
    R = SB * C

    def col(section):
        return lambda b, hg, c: (b * nc + c, section * nhg + hg)

    in_specs = [pl.BlockSpec((R, Wb), col(s)) for s in range(4)] + [
        pl.BlockSpec((DEPTH, Wb), lambda b, hg, c: (0, hg)),
        pl.BlockSpec((1, Wb), lambda b, hg, c: (0, hg)),
    ]
    args = [proj, proj, proj, proj, hg_lower, hg_norm_o]
    state_block = (SB, HB, HG_EXPAND, HG_DI)
    if has_state:
        in_specs.append(pl.BlockSpec(state_block, lambda b, hg, c: (b, hg, 0, 0)))
        args.append(s0)
    return pl.pallas_call(
        functools.partial(_hgrn_kernel, C=C, HB=HB, SB=SB, layer=layer, has_state=has_state),
        grid=(B // SB, nhg, nc),
        in_specs=in_specs,
        out_specs=[pl.BlockSpec((R, Wb), col(0)),
                   pl.BlockSpec(state_block, lambda b, hg, c: (b, hg, 0, 0))],
        out_shape=[jax.ShapeDtypeStruct((B * L, D_MODEL), BF16),
                   jax.ShapeDtypeStruct((B, HG_HEADS, HG_EXPAND, HG_DI), F32)],
        scratch_shapes=[pltpu.VMEM((HB * SB, HG_DI, HG_EXPAND), F32)],
        compiler_params=_cparams("parallel", "parallel", "arbitrary"),
        name="hgrn_state" if has_state else "hgrn_fresh",
    )(*args)


def _pad_lanes(v, offset=0):
    return jnp.zeros((1, LANES), F32).at[0, offset:offset + v.shape[0]].set(v.astype(F32))


def kernel(x_prompt, x_sample, cache_k, cache_v, state_gdn_conv, state_gdn, state_hgrn, page_table, norm_mix, norm_ffn, norm_final, w_in_even, w_out_even, da_lambda, da_subln, gdn_conv_w, gdn_a_log, gdn_dt_bias, gdn_norm, w_in_odd, w_out_odd, hg_lower, hg_norm, w_gate_up, w_down):
    Bp, Lp = x_prompt.shape[:2]
    DB, Ls = x_sample.shape[:2]
    past_len = page_table.shape[1] * PAGE_SIZE
    pos_p = np.tile(np.arange(Lp), Bp)
    pos_s = np.tile(past_len + np.arange(Ls), DB)
    Mp, Ms = Bp * Lp, DB * Ls
    tm_p, tm_s = 1024, Ms

    hp = x_prompt.reshape(Mp, D_MODEL)
    hs = x_sample.reshape(Ms, D_MODEL)
    outs = {k: [] for k in ("k_p", "v_p", "conv_p", "gdn_p", "hg_p", "k_s", "v_s", "conv_s", "gdn_s", "hg_s")}

    for l in range(DEPTH):
        nw_mix = norm_mix[l].reshape(1, D_MODEL)
        nw_ffn = norm_ffn[l].reshape(1, D_MODEL)
        if l % 2 == 0:
            e = l // 2
            lam_init = 0.8 - 0.6 * math.exp(-0.3 * l)
            w_in_t = jnp.transpose(w_in_even, (0, 2, 1))
            wab = jnp.pad(w_in_t[e, EVEN_MAIN:], ((0, LANES - 2 * GDN_HEADS), (0, 0)))
            alog_pad = _pad_lanes(gdn_a_log[e])
            dtb_pad = _pad_lanes(gdn_dt_bias[e])
            sub = da_subln[e].reshape(1, DA_V_DIM)
            gnorm = gdn_norm[e].reshape(1, GDN_DV)
            conv_cols = slice(2 * DA_W, 2 * DA_W + GDN_QKV)
            proj_p, kt_p, ab_p, proj_s, kt_s, ab_s = _inproj_even(
                hp, hs, nw_mix, w_in_t, e, wab, pos_p, pos_s, Bp, tm_p)
            oa_p = _attn_prompt(proj_p, kt_p, da_lambda[e], sub, Bp, Lp, lam_init)
            gdn_c = min(GDN_CHUNK, Lp)
            ob_p, st = _gdn(proj_p, ab_p, gdn_conv_w[e], alog_pad, dtb_pad, gnorm, Bp, Lp, gdn_c,
                            NCH=4 if Lp % (4 * gdn_c) == 0 else 1, out_dtype=BF16)
            outs["k_p"].append(kt_p.reshape(Bp, DA_HEADS, 2, DA_QK_DIM, Lp).transpose(0, 4, 1, 2, 3))
            outs["v_p"].append(proj_p[:, DA_W:2 * DA_W].reshape(Bp, Lp, DA_HEADS, DA_V_DIM))
            outs["conv_p"].append(proj_p.reshape(Bp, Lp, EVEN_ROWS)[:, Lp - (GDN_CONV - 1):, conv_cols])
            outs["gdn_p"].append(st)
            oa_s = _attn_sample(proj_s, kt_s[0], cache_k, cache_v, e, page_table, da_lambda[e], sub,
                                DB, Ls, lam_init)
            prev8 = jnp.pad(state_gdn_conv[e], ((0, 0), (SUBLANES - (GDN_CONV - 1), 0), (0, 0)))
            ob_s, st = _gdn(proj_s, ab_s, gdn_conv_w[e], alog_pad, dtb_pad, gnorm, DB, Ls, min(GDN_CHUNK, Ls),
                            prev8=prev8, s0=state_gdn[e])
            gq = proj_s[:, conv_cols].reshape(DB, Ls, GDN_QKV)
            conv_s = jnp.concatenate([state_gdn_conv[e], gq], axis=1)[:, Ls:]
            outs["k_s"].append(kt_s[0].reshape(DA_HEADS, 2, DA_QK_DIM, DB, Ls).transpose(3, 4, 0, 1, 2))
            outs["v_s"].append(proj_s[:, DA_W:2 * DA_W].reshape(DB, Ls, DA_HEADS, DA_V_DIM))
            outs["conv_s"].append(conv_s)
            outs["gdn_s"].append(st)
            hp, hs = _mm_res([oa_p, ob_p], [oa_s, ob_s], w_out_even, e, hp, hs, tm_p)
        else:
            o = l // 2
            hnorm = hg_norm[o].reshape(1, D_MODEL)
            proj_p, proj_s = _norm_matmul(hp, hs, nw_mix, w_in_odd, o, tm_p)
            y_p, st = _hgrn(proj_p, hg_lower, hnorm, l, Bp, Lp, min(HG_CHUNK, Lp))
            outs["hg_p"].append(st)
            y_s, st = _hgrn(proj_s, hg_lower, hnorm, l, DB, Ls, min(HG_CHUNK, Ls), s0=state_hgrn[o],
                            SB=4 if Ls <= HG_CHUNK and DB % 4 == 0 else 1, HB=8)
            outs["hg_s"].append(st)
            hp, hs = _mm_res([y_p], [y_s], w_out_odd, o, hp, hs, tm_p)
        act_p, act_s = _ffn_up(hp, hs, nw_ffn, w_gate_up, l, tm_p)
        hp, hs = _mm_res([act_p], [act_s], w_down, l, hp, hs, tm_p, tk=D_FF // 2)

    nwf = norm_final.reshape(1, D_MODEL)
    y_prompt = _final_norm(hp, nwf, tm_p).reshape(Bp, Lp, D_MODEL)
    y_sample = _final_norm(hs, nwf, tm_s).reshape(DB, Ls, D_MODEL)
    st = lambda k: jnp.stack(outs[k])
    return (y_prompt, y_sample, st("k_p"), st("v_p"), st("conv_p"), st("gdn_p"), st("hg_p"),
            st("k_s"), st("v_s"), st("conv_s"), st("gdn_s"), st("hg_s"))
```

```python
import functools
import math

import numpy as np
import jax
import jax.numpy as jnp
from jax import lax
from jax.experimental import pallas as pl
from jax.experimental.pallas import tpu as pltpu

F32 = jnp.float32
BF16 = jnp.bfloat16

D_MODEL = 2048
DEPTH = 2
PAGE_SIZE = 128
NORM_EPS = 1e-6

DA_HEADS = 8
DA_QK_DIM = 64
DA_V_DIM = 2 * DA_QK_DIM
DA_SCALE = DA_QK_DIM ** -0.5
LOG2_E = 1.4426950408889634
ROT_DIM = DA_QK_DIM // 4
ROPE_THETA = 500000.0

GDN_HEADS = 8
GDN_DK = 128
GDN_DV = 128
GDN_QKV = GDN_HEADS * (2 * GDN_DK + GDN_DV)
GDN_CONV = 4
GDN_CHUNK = 64

HG_EXPAND = 128
HG_HEADS = D_MODEL // HG_EXPAND
HG_DI = D_MODEL // HG_HEADS
HG_CHUNK = 64

D_FF = -(-(8 * D_MODEL) // (3 * 256)) * 256

DA_W = DA_HEADS * 2 * DA_QK_DIM
EVEN_MAIN = 3 * DA_W + GDN_QKV + GDN_HEADS * GDN_DV
EVEN_ROWS = EVEN_MAIN - DA_W

LANES = 128
SUBLANES = 8
VMEM_LIMIT = 56 * 1024 * 1024
NEG_BIG = -1e30


def _cparams(*sem):
    return pltpu.CompilerParams(dimension_semantics=sem, vmem_limit_bytes=VMEM_LIMIT)


def _split2(x):
    hi = x.astype(BF16)
    lo = (x - hi.astype(F32)).astype(BF16)
    return hi, lo


def _cumsum_rows(tri, x):
    hi = x.astype(BF16)
    r = x - hi.astype(F32)
    mid = r.astype(BF16)
    lo = (r - mid.astype(F32)).astype(BF16)
    d = functools.partial(jnp.dot, preferred_element_type=F32)
    return d(tri, hi) + (d(tri, mid) + d(tri, lo))


def _sigmoid(x):
    return 0.5 * jnp.tanh(0.5 * x) + 0.5


def _silu(x):
    return x * _sigmoid(x)


def _rms_rows(x, w):
    return x * lax.rsqrt(jnp.mean(x * x, axis=-1, keepdims=True) + NORM_EPS) * w


W_SLABS = 4


def _slab_specs(k_rows, tn, index, transposed=False):
    def spec(s):
        def index_map(*g):
            layer, kt, col = index(*g)
            return (layer, col * W_SLABS + s, kt) if transposed else (layer, kt * W_SLABS + s, col)
        block = (None, tn // W_SLABS, k_rows) if transposed else (None, k_rows // W_SLABS, tn)
        return pl.BlockSpec(block, index_map)
    return [spec(s) for s in range(W_SLABS)]


def _dot_nt(a, b):
    return lax.dot_general(a, b, (((1,), (1,)), ((), ())), preferred_element_type=F32)


def _load_weight(w_refs, wbuf_ref):
    rows = wbuf_ref.shape[0] // len(w_refs)
    for s, r in enumerate(w_refs):
        wbuf_ref[s * rows:(s + 1) * rows, :] = r[...].astype(BF16)
    return wbuf_ref[...]


def _small_col(i, j, n_i):
    return jnp.where(i == n_i - 1, j, 0)


def _inproj_even_kernel(xp_ref, xs_ref, nw_ref, *rest, n_i, n_q_tiles, n_k_tiles, tn):
    w_refs = rest[:W_SLABS]
    (wab_ref, lane_p0, lane_p1, sub_p0, sub_p1, lane_s0, lane_s1, sub_s0, sub_s1,
     op_ref, ktp_ref, abp_ref, os_ref, kts_ref, abs_ref,
     xnp_ref, xns_ref, wbuf_ref) = rest[W_SLABS:]
    i = pl.program_id(0)
    j = pl.program_id(1)
    last = i == n_i - 1
    half = ROT_DIM // 2

    def prologue(x_ref, xn_ref, ab_ref):
        xn = _rms_rows(x_ref[...], nw_ref[...]).astype(BF16)
        xn_ref[...] = xn
        ab_ref[...] = _dot_nt(xn, wab_ref[...].astype(BF16))

    @pl.when(j == 0)
    def _():
        prologue(xp_ref, xnp_ref, abp_ref)

    @pl.when(jnp.logical_and(last, j == 0))
    def _():
        prologue(xs_ref, xns_ref, abs_ref)

    def rotate(a, cosf, sinf, axis):
        ch = lax.broadcasted_iota(jnp.int32, a.shape, axis) % DA_QK_DIM
        partner = jnp.where(ch < half, -pltpu.roll(a, LANES - half, axis), pltpu.roll(a, half, axis))
        return a * cosf + partner * sinf

    def q_tile(w, xn_ref, o_ref, tabs):
        acc = _dot_nt(xn_ref[...], w)
        cosf, sinf = (t[...] for t in tabs)
        for c in range(tn // LANES):
            o_ref[:, c * LANES:(c + 1) * LANES] = rotate(acc[:, c * LANES:(c + 1) * LANES], cosf, sinf, 1)

    def k_tile(w, xn_ref, kt_ref, tabs):
        acc = _dot_nt(w, xn_ref[...])
        cosf, sinf = (t[...] for t in tabs)
        for c in range(tn // LANES):
            kt_ref[c * LANES:(c + 1) * LANES, :] = rotate(acc[c * LANES:(c + 1) * LANES, :], cosf, sinf, 0)

    def plain_tile(w, xn_ref, o_ref):
        o_ref[...] = _dot_nt(xn_ref[...], w)

    is_q = j < n_q_tiles
    is_k = jnp.logical_and(j >= n_q_tiles, j < n_q_tiles + n_k_tiles)
    is_plain = j >= n_q_tiles + n_k_tiles

    @pl.when(is_q)
    def _():
        q_tile(_load_weight(w_refs, wbuf_ref), xnp_ref, op_ref, (lane_p0, lane_p1))

    @pl.when(is_k)
    def _():
        k_tile(_load_weight(w_refs, wbuf_ref), xnp_ref, ktp_ref, (sub_p0, sub_p1))

    @pl.when(is_plain)
    def _():
        plain_tile(_load_weight(w_refs, wbuf_ref), xnp_ref, op_ref)

    @pl.when(jnp.logical_and(last, is_q))
    def _():
        q_tile(wbuf_ref[...], xns_ref, os_ref, (lane_s0, lane_s1))

    @pl.when(jnp.logical_and(last, is_k))
    def _():
        k_tile(wbuf_ref[...], xns_ref, kts_ref, (sub_s0, sub_s1))

    @pl.when(jnp.logical_and(last, is_plain))
    def _():
        plain_tile(wbuf_ref[...], xns_ref, os_ref)


def _rope_tables(pos):
    half = ROT_DIM // 2
    inv = np.power(np.float32(ROPE_THETA), -np.arange(half, dtype=np.float32) * np.float32(2.0) / np.float32(ROT_DIM))
    ang = (pos.astype(np.float32)[:, None] * inv[None, :]).astype(np.float32)
    cos = np.cos(ang.astype(np.float64)).astype(np.float32)
    sin = np.sin(ang.astype(np.float64)).astype(np.float32)
    n = pos.shape[0]
    cosf = np.ones((n, LANES), np.float32)
    sinf = np.zeros((n, LANES), np.float32)
    for base in range(0, LANES, DA_QK_DIM):
        for off in (0, half):
            cosf[:, base + off:base + off + half] = cos
            sinf[:, base + off:base + off + half] = sin
    return cosf, sinf


def _inproj_even(xp, xs, nw, w_in_t, e, wab, pos_p, pos_s, B, tm, tn=512):
    M = xp.shape[0]
    Ms = xs.shape[0]
    L = M // B
    lt = L // tm
    n_i = M // tm
    tabs_p = _rope_tables(pos_p)
    tabs_s = _rope_tables(pos_s)
    lane = lambda tabs: [jnp.asarray(t) for t in tabs]
    sub = lambda tabs: [jnp.asarray(np.ascontiguousarray(t.T)) for t in tabs]
    n_q, n_k = DA_W // tn, DA_W // tn
    n_tiles = EVEN_MAIN // tn
    kern = functools.partial(_inproj_even_kernel, n_i=n_i, n_q_tiles=n_q, n_k_tiles=n_k, tn=tn)
    row = lambda i, j: (i, 0)
    col = lambda i, j: (0, i)
    fixed = lambda i, j: (0, 0)
    once = pl.Buffered(1)
    row_col = lambda j: jnp.where(j < n_q, j, jnp.maximum(j - n_k, n_q - 1))
    kt_col = lambda j: jnp.clip(j - n_q, 0, n_k - 1)
    return pl.pallas_call(
        kern,
        grid=(n_i, n_tiles),
        in_specs=[
            pl.BlockSpec((tm, D_MODEL), row),
            pl.BlockSpec((Ms, D_MODEL), fixed, pipeline_mode=once),
            pl.BlockSpec((1, D_MODEL), fixed, pipeline_mode=once),
        ] + _slab_specs(D_MODEL, tn, lambda i, j: (e, 0, j), transposed=True) + [
            pl.BlockSpec((LANES, D_MODEL), fixed, pipeline_mode=once),
        ] + [pl.BlockSpec((tm, LANES), row, pipeline_mode=once)] * 2
          + [pl.BlockSpec((LANES, tm), col, pipeline_mode=once)] * 2
          + [pl.BlockSpec((Ms, LANES), fixed, pipeline_mode=once)] * 2
          + [pl.BlockSpec((LANES, Ms), fixed, pipeline_mode=once)] * 2,
        out_specs=[
            pl.BlockSpec((tm, tn), lambda i, j: (i, row_col(j))),
            pl.BlockSpec((None, tn, tm), lambda i, j: (i // lt, kt_col(j), i % lt)),
            pl.BlockSpec((tm, LANES), row),
            pl.BlockSpec((Ms, tn), lambda i, j: (0, _small_col(i, row_col(j), n_i))),
            pl.BlockSpec((None, tn, Ms), lambda i, j: (0, _small_col(i, kt_col(j), n_i), 0)),
            pl.BlockSpec((Ms, LANES), fixed),
        ],
        out_shape=[jax.ShapeDtypeStruct((M, EVEN_ROWS), F32),
                   jax.ShapeDtypeStruct((B, DA_W, L), F32),
                   jax.ShapeDtypeStruct((M, LANES), F32),
                   jax.ShapeDtypeStruct((Ms, EVEN_ROWS), F32),
                   jax.ShapeDtypeStruct((1, DA_W, Ms), F32),
                   jax.ShapeDtypeStruct((Ms, LANES), F32)],
        scratch_shapes=[pltpu.VMEM((tm, D_MODEL), BF16), pltpu.VMEM((Ms, D_MODEL), BF16),
                        pltpu.VMEM((tn, D_MODEL), BF16)],
        compiler_params=_cparams("arbitrary", "arbitrary"),
        name="inproj_even",
    )(xp, xs, nw, *([w_in_t] * W_SLABS), wab, *lane(tabs_p), *sub(tabs_p), *lane(tabs_s), *sub(tabs_s))


def _norm_matmul_kernel(xp_ref, xs_ref, nw_ref, *rest, n_i):
    w_refs = rest[:W_SLABS]
    op_ref, os_ref, xnp_ref, xns_ref, wbuf_ref = rest[W_SLABS:]
    i = pl.program_id(0)
    j = pl.program_id(1)
    last = i == n_i - 1

    @pl.when(j == 0)
    def _():
        xnp_ref[...] = _rms_rows(xp_ref[...], nw_ref[...]).astype(BF16)

    @pl.when(jnp.logical_and(last, j == 0))
    def _():
        xns_ref[...] = _rms_rows(xs_ref[...], nw_ref[...]).astype(BF16)

    w = _load_weight(w_refs, wbuf_ref)
    op_ref[...] = jnp.dot(xnp_ref[...], w, preferred_element_type=F32)

    @pl.when(last)
    def _():
        os_ref[...] = jnp.dot(xns_ref[...], w, preferred_element_type=F32)


def _norm_matmul(xp, xs, nw, w_all, l, tm, tn=1024):
    M = xp.shape[0]
    Ms = xs.shape[0]
    N = w_all.shape[-1]
    n_i = M // tm
    return pl.pallas_call(
        functools.partial(_norm_matmul_kernel, n_i=n_i),
        grid=(n_i, N // tn),
        in_specs=[
            pl.BlockSpec((tm, D_MODEL), lambda i, j: (i, 0), pipeline_mode=pl.Buffered(1)),
            pl.BlockSpec((Ms, D_MODEL), lambda i, j: (0, 0), pipeline_mode=pl.Buffered(1)),
            pl.BlockSpec((1, D_MODEL), lambda i, j: (0, 0), pipeline_mode=pl.Buffered(1)),
        ] + _slab_specs(D_MODEL, tn, lambda i, j: (l, 0, j)),
        out_specs=[pl.BlockSpec((tm, tn), lambda i, j: (i, j)),
                   pl.BlockSpec((Ms, tn), lambda i, j: (0, _small_col(i, j, n_i)))],
        out_shape=[jax.ShapeDtypeStruct((M, N), F32), jax.ShapeDtypeStruct((Ms, N), F32)],
        scratch_shapes=[pltpu.VMEM((tm, D_MODEL), BF16), pltpu.VMEM((Ms, D_MODEL), BF16),
                        pltpu.VMEM((D_MODEL, tn), BF16)],
        compiler_params=_cparams("arbitrary", "arbitrary"),
        name="norm_matmul",
    )(xp, xs, nw, *([w_all] * W_SLABS))


def _ffn_up_kernel(xp_ref, xs_ref, nw_ref, *rest, n_i):
    wg_refs = rest[:W_SLABS]
    wu_refs = rest[W_SLABS:2 * W_SLABS]
    op_ref, os_ref, xnp_ref, xns_ref, wgbuf_ref, wubuf_ref = rest[2 * W_SLABS:]
    i = pl.program_id(0)
    j = pl.program_id(1)
    last = i == n_i - 1

    @pl.when(j == 0)
    def _():
        xnp_ref[...] = _rms_rows(xp_ref[...], nw_ref[...]).astype(BF16)

    @pl.when(jnp.logical_and(last, j == 0))
    def _():
        xns_ref[...] = _rms_rows(xs_ref[...], nw_ref[...]).astype(BF16)

    wg = _load_weight(wg_refs, wgbuf_ref)
    wu = _load_weight(wu_refs, wubuf_ref)

    def tile(xn_ref, o_ref):
        xn = xn_ref[...]
        g = jnp.dot(xn, wg, preferred_element_type=F32)
        u = jnp.dot(xn, wu, preferred_element_type=F32)
        o_ref[...] = (_silu(g) * u).astype(BF16)

    tile(xnp_ref, op_ref)

    @pl.when(last)
    def _():
        tile(xns_ref, os_ref)


def _ffn_up(xp, xs, nw, w_gate_up, l, tm, tn=512):
    M = xp.shape[0]
    Ms = xs.shape[0]
    nj = D_FF // tn
    n_i = M // tm
    return pl.pallas_call(
        functools.partial(_ffn_up_kernel, n_i=n_i),
        grid=(n_i, nj),
        in_specs=[
            pl.BlockSpec((tm, D_MODEL), lambda i, j: (i, 0)),
            pl.BlockSpec((Ms, D_MODEL), lambda i, j: (0, 0), pipeline_mode=pl.Buffered(1)),
            pl.BlockSpec((1, D_MODEL), lambda i, j: (0, 0), pipeline_mode=pl.Buffered(1)),
        ] + _slab_specs(D_MODEL, tn, lambda i, j: (l, 0, j))
          + _slab_specs(D_MODEL, tn, lambda i, j: (l, 0, j + nj)),
        out_specs=[pl.BlockSpec((tm, tn), lambda i, j: (i, j)),
                   pl.BlockSpec((Ms, tn), lambda i, j: (0, _small_col(i, j, n_i)))],
        out_shape=[jax.ShapeDtypeStruct((M, D_FF), BF16), jax.ShapeDtypeStruct((Ms, D_FF), BF16)],
        scratch_shapes=[pltpu.VMEM((tm, D_MODEL), BF16), pltpu.VMEM((Ms, D_MODEL), BF16),
                        pltpu.VMEM((D_MODEL, tn), BF16), pltpu.VMEM((D_MODEL, tn), BF16)],
        compiler_params=_cparams("arbitrary", "arbitrary"),
        name="ffn_up",
    )(xp, xs, nw, *([w_gate_up] * (2 * W_SLABS)))


def _mm_res_kernel(*refs, nk, na, n_i):
    ap_refs = refs[:na]
    as_refs = refs[na:2 * na]
    w_refs = refs[2 * na:2 * na + W_SLABS]
    (rp_ref, rs_ref, op_ref, os_ref,
     accp_ref, accs_ref, wbuf_ref, abufp_ref, abufs_ref) = refs[2 * na + W_SLABS:]
    i = pl.program_id(0)
    k = pl.program_id(2)
    last = i == n_i - 1
    w = _load_weight(w_refs, wbuf_ref)

    def tile(a_refs, abuf_ref, r_ref, o_ref, acc_ref):
        if na == 1:
            a = a_refs[0][...].astype(BF16)
        else:
            off = 0
            for r in a_refs:
                abuf_ref[:, off:off + r.shape[1]] = r[...].astype(BF16)
                off += r.shape[1]
            a = abuf_ref[...]
        p = jnp.dot(a, w, preferred_element_type=F32)
        if nk == 1:
            o_ref[...] = r_ref[...] + p
        else:
            @pl.when(k == 0)
            def _():
                acc_ref[...] = p

            @pl.when(jnp.logical_and(k > 0, k < nk - 1))
            def _():
                acc_ref[...] += p

            @pl.when(k == nk - 1)
            def _():
                o_ref[...] = r_ref[...] + (acc_ref[...] + p)

    tile(ap_refs, abufp_ref, rp_ref, op_ref, accp_ref)

    @pl.when(last)
    def _():
        tile(as_refs, abufs_ref, rs_ref, os_ref, accs_ref)


def _mm_res(ap_list, as_list, w_all, l, res_p, res_s, tm, tn=512, tk=None):
    M = res_p.shape[0]
    Ms = res_s.shape[0]
    K = sum(a.shape[1] for a in ap_list)
    N = w_all.shape[-1]
    tk = K if tk is None else tk
    nk = K // tk
    na = len(ap_list)
    n_i = M // tm
    assert na == 1 or nk == 1
    small = lambda i, j, k: (0, _small_col(i, j, n_i))
    if na == 1:
        a_specs = [pl.BlockSpec((tm, tk), lambda i, j, k: (i, k)),
                   pl.BlockSpec((Ms, tk), lambda i, j, k: (0, jnp.where(i == n_i - 1, k, 0)))]
    else:
        a_specs = ([pl.BlockSpec((tm, a.shape[1]), lambda i, j, k: (i, 0)) for a in ap_list]
                   + [pl.BlockSpec((Ms, a.shape[1]), lambda i, j, k: (0, 0)) for a in as_list])
    abuf = lambda rows: pltpu.VMEM((rows, tk) if na > 1 else (SUBLANES * 2, LANES), BF16)
    return pl.pallas_call(
        functools.partial(_mm_res_kernel, nk=nk, na=na, n_i=n_i),
        grid=(n_i, N // tn, nk),
        in_specs=a_specs + _slab_specs(tk, tn, lambda i, j, k: (l, k, j))
                 + [pl.BlockSpec((tm, tn), lambda i, j, k: (i, j)), pl.BlockSpec((Ms, tn), small)],
        out_specs=[pl.BlockSpec((tm, tn), lambda i, j, k: (i, j)), pl.BlockSpec((Ms, tn), small)],
        out_shape=[jax.ShapeDtypeStruct((M, N), F32), jax.ShapeDtypeStruct((Ms, N), F32)],
        scratch_shapes=[pltpu.VMEM((tm, tn), F32), pltpu.VMEM((Ms, tn), F32), pltpu.VMEM((tk, tn), BF16),
                        abuf(tm), abuf(Ms)],
        compiler_params=_cparams("arbitrary", "arbitrary", "arbitrary"),
        name="matmul_residual",
    )(*ap_list, *as_list, *([w_all] * W_SLABS), res_p, res_s)


def _final_norm_kernel(x_ref, w_ref, o_ref):
    o_ref[...] = _rms_rows(x_ref[...], w_ref[...])


def _final_norm(x, w, tm):
    M = x.shape[0]
    return pl.pallas_call(
        _final_norm_kernel,
        grid=(M // tm,),
        in_specs=[pl.BlockSpec((tm, D_MODEL), lambda i: (i, 0)),
                  pl.BlockSpec((1, D_MODEL), lambda i: (0, 0))],
        out_specs=pl.BlockSpec((tm, D_MODEL), lambda i: (i, 0)),
        out_shape=jax.ShapeDtypeStruct((M, D_MODEL), F32),
        compiler_params=_cparams("parallel"),
        name="final_norm",
    )(x, w)


def _lambda_of(lam_ref, lam_init):
    lf = lam_ref[...]
    s1 = jnp.sum(lf[0:1] * lf[1:2], axis=-1, keepdims=True)
    s2 = jnp.sum(lf[2:3] * lf[3:4], axis=-1, keepdims=True)
    return jnp.exp(s1) - jnp.exp(s2) + lam_init


def _attn_prompt_kernel(qi_ref, ki_ref, lam_ref, q_ref, k_ref, v_ref, sub_ref, o_ref,
                        m_ref, l_ref, acc_ref, *, t, lam_init):
    step = pl.program_id(2)
    qi = qi_ref[step]
    ki = ki_ref[step]

    @pl.when(ki == 0)
    def _():
        m_ref[...] = jnp.full(m_ref.shape, NEG_BIG, F32)
        l_ref[...] = jnp.zeros(l_ref.shape, F32)
        acc_ref[...] = jnp.zeros(acc_ref.shape, F32)

    def update(diagonal):
        q = q_ref[...] * (DA_SCALE * LOG2_E)
        kt = k_ref[...].astype(BF16)
        v = v_ref[...].astype(BF16)
        lane_map = lax.broadcasted_iota(jnp.int32, (1, LANES), 1) // DA_QK_DIM
        if diagonal:
            causal = (lax.broadcasted_iota(jnp.int32, (1, t), 1)
                      <= lax.broadcasted_iota(jnp.int32, (t, 1), 0))
        for m in range(2):
            qm = jnp.where(lane_map == m, q, 0.0).astype(BF16)
            s = jnp.dot(qm, kt, preferred_element_type=F32)
            if diagonal:
                s = jnp.where(causal, s, NEG_BIG)
            m_prev = m_ref[m]
            m_next = jnp.maximum(m_prev, jnp.max(s, axis=1, keepdims=True))
            alpha = jnp.exp2(m_prev - m_next)
            p = jnp.exp2(s - jnp.concatenate([m_next] * (t // LANES), axis=1))
            l_ref[m] = alpha * l_ref[m] + jnp.sum(p, axis=1, keepdims=True)
            acc_ref[m] = alpha * acc_ref[m] + jnp.dot(p.astype(BF16), v, preferred_element_type=F32)
            m_ref[m] = m_next

    @pl.when(ki < qi)
    def _():
        update(False)

    @pl.when(ki == qi)
    def _():
        update(True)
        lam = _lambda_of(lam_ref, lam_init)
        o = acc_ref[0] / l_ref[0] - lam * (acc_ref[1] / l_ref[1])
        o_ref[...] = (_rms_rows(o, sub_ref[...]) * (1.0 - lam_init)).astype(o_ref.dtype)


def _attn_prompt(rows, kt, da_lambda_e, subln_e, B, S, lam_init, t=512):
    nq = S // t
    pairs = [(qi, ki) for qi in range(nq) for ki in range(qi + 1)]
    qi_tab = jnp.asarray([p[0] for p in pairs], jnp.int32)
    ki_tab = jnp.asarray([p[1] for p in pairs], jnp.int32)
    kern = functools.partial(_attn_prompt_kernel, t=t, lam_init=lam_init)
    grid_spec = pltpu.PrefetchScalarGridSpec(
        num_scalar_prefetch=2,
        grid=(B, DA_HEADS, len(pairs)),
        in_specs=[
            pl.BlockSpec((4, DA_QK_DIM), lambda b, h, s, qt, kt: (0, 0)),
            pl.BlockSpec((t, LANES), lambda b, h, s, qt, kt: (b * nq + qt[s], h)),
            pl.BlockSpec((None, LANES, t), lambda b, h, s, qt, kt: (b, h, kt[s])),
            pl.BlockSpec((t, LANES), lambda b, h, s, qt, kt: (b * nq + kt[s], DA_HEADS + h)),
            pl.BlockSpec((1, DA_V_DIM), lambda b, h, s, qt, kt: (0, 0)),
        ],
        out_specs=pl.BlockSpec((t, LANES), lambda b, h, s, qt, kt: (b * nq + qt[s], h)),
        scratch_shapes=[pltpu.VMEM((2, t, LANES), F32), pltpu.VMEM((2, t, LANES), F32),
                        pltpu.VMEM((2, t, DA_V_DIM), F32)],
    )
    return pl.pallas_call(
        kern,
        grid_spec=grid_spec,
        out_shape=jax.ShapeDtypeStruct((B * S, DA_W), BF16),
        compiler_params=_cparams("parallel", "parallel", "arbitrary"),
        name="diff_attn_prompt",
    )(qi_tab, ki_tab, da_lambda_e, rows, kt, rows, subln_e)


PAGE_RING = 32


def _attn_sample_kernel(pt_ref, lam_ref, qbd_ref, knt_ref, vn_ref, sub_ref, ck_hbm, cv_hbm, o_ref,
                        ring_ref, sem_ref, st_ref, mx_ref, psum_ref, oacc_ref, vpad_ref,
                        *, layer, G, n_steps, dec_seq, lam_init):
    b = pl.program_id(0)
    t = pl.program_id(1)
    n_pages = n_steps * G
    steps_per_seq = 2 * n_steps
    step = b * steps_per_seq + t
    total_steps = pl.num_programs(0) * steps_per_seq
    ahead = PAGE_RING // G - 1

    def page_copy(src_hbm, page, slot):
        return pltpu.make_async_copy(src_hbm.at[layer, page], ring_ref.at[slot], sem_ref.at[slot])

    def start_step(s):
        sb = s // steps_per_seq
        stp = s % steps_per_seq
        is_k = stp < n_steps
        first = sb * n_pages + jnp.where(is_k, stp, stp - n_steps) * G
        for g in range(G):
            slot = (s * G + g) % PAGE_RING
            page = pt_ref[first + g]

            @pl.when(is_k)
            def _():
                page_copy(ck_hbm, page, slot).start()

            @pl.when(jnp.logical_not(is_k))
            def _():
                page_copy(cv_hbm, page, slot).start()

    @pl.when(step == 0)
    def _():
        for s in range(ahead):
            start_step(jnp.int32(s))

    @pl.when(step + ahead < total_steps)
    def _():
        start_step(step + ahead)

    def wait_pages():
        slots = [(step * G + g) % PAGE_RING for g in range(G)]
        for slot in slots:
            page_copy(ck_hbm, 0, slot).wait()
        return slots

    @pl.when(t == 0)
    def _():
        mx_ref[...] = jnp.full(mx_ref.shape, NEG_BIG, F32)
        vpad_ref[...] = jnp.zeros(vpad_ref.shape, F32)
        vpad_ref[0:dec_seq, :] = vn_ref[...]

    def scores(kt_page, idx, mask=None):
        s = jnp.dot(qbd_ref[...], kt_page.astype(BF16), preferred_element_type=F32)
        if mask is not None:
            s = jnp.where(mask, s, NEG_BIG)
        st_ref[idx] = s
        mx_ref[...] = jnp.maximum(mx_ref[...], s)

    @pl.when(t < n_steps)
    def _():
        for g, slot in enumerate(wait_pages()):
            scores(ring_ref[slot], t * G + g)

    @pl.when(t == n_steps - 1)
    def _():
        qry = lax.broadcasted_iota(jnp.int32, (LANES, PAGE_SIZE), 0) % dec_seq
        key = lax.broadcasted_iota(jnp.int32, (LANES, PAGE_SIZE), 1)
        scores(knt_ref[...], n_pages, mask=key <= qry)
        row_max = jnp.max(mx_ref[...], axis=1, keepdims=True)
        mx_ref[...] = jnp.broadcast_to(row_max, mx_ref.shape)
        psum_ref[...] = jnp.zeros(psum_ref.shape, F32)
        oacc_ref[...] = jnp.zeros(oacc_ref.shape, F32)

    def accumulate(v_flat, idx):
        p = jnp.exp2(st_ref[idx] - mx_ref[...])
        psum_ref[...] += p
        oacc_ref[...] += jnp.dot(p.astype(BF16), v_flat.astype(BF16), preferred_element_type=F32)

    def flat_values(page_ref):
        return jnp.concatenate(
            [page_ref[pl.ds(h, PAGE_SIZE, stride=DA_HEADS), :] for h in range(DA_HEADS)], axis=1)

    @pl.when(t >= n_steps)
    def _():
        for g, slot in enumerate(wait_pages()):
            accumulate(flat_values(ring_ref.at[slot]), (t - n_steps) * G + g)

    @pl.when(t == 2 * n_steps - 1)
    def _():
        accumulate(vpad_ref[...], n_pages)
        lam = _lambda_of(lam_ref, lam_init)
        denom = jnp.sum(psum_ref[...], axis=1, keepdims=True)
        rows_per_head = 2 * dec_seq
        for h in range(DA_HEADS):
            r0 = h * rows_per_head
            blk = oacc_ref[r0:r0 + rows_per_head, h * DA_V_DIM:(h + 1) * DA_V_DIM]
            n = blk / denom[r0:r0 + rows_per_head]
            o = n[0:dec_seq] - lam * n[dec_seq:rows_per_head]
            o_ref[:, h * DA_V_DIM:(h + 1) * DA_V_DIM] = _rms_rows(o, sub_ref[...]) * (1.0 - lam_init)


def _attn_sample(proj_s, kt_s, cache_k, cache_v, e, page_table, da_lambda_e, subln_e, DB, dec_seq, lam_init, G=8):
    n_pages = page_table.shape[1]
    n_steps = n_pages // G
    n_rows = DA_HEADS * 2 * dec_seq
    assert n_rows == LANES and n_pages % G == 0 and PAGE_RING % G == 0 and PAGE_RING // G >= 2
    assert DA_W == PAGE_SIZE * DA_HEADS
    n_layers, n_phys = cache_k.shape[:2]
    q = proj_s[:, :DA_W].reshape(DB, dec_seq, DA_HEADS * 2, DA_QK_DIM).transpose(0, 2, 1, 3)
    eye = jnp.eye(DA_HEADS * 2, dtype=F32)
    qbd = (q[:, :, :, None, :] * eye[None, :, None, :, None]).reshape(DB, n_rows, DA_W)
    qbd = (qbd * (DA_SCALE * LOG2_E)).astype(BF16)
    knt = kt_s.reshape(DA_W, DB, dec_seq).transpose(1, 0, 2)
    knt = jnp.pad(knt, ((0, 0), (0, 0), (0, PAGE_SIZE - dec_seq))).astype(BF16)
    ckt = jnp.transpose(cache_k, (0, 1, 3, 4, 5, 2)).reshape(n_layers, n_phys, DA_W, PAGE_SIZE)
    cv = cache_v.reshape(n_layers, n_phys, PAGE_SIZE * DA_HEADS, DA_V_DIM)
    pt = page_table.reshape(-1)

    kern = functools.partial(_attn_sample_kernel, layer=e, G=G, n_steps=n_steps, dec_seq=dec_seq,
                             lam_init=lam_init)
    grid_spec = pltpu.PrefetchScalarGridSpec(
        num_scalar_prefetch=1,
        grid=(DB, 2 * n_steps),
        in_specs=[
            pl.BlockSpec((4, DA_QK_DIM), lambda b, t, pt: (0, 0)),
            pl.BlockSpec((None, n_rows, DA_W), lambda b, t, pt: (b, 0, 0)),
            pl.BlockSpec((None, DA_W, PAGE_SIZE), lambda b, t, pt: (b, 0, 0)),
            pl.BlockSpec((dec_seq, DA_W), lambda b, t, pt: (b, 1)),
            pl.BlockSpec((1, DA_V_DIM), lambda b, t, pt: (0, 0)),
            pl.BlockSpec(memory_space=pl.ANY),
            pl.BlockSpec(memory_space=pl.ANY),
        ],
        out_specs=pl.BlockSpec((dec_seq, DA_W), lambda b, t, pt: (b, 0)),
        scratch_shapes=[
            pltpu.VMEM((PAGE_RING, DA_W, PAGE_SIZE), F32),
            pltpu.SemaphoreType.DMA((PAGE_RING,)),
            pltpu.VMEM((n_pages + 1, n_rows, PAGE_SIZE), F32),
            pltpu.VMEM((n_rows, PAGE_SIZE), F32),
            pltpu.VMEM((n_rows, PAGE_SIZE), F32),
            pltpu.VMEM((n_rows, DA_W), F32),
            pltpu.VMEM((PAGE_SIZE, DA_W), F32),
        ],
    )
    return pl.pallas_call(
        kern,
        grid_spec=grid_spec,
        out_shape=jax.ShapeDtypeStruct((DB * dec_seq, DA_W), F32),
        compiler_params=_cparams("arbitrary", "arbitrary"),
        name="diff_attn_sample",
    )(pt, da_lambda_e, qbd, knt, proj_s, subln_e, ckt, cv)


def _bmm(a, b):
    return jnp.einsum("gmk,gkn->gmn", a.astype(BF16), b.astype(BF16), preferred_element_type=F32)


def _bmm_nt(a, b):
    return jnp.einsum("gmk,gnk->gmn", a.astype(BF16), b.astype(BF16), preferred_element_type=F32)


def _bmm3(a, b):
    ah, al = _split2(a)
    bh, bl = _split2(b)
    d = functools.partial(jnp.einsum, "gmk,gkn->gmn", preferred_element_type=F32)
    return d(ah, bh) + (d(ah, bl) + d(al, bh))


def _tri_inverse(m_strict, C):
    ti = lax.broadcasted_iota(jnp.int32, (1, C, C), 1)
    sj = lax.broadcasted_iota(jnp.int32, (1, C, C), 2)
    eye = (ti == sj).astype(F32)
    a = jnp.where(ti // SUBLANES == sj // SUBLANES, -m_strict, 0.0)
    x = eye + a
    p = _bmm3(a, a)
    x = x + _bmm3(x, p)
    p = _bmm3(p, p)
    x = x + _bmm3(x, p)
    b = SUBLANES
    while b < C:
        join = jnp.logical_and(ti // (2 * b) == sj // (2 * b), ti // b != sj // b)
        c = jnp.where(join, m_strict, 0.0)
        x = x - _bmm3(x, _bmm3(c, x))
        b *= 2
    return x


def _gdn_kernel(*refs, C, NCH, has_state):
    if has_state:
        (q_ref, k_ref, v_ref, z_ref, ab_ref, cw_ref, alog_ref, dtb_ref, nw_ref,
         prev_ref, s0_ref, o_ref, sout_ref, buf_ref, s_ref) = refs
    else:
        (q_ref, k_ref, v_ref, z_ref, ab_ref, cw_ref, alog_ref, dtb_ref, nw_ref,
         o_ref, sout_ref, buf_ref, s_ref) = refs
    ci = pl.program_id(1)
    W = GDN_HEADS * GDN_DK
    H = GDN_HEADS
    R = C * NCH
    G = H * NCH

    @pl.when(ci == 0)
    def _():
        if has_state:
            buf_ref[0:SUBLANES, :] = prev_ref[...]
            s_ref[...] = s0_ref[...]
        else:
            buf_ref[0:SUBLANES, :] = jnp.zeros((SUBLANES, GDN_QKV), F32)
            s_ref[...] = jnp.zeros(s_ref.shape, F32)

    buf_ref[SUBLANES:SUBLANES + R, 0:W] = q_ref[...]
    buf_ref[SUBLANES:SUBLANES + R, W:2 * W] = k_ref[...]
    buf_ref[SUBLANES:SUBLANES + R, 2 * W:3 * W] = v_ref[...]
    y = buf_ref[SUBLANES:SUBLANES + R, :] * cw_ref[GDN_CONV - 1:GDN_CONV, :]
    for back in range(1, GDN_CONV):
        y = y + buf_ref[pl.ds(SUBLANES - back, R), :] * cw_ref[GDN_CONV - 1 - back:GDN_CONV - back, :]
    tail = buf_ref[R:R + SUBLANES, :]
    buf_ref[0:SUBLANES, :] = tail
    y = _silu(y)

    def heads(x, off, width):
        return jnp.stack([x[:, off + h * width:off + (h + 1) * width] for h in range(H)]).reshape(G, C, width)

    def chunk(x, c):
        return x.reshape((H, NCH) + x.shape[1:])[:, c]

    ab = ab_ref[...]
    sp = ab + dtb_ref[...]
    softplus = jnp.maximum(sp, 0.0) + jnp.log(1.0 + jnp.exp(-jnp.abs(sp)))
    g_all = -jnp.exp(alog_ref[...]) * softplus
    beta_all = _sigmoid(ab)

    ti = lax.broadcasted_iota(jnp.int32, (1, C, C), 1)
    sj = lax.broadcasted_iota(jnp.int32, (1, C, C), 2)
    causal = sj <= ti
    strict = sj < ti
    eye = ti == sj
    tri = causal[0].astype(BF16)
    bc_all = jnp.concatenate([_cumsum_rows(tri, g_all[c * C:(c + 1) * C]) for c in range(NCH)], axis=0)

    q3 = heads(y, 0, GDN_DK)
    k3 = heads(y, W, GDN_DK)
    v3 = heads(y, 2 * W, GDN_DV)
    q3 = q3 * lax.rsqrt(jnp.sum(q3 * q3, axis=-1, keepdims=True) + NORM_EPS) * (GDN_DK ** -0.5)
    k3 = k3 * lax.rsqrt(jnp.sum(k3 * k3, axis=-1, keepdims=True) + NORM_EPS)
    bcol = heads(bc_all, 0, 1)
    beta = heads(beta_all, H, 1)
    brow = jnp.sum(jnp.where(eye, bcol, 0.0), axis=1, keepdims=True)
    dec = jnp.exp(jnp.where(causal, bcol - brow, NEG_BIG))
    kb = k3 * beta
    eb = jnp.exp(bcol)
    m_strict = jnp.where(strict, _bmm_nt(kb, k3) * dec, 0.0)
    tinv = _tri_inverse(m_strict, C)
    uw = _bmm(tinv, jnp.concatenate([v3 * beta, kb * eb], axis=2))
    attn = _bmm_nt(q3, k3) * dec
    blast = bcol[:, C - 1:C, :]
    kd = k3 * jnp.exp(blast - bcol)
    gl = jnp.exp(blast)
    wq = jnp.concatenate([uw[:, :, GDN_DV:], q3 * eb], axis=1)
    z3 = heads(z_ref[...], 0, GDN_DV)

    s = s_ref[...]
    for c in range(NCH):
        ws = _bmm(chunk(wq, c), s)
        v_new = chunk(uw, c)[:, :, :GDN_DV] - ws[:, :C]
        o = ws[:, C:] + _bmm(chunk(attn, c), v_new)
        s = s * chunk(gl, c) + _bmm(jnp.swapaxes(chunk(kd, c), 1, 2), v_new)
        o = _rms_rows(o, nw_ref[...]) * _silu(chunk(z3, c))
        for h in range(H):
            o_ref[c * C:(c + 1) * C, h * GDN_DV:(h + 1) * GDN_DV] = o[h].astype(o_ref.dtype)
    s_ref[...] = s

    @pl.when(ci == pl.num_programs(1) - 1)
    def _():
        sout_ref[...] = s


def _gdn(proj, ab, conv_w_e, alog_pad, dtb_pad, norm_e, B, L, C, NCH=1, prev8=None, s0=None, out_dtype=F32):
    has_state = prev8 is not None
    R = C * NCH
    nc = L // R
    W = GDN_HEADS * GDN_DK
    base = (2 * DA_W) // W
    row = lambda b, c: (b * nc + c, 0)
    in_specs = [
        pl.BlockSpec((R, W), lambda b, c: (b * nc + c, base)),
        pl.BlockSpec((R, W), lambda b, c: (b * nc + c, base + 1)),
        pl.BlockSpec((R, W), lambda b, c: (b * nc + c, base + 2)),
        pl.BlockSpec((R, W), lambda b, c: (b * nc + c, base + 3)),
        pl.BlockSpec((R, LANES), row),
        pl.BlockSpec((GDN_CONV, GDN_QKV), lambda b, c: (0, 0)),
        pl.BlockSpec((1, LANES), lambda b, c: (0, 0)),
        pl.BlockSpec((1, LANES), lambda b, c: (0, 0)),
        pl.BlockSpec((1, GDN_DV), lambda b, c: (0, 0)),
    ]
    args = [proj, proj, proj, proj, ab, conv_w_e, alog_pad, dtb_pad, norm_e]
    state_block = (None, GDN_HEADS, GDN_DK, GDN_DV)
    if has_state:
        in_specs += [pl.BlockSpec((None, SUBLANES, GDN_QKV), lambda b, c: (b, 0, 0)),
                     pl.BlockSpec(state_block, lambda b, c: (b, 0, 0, 0))]
        args += [prev8, s0]
    return pl.pallas_call(
        functools.partial(_gdn_kernel, C=C, NCH=NCH, has_state=has_state),
        grid=(B, nc),
        in_specs=in_specs,
        out_specs=[pl.BlockSpec((R, W), row),
                   pl.BlockSpec(state_block, lambda b, c: (b, 0, 0, 0))],
        out_shape=[jax.ShapeDtypeStruct((B * L, W), out_dtype),
                   jax.ShapeDtypeStruct((B, GDN_HEADS, GDN_DK, GDN_DV), F32)],
        scratch_shapes=[pltpu.VMEM((SUBLANES + R, GDN_QKV), F32),
                        pltpu.VMEM((GDN_HEADS, GDN_DK, GDN_DV), F32)],
        compiler_params=_cparams("parallel", "arbitrary"),
        name="gdn_state" if has_state else "gdn_fresh",
    )(*args)


def _sub_block_rows(x4, b):
    sub = lax.broadcasted_iota(jnp.int32, (1, 1, SUBLANES, 1), 2)
    out = None
    for start in range(0, SUBLANES, b):
        mid = start + b // 2
        piece = jnp.broadcast_to(x4[:, :, mid:mid + 1, :], x4.shape)
        out = piece if out is None else jnp.where(sub >= start, piece, out)
    return out


def _hgrn_kernel(*refs, C, HB, SB, layer, has_state):
    if has_state:
        (q_ref, f_ref, i_ref, g_ref, lower_ref, nw_ref, s0_ref, o_ref, sout_ref, st_ref) = refs
    else:
        (q_ref, f_ref, i_ref, g_ref, lower_ref, nw_ref, o_ref, sout_ref, st_ref) = refs
    ci = pl.program_id(2)
    G = HB * SB
    K = HG_EXPAND

    @pl.when(ci == 0)
    def _():
        for h in range(HB):
            for sb in range(SB):
                st_ref[h * SB + sb] = s0_ref[sb, h].T if has_state else jnp.zeros((HG_DI, K), F32)

    def heads(x):
        return jnp.stack([x[:, h * K:(h + 1) * K] for h in range(HB)]).reshape(G, C, K)

    low = lower_ref[...]
    ex = jnp.exp(low - jnp.max(low, axis=0, keepdims=True))
    lb = jnp.sum(ex[1:layer + 1], axis=0, keepdims=True) / jnp.sum(ex, axis=0, keepdims=True)

    fg2 = lb + (1.0 - lb) * _sigmoid(f_ref[...])
    logf2 = jnp.log(fg2)
    ti = lax.broadcasted_iota(jnp.int32, (1, C, C), 1)
    sj = lax.broadcasted_iota(jnp.int32, (1, C, C), 2)
    tri = (sj[0] <= ti[0]).astype(BF16)
    bc2 = jnp.concatenate([_cumsum_rows(tri, logf2[sb * C:(sb + 1) * C]) for sb in range(SB)], axis=0)

    qt = heads(_silu(q_ref[...]) * (K ** -0.5))
    kk = heads(1.0 - fg2)
    v = heads(i_ref[...])
    bc = heads(bc2)
    st = st_ref[...]
    o = _bmm_nt(qt * jnp.exp(bc), st)

    ri = lax.broadcasted_iota(jnp.int32, (1, C, 1), 1)
    a = jnp.where(ti == sj, jnp.sum(qt * kk, axis=-1, keepdims=True), 0.0)
    bc4 = bc.reshape(G, C // SUBLANES, SUBLANES, K)
    b = 2
    while b <= C:
        if b <= SUBLANES:
            ref = _sub_block_rows(bc4, b).reshape(G, C, K)
        else:
            ref = jnp.concatenate(
                [jnp.broadcast_to(bc[:, m:m + 1, :], (G, b, K)) for m in range(b // 2, C, b)], axis=1)
        upper = (ri % b) >= (b // 2)
        d = bc - ref
        e = jnp.exp(jnp.where(upper, d, -d))
        qs = jnp.where(upper, qt * e, 0.0)
        ks = jnp.where(upper, 0.0, kk * e)
        a = a + jnp.where(ti // b == sj // b, _bmm_nt(qs, ks), 0.0)
        b *= 2
    o = o + _bmm(a, v)

    blast = bc[:, C - 1:C, :]
    st = st * jnp.exp(blast) + _bmm(jnp.swapaxes(v, 1, 2), kk * jnp.exp(blast - bc))
    st_ref[...] = st
    nw = jnp.stack([nw_ref[:, h * K:(h + 1) * K] for h in range(HB) for _ in range(SB)])
    o = _rms_rows(o, nw) * heads(_silu(g_ref[...]))
    for h in range(HB):
        for sb in range(SB):
            o_ref[sb * C:(sb + 1) * C, h * K:(h + 1) * K] = o[h * SB + sb].astype(o_ref.dtype)

    @pl.when(ci == pl.num_programs(2) - 1)
    def _():
        for h in range(HB):
            for sb in range(SB):
                sout_ref[sb, h] = st[h * SB + sb].T


def _hgrn(proj, hg_lower, hg_norm_o, layer, B, L, C, s0=None, HB=16, SB=1):
    has_state = s0 is not None
    nc = L // C
    assert SB == 1 or nc == 1
    nhg = HG_HEADS // HB
    Wb = HB * HG_EXPAND
    R = SB * C

    def col(section):
        return lambda b, hg, c: (b * nc + c, section * nhg + hg)

    in_specs = [pl.BlockSpec((R, Wb), col(s)) for s in range(4)] + [
        pl.BlockSpec((DEPTH, Wb), lambda b, hg, c: (0, hg)),
        pl.BlockSpec((1, Wb), lambda b, hg, c: (0, hg)),
    ]
    args = [proj, proj, proj, proj, hg_lower, hg_norm_o]
    state_block = (SB, HB, HG_EXPAND, HG_DI)
    if has_state:
        in_specs.append(pl.BlockSpec(state_block, lambda b, hg, c: (b, hg, 0, 0)))
        args.append(s0)
    return pl.pallas_call(
        functools.partial(_hgrn_kernel, C=C, HB=HB, SB=SB, layer=layer, has_state=has_state),
        grid=(B // SB, nhg, nc),
        in_specs=in_specs,
        out_specs=[pl.BlockSpec((R, Wb), col(0)),
                   pl.BlockSpec(state_block, lambda b, hg, c: (b, hg, 0, 0))],
        out_shape=[jax.ShapeDtypeStruct((B * L, D_MODEL), BF16),
                   jax.ShapeDtypeStruct((B, HG_HEADS, HG_EXPAND, HG_DI), F32)],
        scratch_shapes=[pltpu.VMEM((HB * SB, HG_DI, HG_EXPAND), F32)],
        compiler_params=_cparams("parallel", "parallel", "arbitrary"),
        name="hgrn_state" if has_state else "hgrn_fresh",
    )(*args)


def _pad_lanes(v, offset=0):
    return jnp.zeros((1, LANES), F32).at[0, offset:offset + v.shape[0]].set(v.astype(F32))


def kernel(x_prompt, x_sample, cache_k, cache_v, state_gdn_conv, state_gdn, state_hgrn, page_table, norm_mix, norm_ffn, norm_final, w_in_even, w_out_even, da_lambda, da_subln, gdn_conv_w, gdn_a_log, gdn_dt_bias, gdn_norm, w_in_odd, w_out_odd, hg_lower, hg_norm, w_gate_up, w_down):
    Bp, Lp = x_prompt.shape[:2]
    DB, Ls = x_sample.shape[:2]
    past_len = page_table.shape[1] * PAGE_SIZE
    pos_p = np.tile(np.arange(Lp), Bp)
    pos_s = np.tile(past_len + np.arange(Ls), DB)
    Mp, Ms = Bp * Lp, DB * Ls
    tm_p, tm_s = 1024, Ms

    hp = x_prompt.reshape(Mp, D_MODEL)
    hs = x_sample.reshape(Ms, D_MODEL)
    outs = {k: [] for k in ("k_p", "v_p", "conv_p", "gdn_p", "hg_p", "k_s", "v_s", "conv_s", "gdn_s", "hg_s")}

    for l in range(DEPTH):
        nw_mix = norm_mix[l].reshape(1, D_MODEL)
        nw_ffn = norm_ffn[l].reshape(1, D_MODEL)
        if l % 2 == 0:
            e = l // 2
            lam_init = 0.8 - 0.6 * math.exp(-0.3 * l)
            w_in_t = jnp.transpose(w_in_even, (0, 2, 1))
            wab = jnp.pad(w_in_t[e, EVEN_MAIN:], ((0, LANES - 2 * GDN_HEADS), (0, 0)))
            alog_pad = _pad_lanes(gdn_a_log[e])
            dtb_pad = _pad_lanes(gdn_dt_bias[e])
            sub = da_subln[e].reshape(1, DA_V_DIM)
            gnorm = gdn_norm[e].reshape(1, GDN_DV)
            conv_cols = slice(2 * DA_W, 2 * DA_W + GDN_QKV)
            proj_p, kt_p, ab_p, proj_s, kt_s, ab_s = _inproj_even(
                hp, hs, nw_mix, w_in_t, e, wab, pos_p, pos_s, Bp, tm_p)
            oa_p = _attn_prompt(proj_p, kt_p, da_lambda[e], sub, Bp, Lp, lam_init)
            gdn_c = min(GDN_CHUNK, Lp)
            ob_p, st = _gdn(proj_p, ab_p, gdn_conv_w[e], alog_pad, dtb_pad, gnorm, Bp, Lp, gdn_c,
                            NCH=4 if Lp % (4 * gdn_c) == 0 else 1, out_dtype=BF16)
            outs["k_p"].append(kt_p.reshape(Bp, DA_HEADS, 2, DA_QK_DIM, Lp).transpose(0, 4, 1, 2, 3))
            outs["v_p"].append(proj_p[:, DA_W:2 * DA_W].reshape(Bp, Lp, DA_HEADS, DA_V_DIM))
            outs["conv_p"].append(proj_p.reshape(Bp, Lp, EVEN_ROWS)[:, Lp - (GDN_CONV - 1):, conv_cols])
            outs["gdn_p"].append(st)
            oa_s = _attn_sample(proj_s, kt_s[0], cache_k, cache_v, e, page_table, da_lambda[e], sub,
                                DB, Ls, lam_init)
            prev8 = jnp.pad(state_gdn_conv[e], ((0, 0), (SUBLANES - (GDN_CONV - 1), 0), (0, 0)))
            ob_s, st = _gdn(proj_s, ab_s, gdn_conv_w[e], alog_pad, dtb_pad, gnorm, DB, Ls, min(GDN_CHUNK, Ls),
                            prev8=prev8, s0=state_gdn[e])
            gq = proj_s[:, conv_cols].reshape(DB, Ls, GDN_QKV)
            conv_s = jnp.concatenate([state_gdn_conv[e], gq], axis=1)[:, Ls:]
            outs["k_s"].append(kt_s[0].reshape(DA_HEADS, 2, DA_QK_DIM, DB, Ls).transpose(3, 4, 0, 1, 2))
            outs["v_s"].append(proj_s[:, DA_W:2 * DA_W].reshape(DB, Ls, DA_HEADS, DA_V_DIM))
            outs["conv_s"].append(conv_s)
            outs["gdn_s"].append(st)
            hp, hs = _mm_res([oa_p, ob_p], [oa_s, ob_s], w_out_even, e, hp, hs, tm_p)
        else:
            o = l // 2
            hnorm = hg_norm[o].reshape(1, D_MODEL)
            proj_p, proj_s = _norm_matmul(hp, hs, nw_mix, w_in_odd, o, tm_p)
            y_p, st = _hgrn(proj_p, hg_lower, hnorm, l, Bp, Lp, min(HG_CHUNK, Lp))
            outs["hg_p"].append(st)
            y_s, st = _hgrn(proj_s, hg_lower, hnorm, l, DB, Ls, min(HG_CHUNK, Ls), s0=state_hgrn[o],
                            SB=4 if Ls <= HG_CHUNK and DB % 4 == 0 else 1, HB=8)
            outs["hg_s"].append(st)
            hp, hs = _mm_res([y_p], [y_s], w_out_odd, o, hp, hs, tm_p)
        act_p, act_s = _ffn_up(hp, hs, nw_ffn, w_gate_up, l, tm_p)
        hp, hs = _mm_res([act_p], [act_s], w_down, l, hp, hs, tm_p, tk=D_FF // 2)

    nwf = norm_final.reshape(1, D_MODEL)
    y_prompt = _final_norm(hp, nwf, tm_p).reshape(Bp, Lp, D_MODEL)
    y_sample = _final_norm(hs, nwf, tm_s).reshape(DB, Ls, D_MODEL)
    st = lambda k: jnp.stack(outs[k])
    return (y_prompt, y_sample, st("k_p"), st("v_p"), st("conv_p"), st("gdn_p"), st("hg_p"),
            st("k_s"), st("v_s"), st("conv_s"), st("gdn_s"), st("hg_s"))
```
